```python
import math
import jax, jax.numpy as jnp
from jax import lax
import numpy as np

D_MODEL = 1024
BATCH = 4
SEQ = 8192
DEPTH = 2

CHUNK = 64
N_META = 16
Q_BLOCK = 128
N_MIXERS = 2
MLA_HEADS = 16
MLA_Q_RANK = 384
MLA_KV_RANK = 256
MLA_NOPE = 64
MLA_ROPE = 32
MLA_V = 64
MLA_QK = MLA_NOPE + MLA_ROPE
ROPE_THETA = 10000.0
GLA_HEADS = 4
GLA_DK = D_MODEL // 2 // GLA_HEADS
GLA_DV = D_MODEL // GLA_HEADS
GLA_GATE_RANK = 16
GLA_TAU = 16.0
N_GROUPS = 4
EXPERTS_PER_GROUP = 8
N_EXPERTS = N_GROUPS * EXPERTS_PER_GROUP
TOP_K = 2
D_EXPERT = 256
EPS = 1e-6

N_MLA_LAYERS = (DEPTH + 1) // 2
N_GLA_LAYERS = DEPTH // 2
MLA_IN = MLA_Q_RANK + MLA_KV_RANK + MLA_ROPE
GLA_NK = GLA_HEADS * GLA_DK
GLA_NV = GLA_HEADS * GLA_DV
GLA_IN = 2 * GLA_NK + GLA_NV + GLA_GATE_RANK + GLA_NV

kernel_name = "hybrid_mla_gla_hmoe_streaming"


def rms_norm(x, g):
    xf = x.astype(jnp.float32)
    y = xf * lax.rsqrt(jnp.mean(xf * xf, axis=-1, keepdims=True) + EPS)
    return (y * g.astype(jnp.float32)).astype(x.dtype)


def chunk_ids(n):
    p = jnp.arange(n)
    return jnp.where(p < N_META, 0, 1 + (p - N_META) // CHUNK)


def rope(x, pos):
    half = x.shape[-1] // 2
    inv = 1.0 / (ROPE_THETA ** (jnp.arange(half, dtype=jnp.float32) / half))
    ang = pos.astype(jnp.float32)[:, None] * inv[None, :]
    cos = jnp.cos(ang)[:, None, :]
    sin = jnp.sin(ang)[:, None, :]
    xf = x.astype(jnp.float32)
    x1, x2 = xf[..., :half], xf[..., half:]
    return jnp.concatenate([x1 * cos - x2 * sin, x2 * cos + x1 * sin], axis=-1).astype(x.dtype)


def block_causal_attention(q, k, v):
    B, L, H, dh = q.shape
    nblk = L // Q_BLOCK
    cid = chunk_ids(L)
    scale = dh ** -0.5
    qb = q.reshape(B, nblk, Q_BLOCK, H, dh).transpose(1, 0, 3, 2, 4)
    cb = cid.reshape(nblk, Q_BLOCK)
    kt = k.transpose(0, 2, 1, 3)
    vt = v.transpose(0, 2, 1, 3)

    def one_block(args):
        q_blk, c_blk = args
        s = jnp.einsum('bhqd,bhkd->bhqk', q_blk, kt,
                       preferred_element_type=jnp.float32) * scale
        mask = c_blk[:, None] >= cid[None, :]
        s = jnp.where(mask[None, None], s, -jnp.inf)
        p = jax.nn.softmax(s, axis=-1)
        return jnp.einsum('bhqk,bhkd->bhqd', p.astype(vt.dtype), vt)

    o = lax.map(one_block, (qb, cb))
    return o.transpose(1, 0, 3, 2, 4).reshape(B, L, H, -1)


def mla_mixer(h, w_in, g_q, w_uq, g_kv, w_ukv, g_qn, g_kn, w_o):
    B, L, _ = h.shape
    z = h @ w_in
    c_q, c_kv, k_r = jnp.split(z, [MLA_Q_RANK, MLA_Q_RANK + MLA_KV_RANK], axis=-1)
    q = (rms_norm(c_q, g_q) @ w_uq).reshape(B, L, MLA_HEADS, MLA_QK)
    kv = (rms_norm(c_kv, g_kv) @ w_ukv).reshape(B, L, MLA_HEADS, MLA_NOPE + MLA_V)
    k_nope, v = jnp.split(kv, [MLA_NOPE], axis=-1)
    k_r = jnp.broadcast_to(k_r[:, :, None, :], (B, L, MLA_HEADS, MLA_ROPE))
    k = jnp.concatenate([k_nope, k_r], axis=-1)
    q = rms_norm(q, g_qn)
    k = rms_norm(k, g_kn)
    pos = jnp.arange(L)
    q = jnp.concatenate([q[..., :MLA_NOPE], rope(q[..., MLA_NOPE:], pos)], axis=-1)
    k = jnp.concatenate([k[..., :MLA_NOPE], rope(k[..., MLA_NOPE:], pos)], axis=-1)
    o = block_causal_attention(q, k, v)
    return o.reshape(B, L, MLA_HEADS * MLA_V) @ w_o


def gla_mixer(h, w_in, w_gate_up, b_gate, g_out, w_o):
    B, L, _ = h.shape
    f32 = jnp.float32
    z = h @ w_in
    q, k, v, g_lr, r = jnp.split(
        z, [GLA_NK, 2 * GLA_NK, 2 * GLA_NK + GLA_NV, 2 * GLA_NK + GLA_NV + GLA_GATE_RANK], axis=-1)
    log_a = jax.nn.log_sigmoid((g_lr @ w_gate_up + b_gate).astype(f32)) / GLA_TAU
    n = L // CHUNK

    def heads(t, d):
        return t.astype(f32).reshape(B, n, CHUNK, GLA_HEADS, d).transpose(1, 0, 3, 2, 4)

    qc = heads(q, GLA_DK) * (GLA_DK ** -0.5)
    kc = heads(k, GLA_DK)
    vc = heads(v, GLA_DV)
    gc = heads(log_a, GLA_DK)
    tri = jnp.tril(jnp.ones((CHUNK, CHUNK), dtype=bool))

    def step(S, inp):
        qq, kk, vv, gg = inp
        b = jnp.cumsum(gg, axis=-2)
        o_inter = jnp.einsum('bhck,bhkv->bhcv', qq * jnp.exp(b), S)
        diff = b[:, :, :, None, :] - b[:, :, None, :, :]
        decay = jnp.exp(jnp.where(tri[:, :, None], diff, -jnp.inf))
        att = jnp.einsum('bhtk,bhtsk,bhsk->bhts', qq, decay, kk)
        o = o_inter + jnp.einsum('bhts,bhsv->bhtv', att, vv)
        b_last = b[:, :, -1:, :]
        S = jnp.exp(b_last[:, :, 0, :, None]) * S + jnp.einsum(
            'bhck,bhcv->bhkv', kk * jnp.exp(b_last - b), vv)
        return S, o

    S0 = jnp.zeros((B, GLA_HEADS, GLA_DK, GLA_DV), f32)
    _, o = lax.scan(step, S0, (qc, kc, vc, gc))
    o = o.transpose(1, 0, 3, 2, 4).reshape(B, L, GLA_HEADS, GLA_DV)
    o = rms_norm(o, g_out).astype(h.dtype).reshape(B, L, GLA_NV)
    o = o * jax.nn.silu(r)
    return o @ w_o


def hier_moe(h, w_rg, b_rg, w_re, b_re, w_gate, w_up, w_down):
    B, L, D = h.shape
    f32 = jnp.float32
    t = h.reshape(-1, D)
    g_logits = (t @ w_rg).astype(f32) + b_rg.astype(f32)
    g_prob = jax.nn.softmax(g_logits, axis=-1)
    g_sel = jnp.argmax(g_logits, axis=-1)
    p_g = jnp.take_along_axis(g_prob, g_sel[:, None], axis=-1)
    e_logits = ((t @ w_re).astype(f32) + b_re.astype(f32)).reshape(-1, N_GROUPS, EXPERTS_PER_GROUP)
    e_logits = jnp.take_along_axis(e_logits, g_sel[:, None, None], axis=1)[:, 0]
    e_prob = jax.nn.softmax(e_logits, axis=-1)
    top_p, top_i = lax.top_k(e_prob, TOP_K)
    top_w = top_p / jnp.sum(top_p, axis=-1, keepdims=True) * p_g
    expert_id = g_sel[:, None] * EXPERTS_PER_GROUP + top_i
    gates = jnp.einsum('tk,tke->te', top_w, jax.nn.one_hot(expert_id, N_EXPERTS, dtype=f32))
    out = jnp.zeros(t.shape, f32)
    for e in range(N_EXPERTS):
        hid = jax.nn.silu(t @ w_gate[e]) * (t @ w_up[e])
        out = out + gates[:, e:e + 1] * (hid @ w_down[e]).astype(f32)
    return out.astype(h.dtype).reshape(B, L, D)


def setup_inputs(seed: int = 0) -> dict:
    key = jax.random.key(seed)
    ks = iter(jax.random.split(key, 40))
    f32 = jnp.float32

    def dense(shape, fan_in):
        return jax.random.normal(next(ks), shape, f32) * (fan_in ** -0.5)

    def gain(shape):
        return 1.0 + 0.01 * jax.random.normal(next(ks), shape, f32)

    def bias(shape, s):
        return s * jax.random.normal(next(ks), shape, f32)

    nm, ng = N_MLA_LAYERS, N_GLA_LAYERS
    return {
        "x": jax.random.normal(next(ks), (BATCH, SEQ, D_MODEL), f32),
        "meta_tokens": jax.random.normal(next(ks), (N_META, D_MODEL), f32),
        "norm_mix": gain((DEPTH, D_MODEL)),
        "norm_ffn": gain((DEPTH, D_MODEL)),
        "mla_w_in": dense((nm, D_MODEL, MLA_IN), D_MODEL),
        "mla_g_q": gain((nm, MLA_Q_RANK)),
        "mla_w_uq": dense((nm, MLA_Q_RANK, MLA_HEADS * MLA_QK), MLA_Q_RANK),
        "mla_g_kv": gain((nm, MLA_KV_RANK)),
        "mla_w_ukv": dense((nm, MLA_KV_RANK, MLA_HEADS * (MLA_NOPE + MLA_V)), MLA_KV_RANK),
        "mla_g_qn": gain((nm, MLA_QK)),
        "mla_g_kn": gain((nm, MLA_QK)),
        "mla_w_o": dense((nm, MLA_HEADS * MLA_V, D_MODEL), MLA_HEADS * MLA_V),
        "gla_w_in": dense((ng, D_MODEL, GLA_IN), D_MODEL),
        "gla_w_gate_up": dense((ng, GLA_GATE_RANK, GLA_NK), GLA_GATE_RANK),
        "gla_b_gate": bias((ng, GLA_NK), 0.1),
        "gla_g_out": gain((ng, GLA_DV)),
        "gla_w_o": dense((ng, GLA_NV, D_MODEL), GLA_NV),
        "moe_w_rg": dense((DEPTH, D_MODEL, N_GROUPS), D_MODEL),
        "moe_b_rg": bias((DEPTH, N_GROUPS), 0.01),
        "moe_w_re": dense((DEPTH, D_MODEL, N_EXPERTS), D_MODEL),
        "moe_b_re": bias((DEPTH, N_EXPERTS), 0.01),
        "moe_w_gate": dense((DEPTH, N_EXPERTS, D_MODEL, D_EXPERT), D_MODEL),
        "moe_w_up": dense((DEPTH, N_EXPERTS, D_MODEL, D_EXPERT), D_MODEL),
        "moe_w_down": dense((DEPTH, N_EXPERTS, D_EXPERT, D_MODEL), D_EXPERT),
    }


def reference(x, meta_tokens, norm_mix, norm_ffn,
              mla_w_in, mla_g_q, mla_w_uq, mla_g_kv, mla_w_ukv, mla_g_qn, mla_g_kn, mla_w_o,
              gla_w_in, gla_w_gate_up, gla_b_gate, gla_g_out, gla_w_o,
              moe_w_rg, moe_b_rg, moe_w_re, moe_b_re, moe_w_gate, moe_w_up, moe_w_down):
    B, S, D = x.shape
    L = N_META + S
    Lp = ((L + Q_BLOCK - 1) // Q_BLOCK) * Q_BLOCK
    meta = jnp.broadcast_to(meta_tokens.astype(x.dtype)[None], (B, N_META, D))
    h = jnp.concatenate([meta, x, jnp.zeros((B, Lp - L, D), x.dtype)], axis=1)
    for i in range(DEPTH):
        j = i // N_MIXERS
        hn = rms_norm(h, norm_mix[i])
        if i % N_MIXERS == 0:
            h = h + mla_mixer(hn, mla_w_in[j], mla_g_q[j], mla_w_uq[j], mla_g_kv[j],
                              mla_w_ukv[j], mla_g_qn[j], mla_g_kn[j], mla_w_o[j])
        else:
            h = h + gla_mixer(hn, gla_w_in[j], gla_w_gate_up[j], gla_b_gate[j],
                              gla_g_out[j], gla_w_o[j])
        h = h + hier_moe(rms_norm(h, norm_ffn[i]), moe_w_rg[i], moe_b_rg[i], moe_w_re[i],
                         moe_b_re[i], moe_w_gate[i], moe_w_up[i], moe_w_down[i])
    return h[:, N_META:N_META + S]
```

```python
import functools

import jax
import jax.numpy as jnp
from jax import lax
from jax.experimental import pallas as pl
from jax.experimental.pallas import tpu as pltpu

F32 = jnp.float32
BF16 = jnp.bfloat16

D_MODEL = 1024
CHUNK = 64
N_META = 16
MLA_HEADS = 16
MLA_Q_RANK = 384
MLA_KV_RANK = 256
MLA_NOPE = 64
MLA_ROPE = 32
MLA_V = 64
MLA_QK = MLA_NOPE + MLA_ROPE
ROPE_THETA = 10000.0
GLA_HEADS = 4
GLA_DK = 128
GLA_DV = 256
GLA_GATE_RANK = 16
GLA_TAU = 16.0
N_GROUPS = 4
EXPERTS_PER_GROUP = 8
N_EXPERTS = N_GROUPS * EXPERTS_PER_GROUP
D_EXPERT = 256
EPS = 1e-6

LANES = 128
META_BLOCK = 128
PAD_ROWS = META_BLOCK - N_META
NEG = -1e30

TOK_TILE = 512
ATT_TILE = 512
SCAN_CHUNK = 64
EXP_TILE = 256
VMEM_LIMIT = 56 * 1024 * 1024

GLA_LEVELS = (32, 16, 8, 4, 2, 1)


def _cparams(sem):
    return pltpu.CompilerParams(dimension_semantics=sem, vmem_limit_bytes=VMEM_LIMIT)


def _nt_dot(a, b):
    return lax.dot_general(a, b, (((1,), (1,)), ((), ())), preferred_element_type=F32)


def _tn_dot(a, b):
    return lax.dot_general(a, b, (((0,), (0,)), ((), ())), preferred_element_type=F32)


def _mla_proj_kernel(h_ref, winT_ref, wuqT_ref, wkT_ref, wv_ref, vone_ref, gq_ref, gk_ref,
                     cos_ref, sin_ref, q_ref, kT_ref, v_ref):
    tt = h_ref.shape[0]
    h = h_ref[...]
    hn = (h * lax.rsqrt(jnp.mean(h * h, axis=-1, keepdims=True) + EPS)).astype(BF16)
    zT = _nt_dot(winT_ref[...], hn)
    cq = zT[0:MLA_Q_RANK]
    ckv = zT[MLA_Q_RANK:MLA_Q_RANK + MLA_KV_RANK]
    kr = zT[MLA_Q_RANK + MLA_KV_RANK:]
    cqn = (cq * lax.rsqrt(jnp.mean(cq * cq, axis=0, keepdims=True) + EPS)).astype(BF16)
    ckvn = ckv * lax.rsqrt(jnp.mean(ckv * ckv, axis=0, keepdims=True) + EPS)
    qT = jnp.dot(wuqT_ref[...], cqn, preferred_element_type=F32)
    kT = jnp.dot(wkT_ref[...], ckvn.astype(BF16), preferred_element_type=F32)
    v = jnp.dot(ckvn.T.astype(BF16), wv_ref[...], preferred_element_type=F32) + vone_ref[...]
    cos = cos_ref[...]
    sin = sin_ref[...]
    gq = gq_ref[...]
    gk = gk_ref[...]
    half = MLA_ROPE // 2
    zpad = jnp.zeros((LANES - MLA_QK, tt), F32)
    kr_ss = jnp.sum(kr * kr, axis=0, keepdims=True)
    for hh in range(MLA_HEADS):
        qh = qT[MLA_QK * hh:MLA_QK * (hh + 1)]
        qn = qh * lax.rsqrt(jnp.sum(qh * qh, axis=0, keepdims=True) * (1.0 / MLA_QK) + EPS) * gq
        x1 = qn[MLA_NOPE:MLA_NOPE + half]
        x2 = qn[MLA_NOPE + half:]
        qrot = jnp.concatenate([qn[:MLA_NOPE], x1 * cos - x2 * sin, x2 * cos + x1 * sin, zpad], axis=0)
        q_ref[hh] = qrot.T.astype(BF16)
        kn = kT[MLA_NOPE * hh:MLA_NOPE * (hh + 1)]
        rk = lax.rsqrt((jnp.sum(kn * kn, axis=0, keepdims=True) + kr_ss) * (1.0 / MLA_QK) + EPS)
        knn = kn * rk * gk[:MLA_NOPE]
        krn = kr * rk * gk[MLA_NOPE:]
        y1 = krn[:half]
        y2 = krn[half:]
        kfull = jnp.concatenate([knn, y1 * cos - y2 * sin, y2 * cos + y1 * sin, zpad], axis=0).astype(BF16)
        for u in range(tt // LANES):
            kT_ref[hh, u] = kfull[:, LANES * u:LANES * (u + 1)]
        v_ref[hh] = v[:, LANES * hh:LANES * (hh + 1)].astype(BF16)


def _mla_proj(h, winT, wuqT, wkT, wv, vone, gq, gk, cos, sin):
    T = h.shape[0]
    tt = TOK_TILE
    H = MLA_HEADS
    const = lambda shape: pl.BlockSpec(shape, lambda i: (0,) * len(shape))
    return pl.pallas_call(
        _mla_proj_kernel,
        grid=(T // tt,),
        in_specs=[
            pl.BlockSpec((tt, D_MODEL), lambda i: (i, 0)),
            const(winT.shape), const(wuqT.shape), const(wkT.shape), const(wv.shape), const(vone.shape),
            const(gq.shape), const(gk.shape),
            pl.BlockSpec((MLA_ROPE // 2, tt), lambda i: (0, i)),
            pl.BlockSpec((MLA_ROPE // 2, tt), lambda i: (0, i)),
        ],
        out_specs=[
            pl.BlockSpec((H, tt, LANES), lambda i: (0, i, 0)),
            pl.BlockSpec((H, tt // LANES, LANES, LANES), lambda i: (0, i, 0, 0)),
            pl.BlockSpec((H, tt, LANES), lambda i: (0, i, 0)),
        ],
        out_shape=[
            jax.ShapeDtypeStruct((H, T, LANES), BF16),
            jax.ShapeDtypeStruct((H, T // LANES, LANES, LANES), BF16),
            jax.ShapeDtypeStruct((H, T, LANES), BF16),
        ],
        compiler_params=_cparams(("arbitrary",)),
        name="mla_proj",
    )(h, winT, wuqT, wkT, wv, vone, gq, gk, cos, sin)


def _attn_kernel(q_ref, kT_ref, v_ref, o_ref, *, nf):
    tq = ATT_TILE
    nsub = tq // LANES
    row = lax.broadcasted_iota(jnp.int32, (tq, tq), 0)
    col = lax.broadcasted_iota(jnp.int32, (tq, tq), 1)
    diag_bias = jnp.where((col // CHUNK) <= (row // CHUNK), 0.0, NEG).astype(F32)
    col0 = lax.broadcasted_iota(jnp.int32, (1, LANES), 1)
    bias0 = jnp.where(col0 >= PAD_ROWS, 0.0, NEG).astype(F32)

    def step(q, kt, vv, m, acc, bias):
        s = jnp.dot(q, kt, preferred_element_type=F32)
        if bias is not None:
            s = s + bias
        m_new = jnp.maximum(m, jnp.max(s, axis=-1, keepdims=True))
        p = jnp.exp(s - m_new)
        acc = jnp.exp(m - m_new) * acc + jnp.dot(p.astype(BF16), vv, preferred_element_type=F32)
        return m_new, acc

    def finish(acc):
        return (acc[:, :MLA_V] / acc[:, MLA_V:MLA_V + 1]).astype(BF16)

    for hh in range(2):
        k0 = kT_ref[hh, 0]
        v0 = v_ref[hh, 0:META_BLOCK, :]
        cs = slice(MLA_V * hh, MLA_V * (hh + 1))

        m, acc = step(q_ref[hh, 0:META_BLOCK, :], k0, v0,
                      jnp.full((META_BLOCK, 1), NEG, F32), jnp.zeros((META_BLOCK, LANES), F32), bias0)
        o_ref[0:META_BLOCK, cs] = finish(acc)

        def ktile(blk):
            return jnp.concatenate([kT_ref[hh, blk + u] for u in range(nsub)], axis=1)

        def qtile(i, carry):
            r0 = pl.multiple_of(META_BLOCK + i * tq, LANES)
            q = q_ref[hh, pl.ds(r0, tq), :]
            m, acc = step(q, k0, v0, jnp.full((tq, 1), NEG, F32), jnp.zeros((tq, LANES), F32), bias0)

            def kv(j, c):
                c0 = pl.multiple_of(META_BLOCK + j * tq, LANES)
                return step(q, ktile(1 + j * nsub), v_ref[hh, pl.ds(c0, tq), :], c[0], c[1], None)

            m, acc = lax.fori_loop(0, i, kv, (m, acc))
            m, acc = step(q, ktile(1 + i * nsub), v_ref[hh, pl.ds(r0, tq), :], m, acc, diag_bias)
            o_ref[pl.ds(r0, tq), cs] = finish(acc)
            return carry

        lax.fori_loop(0, nf, qtile, 0)


def _attention(q, kT, v, B, LK):
    H, T, _ = q.shape
    nf = (LK - META_BLOCK) // ATT_TILE
    nblk = LK // LANES
    return pl.pallas_call(
        functools.partial(_attn_kernel, nf=nf),
        grid=(B, H // 2),
        in_specs=[
            pl.BlockSpec((2, LK, LANES), lambda b, hp: (hp, b, 0)),
            pl.BlockSpec((2, nblk, LANES, LANES), lambda b, hp: (hp, b, 0, 0)),
            pl.BlockSpec((2, LK, LANES), lambda b, hp: (hp, b, 0)),
        ],
        out_specs=pl.BlockSpec((LK, 2 * MLA_V), lambda b, hp: (b, hp)),
        out_shape=jax.ShapeDtypeStruct((T, H * MLA_V), BF16),
        compiler_params=_cparams(("arbitrary", "arbitrary")),
        name="mla_attention",
    )(q, kT, v)


def _route(hnew, gffn_ref, wrh_ref, wrl_ref, br_ref, tri_ref, hn_ref, info_ref, cnt_ref, carry_ref):
    tt = hnew.shape[0]
    hn = hnew * lax.rsqrt(jnp.mean(hnew * hnew, axis=-1, keepdims=True) + EPS) * gffn_ref[...]
    hn_ref[...] = hn
    hn_hi = hn.astype(BF16)
    hn_lo = (hn - hn_hi.astype(F32)).astype(BF16)
    wrh = wrh_ref[...]
    logits = (jnp.dot(hn_hi, wrh, preferred_element_type=F32)
              + jnp.dot(hn_lo, wrh, preferred_element_type=F32)
              + jnp.dot(hn_hi, wrl_ref[...], preferred_element_type=F32)) + br_ref[...]

    lane = lax.broadcasted_iota(jnp.int32, (tt, LANES), 1)
    lane_f = lane.astype(F32)
    big = float(LANES)
    gmask = (lane >= N_EXPERTS) & (lane < N_EXPERTS + N_GROUPS)
    gl = jnp.where(gmask, logits, NEG)
    gmax = jnp.max(gl, axis=-1, keepdims=True)
    gsel = jnp.min(jnp.where(gl == gmax, lane_f, big), axis=-1, keepdims=True) - float(N_EXPERTS)
    pg = 1.0 / jnp.sum(jnp.exp(gl - gmax), axis=-1, keepdims=True)
    egrp = (lane // EXPERTS_PER_GROUP).astype(F32)
    emask = (lane < N_EXPERTS) & (egrp == gsel)
    el = jnp.where(emask, logits, NEG)
    m1 = jnp.max(el, axis=-1, keepdims=True)
    i1 = jnp.min(jnp.where(el == m1, lane_f, big), axis=-1, keepdims=True)
    el2 = jnp.where(lane_f == i1, NEG, el)
    m2 = jnp.max(el2, axis=-1, keepdims=True)
    i2 = jnp.min(jnp.where(el2 == m2, lane_f, big), axis=-1, keepdims=True)
    t21 = jnp.exp(m2 - m1)
    w1 = pg / (1.0 + t21)
    w2 = w1 * t21

    sel1 = lane_f == i1
    sel2 = lane_f == i2
    oh = jnp.where(sel1 | sel2, 1.0, 0.0)
    before = jnp.dot(tri_ref[...], oh.astype(BF16), preferred_element_type=F32) + carry_ref[...]
    rank1 = jnp.sum(jnp.where(sel1, before, 0.0), axis=-1, keepdims=True)
    rank2 = jnp.sum(jnp.where(sel2, before, 0.0), axis=-1, keepdims=True)
    carry_ref[...] = carry_ref[...] + jnp.sum(oh, axis=0, keepdims=True)
    info = jnp.where(lane == 0, i1, jnp.where(lane == 1, i2, jnp.where(lane == 2, w1, jnp.where(
        lane == 3, w2, jnp.where(lane == 4, rank1, jnp.where(lane == 5, rank2, 0.0))))))
    info_ref[...] = info
    cnt_ref[...] = jnp.broadcast_to(carry_ref[...], cnt_ref.shape)


def _mla_out_kernel(h_ref, a_ref, wo_ref, gffn_ref, wrh_ref, wrl_ref, br_ref, tri_ref,
                    hout_ref, hn_ref, info_ref, cnt_ref, carry_ref):
    @pl.when(pl.program_id(0) == 0)
    def _():
        carry_ref[...] = jnp.zeros_like(carry_ref)

    hnew = h_ref[...] + jnp.dot(a_ref[...], wo_ref[...], preferred_element_type=F32)
    hout_ref[...] = hnew
    _route(hnew, gffn_ref, wrh_ref, wrl_ref, br_ref, tri_ref, hn_ref, info_ref, cnt_ref, carry_ref)


def _gla_out_kernel(h_ref, a_ref, r_ref, gout_ref, wo_ref, gffn_ref, wrh_ref, wrl_ref, br_ref, tri_ref,
                    hout_ref, hn_ref, info_ref, cnt_ref, carry_ref):
    @pl.when(pl.program_id(0) == 0)
    def _():
        carry_ref[...] = jnp.zeros_like(carry_ref)

    o = a_ref[...].astype(F32)
    r = r_ref[...].astype(F32)
    gout = gout_ref[...]
    parts = []
    for hh in range(GLA_HEADS):
        oh = o[:, GLA_DV * hh:GLA_DV * (hh + 1)]
        parts.append(oh * lax.rsqrt(jnp.mean(oh * oh, axis=-1, keepdims=True) + EPS) * gout)
    a = (jnp.concatenate(parts, axis=1) * (r * jax.nn.sigmoid(r))).astype(BF16)
    hnew = h_ref[...] + jnp.dot(a, wo_ref[...], preferred_element_type=F32)
    hout_ref[...] = hnew
    _route(hnew, gffn_ref, wrh_ref, wrl_ref, br_ref, tri_ref, hn_ref, info_ref, cnt_ref, carry_ref)


def _mixer_out(h, a, wo, gffn, wrh, wrl, br, tri, gla_extra=None):
    T = h.shape[0]
    tt = TOK_TILE
    tile = lambda w: pl.BlockSpec((tt, w), lambda i: (i, 0))
    const = lambda shape: pl.BlockSpec(shape, lambda i: (0,) * len(shape))
    if gla_extra is None:
        kern = _mla_out_kernel
        ins = [h, a]
        in_specs = [tile(D_MODEL), tile(a.shape[1])]
    else:
        r, gout = gla_extra
        kern = _gla_out_kernel
        ins = [h, a, r, gout]
        in_specs = [tile(D_MODEL), tile(a.shape[1]), tile(r.shape[1]), const(gout.shape)]
    ins += [wo, gffn, wrh, wrl, br, tri]
    in_specs += [const(wo.shape), const(gffn.shape), const(wrh.shape), const(wrl.shape), const(br.shape),
                 const(tri.shape)]
    return pl.pallas_call(
        kern,
        grid=(T // tt,),
        in_specs=in_specs,
        out_specs=[tile(D_MODEL), tile(D_MODEL), tile(LANES), pl.BlockSpec((8, LANES), lambda i: (0, 0))],
        out_shape=[
            jax.ShapeDtypeStruct((T, D_MODEL), F32),
            jax.ShapeDtypeStruct((T, D_MODEL), F32),
            jax.ShapeDtypeStruct((T, LANES), F32),
            jax.ShapeDtypeStruct((8, LANES), F32),
        ],
        scratch_shapes=[pltpu.VMEM((1, LANES), F32)],
        compiler_params=_cparams(("arbitrary",)),
        name="mixer_out_router",
    )(*ins)


def _dispatch_kernel(dest_ref, hn_ref, xs_in_ref, xs_ref, sem):
    del xs_in_ref
    tt = hn_ref.shape[0]

    def row_copy(r, d):
        return pltpu.make_async_copy(hn_ref.at[pl.ds(r, 1)], xs_ref.at[pl.ds(d, 1)], sem)

    def issue(r, c):
        row_copy(r, dest_ref[0, 0, 2 * r]).start()
        row_copy(r, dest_ref[0, 0, 2 * r + 1]).start()
        return c

    lax.fori_loop(0, tt, issue, 0, unroll=8)

    def drain(r, c):
        row_copy(0, 0).wait()
        row_copy(0, 0).wait()
        return c

    lax.fori_loop(0, tt, drain, 0, unroll=8)


def _dispatch(dest3, hn, xs0):
    T = hn.shape[0]
    tt = TOK_TILE
    return pl.pallas_call(
        _dispatch_kernel,
        grid=(T // tt,),
        in_specs=[
            pl.BlockSpec((1, 1, 2 * tt), lambda i: (i, 0, 0), memory_space=pltpu.SMEM),
            pl.BlockSpec((tt, D_MODEL), lambda i: (i, 0)),
            pl.BlockSpec(memory_space=pl.ANY),
        ],
        out_specs=pl.BlockSpec(memory_space=pl.ANY),
        out_shape=jax.ShapeDtypeStruct(xs0.shape, xs0.dtype),
        scratch_shapes=[pltpu.SemaphoreType.DMA],
        input_output_aliases={2: 0},
        compiler_params=_cparams(("arbitrary",)),
        name="moe_dispatch",
    )(dest3, hn, xs0)


def _expert_kernel(te_ref, nv_ref, xs_ref, wgu_ref, wd_ref, y_ref):
    del te_ref
    valid = pl.program_id(0) < nv_ref[0]

    @pl.when(valid)
    def _():
        x = xs_ref[...].astype(BF16)
        gu = jnp.dot(x, wgu_ref[0], preferred_element_type=F32)
        g = gu[:, :D_EXPERT]
        u = gu[:, D_EXPERT:]
        hid = (g * jax.nn.sigmoid(g) * u).astype(BF16)
        y_ref[...] = jnp.dot(hid, wd_ref[0], preferred_element_type=F32)

    @pl.when(jnp.logical_not(valid))
    def _():
        y_ref[...] = jnp.zeros_like(y_ref)


def _experts(tile_expert, nvalid, xs, wgu, wd):
    P = xs.shape[0]
    tm = EXP_TILE
    row = lambda i, te, nv: (jnp.minimum(i, nv[0] - 1), 0)
    return pl.pallas_call(
        _expert_kernel,
        grid_spec=pltpu.PrefetchScalarGridSpec(
            num_scalar_prefetch=2,
            grid=(P // tm,),
            in_specs=[
                pl.BlockSpec((tm, D_MODEL), row),
                pl.BlockSpec((1, D_MODEL, 2 * D_EXPERT), lambda i, te, nv: (te[i], 0, 0)),
                pl.BlockSpec((1, D_EXPERT, D_MODEL), lambda i, te, nv: (te[i], 0, 0)),
            ],
            out_specs=pl.BlockSpec((tm, D_MODEL), lambda i, te, nv: (i, 0)),
        ),
        out_shape=jax.ShapeDtypeStruct((P, D_MODEL), F32),
        compiler_params=_cparams(("arbitrary",)),
        name="moe_experts",
    )(tile_expert, nvalid, xs, wgu, wd)


def _combine_kernel(dest_ref, info_ref, h_ref, y_ref, out_ref, ybuf, sem):
    tt = h_ref.shape[0]

    def row_copy(s, r, d):
        return pltpu.make_async_copy(y_ref.at[pl.ds(d, 1)], ybuf.at[s, pl.ds(r, 1)], sem)

    def issue(r, c):
        row_copy(0, r, dest_ref[0, 0, 2 * r]).start()
        row_copy(1, r, dest_ref[0, 0, 2 * r + 1]).start()
        return c

    lax.fori_loop(0, tt, issue, 0, unroll=8)

    def drain(r, c):
        row_copy(0, 0, 0).wait()
        row_copy(1, 0, 0).wait()
        return c

    lax.fori_loop(0, tt, drain, 0, unroll=8)
    info = info_ref[...]
    out_ref[...] = h_ref[...] + info[:, 2:3] * ybuf[0] + info[:, 3:4] * ybuf[1]


def _combine(dest3, info, h, y):
    T = h.shape[0]
    tt = TOK_TILE
    return pl.pallas_call(
        _combine_kernel,
        grid=(T // tt,),
        in_specs=[
            pl.BlockSpec((1, 1, 2 * tt), lambda i: (i, 0, 0), memory_space=pltpu.SMEM),
            pl.BlockSpec((tt, LANES), lambda i: (i, 0)),
            pl.BlockSpec((tt, D_MODEL), lambda i: (i, 0)),
            pl.BlockSpec(memory_space=pl.ANY),
        ],
        out_specs=pl.BlockSpec((tt, D_MODEL), lambda i: (i, 0)),
        out_shape=jax.ShapeDtypeStruct((T, D_MODEL), F32),
        scratch_shapes=[pltpu.VMEM((2, tt, D_MODEL), F32), pltpu.SemaphoreType.DMA],
        compiler_params=_cparams(("arbitrary",)),
        name="moe_combine",
    )(dest3, info, h, y)


def _moe(h, hn, info, cnt, wgu, wd):
    T = h.shape[0]
    tm = EXP_TILE
    ntiles = (2 * T) // tm + N_EXPERTS
    eid = info[:, 0:2].astype(jnp.int32)
    rank = info[:, 4:6].astype(jnp.int32)
    counts = cnt[0, :N_EXPERTS].astype(jnp.int32)
    tiles_e = (counts + tm - 1) // tm
    cum = jnp.cumsum(tiles_e)
    start_row = (cum - tiles_e) * tm
    dest = jnp.take(start_row, eid) + rank
    dest3 = dest.reshape(T // TOK_TILE, 1, 2 * TOK_TILE)
    tile_expert = jnp.minimum(
        jnp.searchsorted(cum, jnp.arange(ntiles, dtype=jnp.int32), side="right"), N_EXPERTS - 1
    ).astype(jnp.int32)
    nvalid = cum[-1:].astype(jnp.int32)
    xs = _dispatch(dest3, hn, jnp.zeros((ntiles * tm, D_MODEL), F32))
    y = _experts(tile_expert, nvalid, xs, wgu, wd)
    return _combine(dest3, info, h, y)


def _gla_in_kernel(h_ref, w_ref, wgu_ref, bg_ref, q_ref, k_ref, g_ref, v_ref, r_ref):
    h = h_ref[...]
    hn = (h * lax.rsqrt(jnp.mean(h * h, axis=-1, keepdims=True) + EPS)).astype(BF16)
    z = jnp.dot(hn, w_ref[...], preferred_element_type=F32)
    nk = GLA_HEADS * GLA_DK
    nv = GLA_HEADS * GLA_DV
    o_v = 2 * nk
    o_g = o_v + nv
    o_r = o_g + LANES
    xg = jnp.dot(z[:, o_g:o_r].astype(BF16), wgu_ref[...], preferred_element_type=F32) + bg_ref[...]
    log_a = (jnp.minimum(xg, 0.0) - jnp.log(1.0 + jnp.exp(-jnp.abs(xg)))) * (1.0 / GLA_TAU)
    for hh in range(GLA_HEADS):
        q_ref[hh] = z[:, GLA_DK * hh:GLA_DK * (hh + 1)].astype(BF16)
        k_ref[hh] = z[:, nk + GLA_DK * hh:nk + GLA_DK * (hh + 1)].astype(BF16)
        g_ref[hh] = log_a[:, GLA_DK * hh:GLA_DK * (hh + 1)]
        v_ref[hh] = z[:, o_v + GLA_DV * hh:o_v + GLA_DV * (hh + 1)].astype(BF16)
    r_ref[...] = z[:, o_r:].astype(BF16)


def _gla_in(h, w, wgu, bg):
    T = h.shape[0]
    tt = TOK_TILE
    GH = GLA_HEADS
    const = lambda shape: pl.BlockSpec(shape, lambda i: (0,) * len(shape))
    hspec = lambda w_: pl.BlockSpec((GH, tt, w_), lambda i: (0, i, 0))
    return pl.pallas_call(
        _gla_in_kernel,
        grid=(T // tt,),
        in_specs=[pl.BlockSpec((tt, D_MODEL), lambda i: (i, 0)), const(w.shape), const(wgu.shape), const(bg.shape)],
        out_specs=[hspec(GLA_DK), hspec(GLA_DK), hspec(GLA_DK), hspec(GLA_DV),
                   pl.BlockSpec((tt, GH * GLA_DV), lambda i: (i, 0))],
        out_shape=[
            jax.ShapeDtypeStruct((GH, T, GLA_DK), BF16),
            jax.ShapeDtypeStruct((GH, T, GLA_DK), BF16),
            jax.ShapeDtypeStruct((GH, T, GLA_DK), F32),
            jax.ShapeDtypeStruct((GH, T, GLA_DV), BF16),
            jax.ShapeDtypeStruct((T, GH * GLA_DV), BF16),
        ],
        compiler_params=_cparams(("arbitrary",)),
        name="gla_in",
    )(h, w, wgu, bg)


def _gla_scan_kernel(q_ref, k_ref, g_ref, v_ref, stack_ref, o_ref, st_ref, *, tile):
    C = SCAN_CHUNK
    t = pl.program_id(2)

    @pl.when(t == 0)
    def _():
        st_ref[...] = jnp.zeros_like(st_ref)

    row = lax.broadcasted_iota(jnp.int32, (C, C), 0)
    col = lax.broadcasted_iota(jnp.int32, (C, C), 1)
    diag = row == col
    masks = [((row // (2 * c)) == (col // (2 * c))) & ((row % (2 * c)) >= c) & ((col % (2 * c)) < c)
             for c in GLA_LEVELS]
    rid = lax.broadcasted_iota(jnp.int32, (C, 1), 0)

    def chunk(ci, carry):
        r0 = pl.multiple_of(ci * C, C)
        q = q_ref[0, pl.ds(r0, C), :].astype(F32)
        k = k_ref[0, pl.ds(r0, C), :].astype(F32)
        g = g_ref[0, pl.ds(r0, C), :]
        v = v_ref[0, pl.ds(r0, C), :]
        k = jnp.where(t * tile + r0 + rid >= PAD_ROWS, k, 0.0)
        g_hi = g.astype(BF16)
        g_lo = (g - g_hi.astype(F32)).astype(BF16)
        sums2 = jnp.dot(stack_ref[...], jnp.concatenate([g_hi, g_lo], axis=1), preferred_element_type=F32)
        sums = sums2[:, :GLA_DK] + sums2[:, GLA_DK:]
        b = sums[0:C]
        b_last = b[C - 1:C]
        st = st_ref[...]
        o = _nt_dot((q * jnp.exp(b)).astype(BF16), st.astype(BF16))
        att = jnp.where(diag, _nt_dot(q.astype(BF16), k.astype(BF16)), 0.0)
        for li in range(len(GLA_LEVELS)):
            ref = sums[C * (li + 1):C * (li + 2)]
            qf = (q * jnp.exp(jnp.minimum(b - ref, 0.0))).astype(BF16)
            kb = (k * jnp.exp(jnp.minimum(ref - b, 0.0))).astype(BF16)
            att = att + jnp.where(masks[li], _nt_dot(qf, kb), 0.0)
        o = o + jnp.dot(att.astype(BF16), v, preferred_element_type=F32)
        o_ref[pl.ds(r0, C), :] = o.astype(BF16)
        kd = (k * jnp.exp(b_last - b)).astype(BF16)
        st_ref[...] = st * jnp.exp(b_last) + _tn_dot(v, kd)
        return carry

    lax.fori_loop(0, tile // C, chunk, 0)


def _gla_scan(q, k, g, v, stack, B, LK):
    GH, T, _ = q.shape
    tile = 640 if LK % 640 == 0 else LANES
    nt = LK // tile
    hspec = lambda w_: pl.BlockSpec((1, tile, w_), lambda b, h, t: (h, b * nt + t, 0))
    return pl.pallas_call(
        functools.partial(_gla_scan_kernel, tile=tile),
        grid=(B, GH, nt),
        in_specs=[hspec(GLA_DK), hspec(GLA_DK), hspec(GLA_DK), hspec(GLA_DV),
                  pl.BlockSpec(stack.shape, lambda b, h, t: (0, 0))],
        out_specs=pl.BlockSpec((tile, GLA_DV), lambda b, h, t: (b * nt + t, h)),
        out_shape=jax.ShapeDtypeStruct((T, GH * GLA_DV), BF16),
        scratch_shapes=[pltpu.VMEM((GLA_DV, GLA_DK), F32)],
        compiler_params=_cparams(("arbitrary", "arbitrary", "arbitrary")),
        name="gla_scan",
    )(q, k, g, v, stack)


def _gla_stack():
    C = SCAN_CHUNK
    t = jnp.arange(C)[:, None]
    u = jnp.arange(C)[None, :]
    mats = [u <= t]
    for c in GLA_LEVELS:
        boundary = (t // (2 * c)) * (2 * c) + c - 1
        mats.append(u <= boundary)
    return jnp.concatenate(mats, axis=0).astype(BF16)


def _router_weights(w_rg, b_rg, w_re, b_re):
    w = jnp.zeros((D_MODEL, LANES), F32).at[:, :N_EXPERTS].set(w_re).at[:, N_EXPERTS:N_EXPERTS + N_GROUPS].set(w_rg)
    b = jnp.zeros((1, LANES), F32).at[0, :N_EXPERTS].set(b_re).at[0, N_EXPERTS:N_EXPERTS + N_GROUPS].set(b_rg)
    w_hi = w.astype(BF16)
    w_lo = (w - w_hi.astype(F32)).astype(BF16)
    return w_hi, w_lo, b


def kernel(x, meta_tokens, norm_mix, norm_ffn, mla_w_in, mla_g_q, mla_w_uq, mla_g_kv, mla_w_ukv, mla_g_qn, mla_g_kn, mla_w_o, gla_w_in, gla_w_gate_up, gla_b_gate, gla_g_out, gla_w_o, moe_w_rg, moe_b_rg, moe_w_re, moe_b_re, moe_w_gate, moe_w_up, moe_w_down):
    B, S, D = x.shape
    assert D == D_MODEL and S % ATT_TILE == 0
    LK = META_BLOCK + S
    T = B * LK
    assert T % TOK_TILE == 0 and TOK_TILE % LANES == 0
    tt = TOK_TILE
    H = MLA_HEADS

    meta = jnp.broadcast_to(meta_tokens.astype(F32)[None], (B, N_META, D))
    h = jnp.concatenate([jnp.zeros((B, PAD_ROWS, D), F32), meta, x.astype(F32)], axis=1).reshape(T, D)

    rows = jnp.arange(LK)
    pos = jnp.where(rows < META_BLOCK, jnp.maximum(rows - PAD_ROWS, 0), rows - META_BLOCK + N_META)
    half = MLA_ROPE // 2
    inv = 1.0 / (ROPE_THETA ** (jnp.arange(half, dtype=F32) / half))
    ang = inv[:, None] * pos.astype(F32)[None, :]
    cos = jnp.tile(jnp.cos(ang), (1, B))
    sin = jnp.tile(jnp.sin(ang), (1, B))

    tri = (jnp.arange(tt)[None, :] < jnp.arange(tt)[:, None]).astype(BF16)

    winT = (mla_w_in[0] * norm_mix[0][:, None]).T.astype(BF16)
    wuqT = (mla_w_uq[0] * mla_g_q[0][:, None]).T.astype(BF16)
    wukv = (mla_w_ukv[0] * mla_g_kv[0][:, None]).reshape(MLA_KV_RANK, H, MLA_NOPE + MLA_V)
    wkT = wukv[:, :, :MLA_NOPE].reshape(MLA_KV_RANK, H * MLA_NOPE).T.astype(BF16)
    wv = jnp.zeros((MLA_KV_RANK, H, LANES), F32).at[:, :, :MLA_V].set(wukv[:, :, MLA_NOPE:])
    wv = wv.reshape(MLA_KV_RANK, H * LANES).astype(BF16)
    vone = jnp.zeros((1, H, LANES), F32).at[:, :, MLA_V].set(1.0).reshape(1, H * LANES)
    gq = jnp.broadcast_to((mla_g_qn[0] * (MLA_QK ** -0.5))[:, None], (MLA_QK, tt)).astype(F32)
    gk = jnp.broadcast_to(mla_g_kn[0][:, None], (MLA_QK, tt)).astype(F32)
    q, kT, v = _mla_proj(h, winT, wuqT, wkT, wv, vone, gq, gk, cos, sin)
    att = _attention(q, kT, v, B, LK)

    wrh, wrl, br = _router_weights(moe_w_rg[0], moe_b_rg[0], moe_w_re[0], moe_b_re[0])
    h, hn, info, cnt = _mixer_out(h, att, mla_w_o[0].astype(BF16), norm_ffn[0][None, :], wrh, wrl, br, tri)
    wgu = jnp.concatenate([moe_w_gate[0], moe_w_up[0]], axis=-1).astype(BF16)
    h = _moe(h, hn, info, cnt, wgu, moe_w_down[0].astype(BF16))

    nk = GLA_HEADS * GLA_DK
    nv = GLA_HEADS * GLA_DV
    w1 = gla_w_in[0] * norm_mix[1][:, None]
    wg_pad = jnp.zeros((D, LANES), F32).at[:, :GLA_GATE_RANK].set(w1[:, 2 * nk + nv:2 * nk + nv + GLA_GATE_RANK])
    w_all = jnp.concatenate([w1[:, :nk] * (GLA_DK ** -0.5), w1[:, nk:2 * nk + nv], wg_pad,
                             w1[:, 2 * nk + nv + GLA_GATE_RANK:]], axis=1).astype(BF16)
    wgate = jnp.zeros((LANES, nk), F32).at[:GLA_GATE_RANK].set(gla_w_gate_up[0]).astype(BF16)
    gq_, gk_, gg_, gv_, gr_ = _gla_in(h, w_all, wgate, gla_b_gate[0][None, :])
    go = _gla_scan(gq_, gk_, gg_, gv_, _gla_stack(), B, LK)

    wrh, wrl, br = _router_weights(moe_w_rg[1], moe_b_rg[1], moe_w_re[1], moe_b_re[1])
    h, hn, info, cnt = _mixer_out(h, go, gla_w_o[0].astype(BF16), norm_ffn[1][None, :], wrh, wrl, br, tri,
                                  gla_extra=(gr_, gla_g_out[0][None, :]))
    wgu = jnp.concatenate([moe_w_gate[1], moe_w_up[1]], axis=-1).astype(BF16)
    h = _moe(h, hn, info, cnt, wgu, moe_w_down[1].astype(BF16))

    return h.reshape(B, LK, D)[:, META_BLOCK:].astype(x.dtype)
```

```python
import functools

import jax
import jax.numpy as jnp
from jax import lax
from jax.experimental import pallas as pl
from jax.experimental.pallas import tpu as pltpu

F32 = jnp.float32
BF16 = jnp.bfloat16

D_MODEL = 1024
CHUNK = 64
N_META = 16
MLA_HEADS = 16
MLA_Q_RANK = 384
MLA_KV_RANK = 256
MLA_NOPE = 64
MLA_ROPE = 32
MLA_V = 64
MLA_QK = MLA_NOPE + MLA_ROPE
ROPE_THETA = 10000.0
GLA_HEADS = 4
GLA_DK = 128
GLA_DV = 256
GLA_GATE_RANK = 16
GLA_TAU = 16.0
N_GROUPS = 4
EXPERTS_PER_GROUP = 8
N_EXPERTS = N_GROUPS * EXPERTS_PER_GROUP
D_EXPERT = 256
EPS = 1e-6

LANES = 128
META_BLOCK = 128
PAD_ROWS = META_BLOCK - N_META
NEG = -1e30
LOG2E = 1.4426950408889634

TOK_TILE = 512
ATT_TILE = 512
ATT_SAFE_BOUND = 50.0
SCAN_CHUNK = 64
EXP_TILE = 256
VMEM_LIMIT = 56 * 1024 * 1024

GLA_LEVELS = (32, 16, 8, 4, 2, 1)


def _cparams(sem):
    return pltpu.CompilerParams(dimension_semantics=sem, vmem_limit_bytes=VMEM_LIMIT)


def _nt_dot(a, b):
    return lax.dot_general(a, b, (((1,), (1,)), ((), ())), preferred_element_type=F32)


def _tn_dot(a, b):
    return lax.dot_general(a, b, (((0,), (0,)), ((), ())), preferred_element_type=F32)


def _mla_proj_kernel(h_ref, winT_ref, wuqT_ref, wkT_ref, wv_ref, vone_ref, gq_ref, gk_ref,
                     cos_ref, sin_ref, q_ref, kT_ref, v_ref):
    tt = h_ref.shape[0]
    h = h_ref[...]
    hn = (h * lax.rsqrt(jnp.mean(h * h, axis=-1, keepdims=True) + EPS)).astype(BF16)
    zT = _nt_dot(winT_ref[...], hn)
    cq = zT[0:MLA_Q_RANK]
    ckv = zT[MLA_Q_RANK:MLA_Q_RANK + MLA_KV_RANK]
    kr = zT[MLA_Q_RANK + MLA_KV_RANK:]
    cqn = (cq * lax.rsqrt(jnp.mean(cq * cq, axis=0, keepdims=True) + EPS)).astype(BF16)
    ckvn = ckv * lax.rsqrt(jnp.mean(ckv * ckv, axis=0, keepdims=True) + EPS)
    qT = jnp.dot(wuqT_ref[...], cqn, preferred_element_type=F32)
    kT = jnp.dot(wkT_ref[...], ckvn.astype(BF16), preferred_element_type=F32)
    v = jnp.dot(ckvn.T.astype(BF16), wv_ref[...], preferred_element_type=F32) + vone_ref[...]
    cos = cos_ref[...]
    sin = sin_ref[...]
    gq = gq_ref[...]
    gk = gk_ref[...]
    half = MLA_ROPE // 2
    zpad = jnp.zeros((LANES - MLA_QK, tt), F32)
    kpad = jnp.concatenate([jnp.ones((1, tt), F32), jnp.zeros((LANES - MLA_QK - 1, tt), F32)], axis=0)
    kr_ss = jnp.sum(kr * kr, axis=0, keepdims=True)
    for hh in range(MLA_HEADS):
        qh = qT[MLA_QK * hh:MLA_QK * (hh + 1)]
        qn = qh * lax.rsqrt(jnp.sum(qh * qh, axis=0, keepdims=True) * (1.0 / MLA_QK) + EPS) * gq
        x1 = qn[MLA_NOPE:MLA_NOPE + half]
        x2 = qn[MLA_NOPE + half:]
        qrot = jnp.concatenate([qn[:MLA_NOPE], x1 * cos - x2 * sin, x2 * cos + x1 * sin, zpad], axis=0)
        q_ref[hh] = qrot.T.astype(BF16)
        kn = kT[MLA_NOPE * hh:MLA_NOPE * (hh + 1)]
        rk = lax.rsqrt((jnp.sum(kn * kn, axis=0, keepdims=True) + kr_ss) * (1.0 / MLA_QK) + EPS)
        knn = kn * rk * gk[:MLA_NOPE]
        krn = kr * rk * gk[MLA_NOPE:]
        y1 = krn[:half]
        y2 = krn[half:]
        kfull = jnp.concatenate([knn, y1 * cos - y2 * sin, y2 * cos + y1 * sin, kpad], axis=0).astype(BF16)
        for u in range(tt // LANES):
            kT_ref[hh, u] = kfull[:, LANES * u:LANES * (u + 1)]
        v_ref[hh] = v[:, LANES * hh:LANES * (hh + 1)].astype(BF16)


def _mla_proj(h, winT, wuqT, wkT, wv, vone, gq, gk, cos, sin):
    T = h.shape[0]
    tt = TOK_TILE
    H = MLA_HEADS
    const = lambda shape: pl.BlockSpec(shape, lambda i: (0,) * len(shape))
    return pl.pallas_call(
        _mla_proj_kernel,
        grid=(T // tt,),
        in_specs=[
            pl.BlockSpec((tt, D_MODEL), lambda i: (i, 0)),
            const(winT.shape), const(wuqT.shape), const(wkT.shape), const(wv.shape), const(vone.shape),
            const(gq.shape), const(gk.shape),
            pl.BlockSpec((MLA_ROPE // 2, tt), lambda i: (0, i)),
            pl.BlockSpec((MLA_ROPE // 2, tt), lambda i: (0, i)),
        ],
        out_specs=[
            pl.BlockSpec((H, tt, LANES), lambda i: (0, i, 0)),
            pl.BlockSpec((H, tt // LANES, LANES, LANES), lambda i: (0, i, 0, 0)),
            pl.BlockSpec((H, tt, LANES), lambda i: (0, i, 0)),
        ],
        out_shape=[
            jax.ShapeDtypeStruct((H, T, LANES), BF16),
            jax.ShapeDtypeStruct((H, T // LANES, LANES, LANES), BF16),
            jax.ShapeDtypeStruct((H, T, LANES), BF16),
        ],
        compiler_params=_cparams(("arbitrary",)),
        name="mla_proj",
    )(h, winT, wuqT, wkT, wv, vone, gq, gk, cos, sin)


def _attn_kernel(q_ref, kT_ref, v_ref, o_ref, *, nf, nblk):
    tq = ATT_TILE
    nsub = tq // LANES
    heads = range(2)
    row = lax.broadcasted_iota(jnp.int32, (tq, tq), 0)
    col = lax.broadcasted_iota(jnp.int32, (tq, tq), 1)
    diag_bias = jnp.where((col // CHUNK) <= (row // CHUNK), 0.0, NEG).astype(F32)
    col0 = lax.broadcasted_iota(jnp.int32, (1, LANES), 1)
    bias0 = jnp.where(col0 >= PAD_ROWS, 0.0, NEG).astype(F32)

    def ktile(hh, blk):
        return jnp.concatenate([kT_ref[hh, blk + u] for u in range(nsub)], axis=1)

    def vtile(hh, t):
        return v_ref[hh, pl.ds(pl.multiple_of(META_BLOCK + t * tq, LANES), tq), :]

    def finish(acc):
        return (acc[:, :MLA_V] / acc[:, MLA_V:MLA_V + 1]).astype(BF16)

    def put(r0, n, hh, acc):
        o_ref[pl.ds(r0, n), MLA_V * hh:MLA_V * (hh + 1)] = finish(acc)

    def norms(hh):
        def body(blk, c):
            kk = kT_ref[hh, blk][:MLA_QK].astype(F32)
            qq = q_ref[hh, pl.ds(pl.multiple_of(blk * LANES, LANES), LANES), :].astype(F32)
            return (jnp.maximum(c[0], jnp.sum(kk * kk, axis=0, keepdims=True)),
                    jnp.maximum(c[1], jnp.sum(qq * qq, axis=1, keepdims=True)))
        k2, q2 = lax.fori_loop(0, nblk, body, (jnp.zeros((1, LANES), F32), jnp.zeros((LANES, 1), F32)))
        return jnp.max(k2, axis=1, keepdims=True), jnp.max(q2, axis=0, keepdims=True)

    kq = [norms(hh) for hh in heads]
    worst = jnp.sqrt(jnp.maximum(kq[0][0] * kq[0][1], kq[1][0] * kq[1][1]))

    def shifted():
        kmax = [jnp.sqrt(kq[hh][0]) for hh in heads]
        lane = lax.broadcasted_iota(jnp.int32, (1, LANES), 1)

        def with_shift(q, hh):
            qf = q.astype(F32)
            bound = jnp.sqrt(jnp.sum(qf * qf, axis=1, keepdims=True)) * kmax[hh]
            return jnp.where(lane == MLA_QK, -bound, qf).astype(BF16)

        def part(q, kt, vv, bias):
            s = jnp.dot(q, kt, preferred_element_type=F32)
            if bias is not None:
                s = s + bias
            return jnp.dot(jnp.exp2(s).astype(BF16), vv, preferred_element_type=F32)

        for hh in heads:
            q0 = with_shift(q_ref[hh, 0:META_BLOCK, :], hh)
            put(0, META_BLOCK, hh, part(q0, kT_ref[hh, 0], v_ref[hh, 0:META_BLOCK, :], bias0))

        def qtile(i, carry):
            r0 = pl.multiple_of(META_BLOCK + i * tq, LANES)
            qs = [with_shift(q_ref[hh, pl.ds(r0, tq), :], hh) for hh in heads]
            accs = [part(qs[hh], kT_ref[hh, 0], v_ref[hh, 0:META_BLOCK, :], bias0)
                    + part(qs[hh], ktile(hh, 1 + i * nsub), vtile(hh, i), diag_bias) for hh in heads]

            def kv(j, acc):
                return tuple(acc[hh] + part(qs[hh], ktile(hh, 1 + j * nsub), vtile(hh, j), None) for hh in heads)

            accs = lax.fori_loop(0, i, kv, tuple(accs))
            for hh in heads:
                put(r0, tq, hh, accs[hh])
            return carry

        lax.fori_loop(0, nf, qtile, 0)

    def online():
        def step(q, kt, vv, m, acc, bias):
            s = jnp.dot(q, kt, preferred_element_type=F32)
            if bias is not None:
                s = s + bias
            m_new = jnp.maximum(m, jnp.max(s, axis=-1, keepdims=True))
            p = jnp.exp2(s - m_new)
            acc = jnp.exp2(m - m_new) * acc + jnp.dot(p.astype(BF16), vv, preferred_element_type=F32)
            return m_new, acc

        def first(q, hh, n):
            return step(q, kT_ref[hh, 0], v_ref[hh, 0:META_BLOCK, :], jnp.full((n, 1), NEG, F32),
                        jnp.zeros((n, LANES), F32), bias0)

        for hh in heads:
            put(0, META_BLOCK, hh, first(q_ref[hh, 0:META_BLOCK, :], hh, META_BLOCK)[1])

        def qtile(i, carry):
            r0 = pl.multiple_of(META_BLOCK + i * tq, LANES)
            qs = [q_ref[hh, pl.ds(r0, tq), :] for hh in heads]
            state = []
            for hh in heads:
                state += list(first(qs[hh], hh, tq))

            def kv(j, c):
                out = []
                for hh in heads:
                    out += list(step(qs[hh], ktile(hh, 1 + j * nsub), vtile(hh, j), c[2 * hh], c[2 * hh + 1], None))
                return tuple(out)

            state = lax.fori_loop(0, i, kv, tuple(state))
            for hh in heads:
                m, acc = step(qs[hh], ktile(hh, 1 + i * nsub), vtile(hh, i), state[2 * hh], state[2 * hh + 1],
                              diag_bias)
                put(r0, tq, hh, acc)
            return carry

        lax.fori_loop(0, nf, qtile, 0)

    lax.cond(worst[0, 0] <= ATT_SAFE_BOUND, shifted, online)


def _attention(q, kT, v, B, LK):
    H, T, _ = q.shape
    nf = (LK - META_BLOCK) // ATT_TILE
    nblk = LK // LANES
    return pl.pallas_call(
        functools.partial(_attn_kernel, nf=nf, nblk=nblk),
        grid=(B, H // 2),
        in_specs=[
            pl.BlockSpec((2, LK, LANES), lambda b, hp: (hp, b, 0)),
            pl.BlockSpec((2, nblk, LANES, LANES), lambda b, hp: (hp, b, 0, 0)),
            pl.BlockSpec((2, LK, LANES), lambda b, hp: (hp, b, 0)),
        ],
        out_specs=pl.BlockSpec((LK, 2 * MLA_V), lambda b, hp: (b, hp)),
        out_shape=jax.ShapeDtypeStruct((T, H * MLA_V), BF16),
        compiler_params=_cparams(("arbitrary", "arbitrary")),
        name="mla_attention",
    )(q, kT, v)


def _route(hnew, gffn_ref, wrh_ref, wrl_ref, br_ref, tri_ref, hn_ref, info_ref, cnt_ref, carry_ref):
    tt = hnew.shape[0]
    hn = hnew * lax.rsqrt(jnp.mean(hnew * hnew, axis=-1, keepdims=True) + EPS) * gffn_ref[...]
    hn_ref[...] = hn
    hn_hi = hn.astype(BF16)
    hn_lo = (hn - hn_hi.astype(F32)).astype(BF16)
    wrh = wrh_ref[...]
    logits = (jnp.dot(hn_hi, wrh, preferred_element_type=F32)
              + jnp.dot(hn_lo, wrh, preferred_element_type=F32)
              + jnp.dot(hn_hi, wrl_ref[...], preferred_element_type=F32)) + br_ref[...]

    lane = lax.broadcasted_iota(jnp.int32, (tt, LANES), 1)
    lane_f = lane.astype(F32)
    big = float(LANES)
    gmask = (lane >= N_EXPERTS) & (lane < N_EXPERTS + N_GROUPS)
    gl = jnp.where(gmask, logits, NEG)
    gmax = jnp.max(gl, axis=-1, keepdims=True)
    gsel = jnp.min(jnp.where(gl == gmax, lane_f, big), axis=-1, keepdims=True) - float(N_EXPERTS)
    pg = 1.0 / jnp.sum(jnp.exp(gl - gmax), axis=-1, keepdims=True)
    egrp = (lane // EXPERTS_PER_GROUP).astype(F32)
    emask = (lane < N_EXPERTS) & (egrp == gsel)
    el = jnp.where(emask, logits, NEG)
    m1 = jnp.max(el, axis=-1, keepdims=True)
    i1 = jnp.min(jnp.where(el == m1, lane_f, big), axis=-1, keepdims=True)
    el2 = jnp.where(lane_f == i1, NEG, el)
    m2 = jnp.max(el2, axis=-1, keepdims=True)
    i2 = jnp.min(jnp.where(el2 == m2, lane_f, big), axis=-1, keepdims=True)
    t21 = jnp.exp(m2 - m1)
    w1 = pg / (1.0 + t21)
    w2 = w1 * t21

    sel1 = lane_f == i1
    sel2 = lane_f == i2
    oh = jnp.where(sel1 | sel2, 1.0, 0.0)
    before = jnp.dot(tri_ref[...], oh.astype(BF16), preferred_element_type=F32) + carry_ref[...]
    rank1 = jnp.sum(jnp.where(sel1, before, 0.0), axis=-1, keepdims=True)
    rank2 = jnp.sum(jnp.where(sel2, before, 0.0), axis=-1, keepdims=True)
    carry_ref[...] = carry_ref[...] + jnp.sum(oh, axis=0, keepdims=True)
    info = jnp.where(lane == 0, i1, jnp.where(lane == 1, i2, jnp.where(lane == 2, w1, jnp.where(
        lane == 3, w2, jnp.where(lane == 4, rank1, jnp.where(lane == 5, rank2, 0.0))))))
    info_ref[...] = info
    cnt_ref[...] = jnp.broadcast_to(carry_ref[...], cnt_ref.shape)


def _mla_out_kernel(h_ref, a_ref, wo_ref, gffn_ref, wrh_ref, wrl_ref, br_ref, tri_ref,
                    hout_ref, hn_ref, info_ref, cnt_ref, carry_ref):
    @pl.when(pl.program_id(0) == 0)
    def _():
        carry_ref[...] = jnp.zeros_like(carry_ref)

    hnew = h_ref[...] + jnp.dot(a_ref[...], wo_ref[...], preferred_element_type=F32)
    hout_ref[...] = hnew
    _route(hnew, gffn_ref, wrh_ref, wrl_ref, br_ref, tri_ref, hn_ref, info_ref, cnt_ref, carry_ref)


def _gla_out_kernel(h_ref, a_ref, r_ref, gout_ref, wo_ref, gffn_ref, wrh_ref, wrl_ref, br_ref, tri_ref,
                    hout_ref, hn_ref, info_ref, cnt_ref, carry_ref):
    @pl.when(pl.program_id(0) == 0)
    def _():
        carry_ref[...] = jnp.zeros_like(carry_ref)

    o = a_ref[...].astype(F32)
    r = r_ref[...].astype(F32)
    gout = gout_ref[...]
    parts = []
    for hh in range(GLA_HEADS):
        oh = o[:, GLA_DV * hh:GLA_DV * (hh + 1)]
        parts.append(oh * lax.rsqrt(jnp.mean(oh * oh, axis=-1, keepdims=True) + EPS) * gout)
    a = (jnp.concatenate(parts, axis=1) * (r * jax.nn.sigmoid(r))).astype(BF16)
    hnew = h_ref[...] + jnp.dot(a, wo_ref[...], preferred_element_type=F32)
    hout_ref[...] = hnew
    _route(hnew, gffn_ref, wrh_ref, wrl_ref, br_ref, tri_ref, hn_ref, info_ref, cnt_ref, carry_ref)


def _mixer_out(h, a, wo, gffn, wrh, wrl, br, tri, gla_extra=None):
    T = h.shape[0]
    tt = TOK_TILE
    tile = lambda w: pl.BlockSpec((tt, w), lambda i: (i, 0))
    const = lambda shape: pl.BlockSpec(shape, lambda i: (0,) * len(shape))
    if gla_extra is None:
        kern = _mla_out_kernel
        ins = [h, a]
        in_specs = [tile(D_MODEL), tile(a.shape[1])]
    else:
        r, gout = gla_extra
        kern = _gla_out_kernel
        ins = [h, a, r, gout]
        in_specs = [tile(D_MODEL), tile(a.shape[1]), tile(r.shape[1]), const(gout.shape)]
    ins += [wo, gffn, wrh, wrl, br, tri]
    in_specs += [const(wo.shape), const(gffn.shape), const(wrh.shape), const(wrl.shape), const(br.shape),
                 const(tri.shape)]
    return pl.pallas_call(
        kern,
        grid=(T // tt,),
        in_specs=in_specs,
        out_specs=[tile(D_MODEL), tile(D_MODEL), tile(LANES), pl.BlockSpec((8, LANES), lambda i: (0, 0))],
        out_shape=[
            jax.ShapeDtypeStruct((T, D_MODEL), F32),
            jax.ShapeDtypeStruct((T, D_MODEL), F32),
            jax.ShapeDtypeStruct((T, LANES), F32),
            jax.ShapeDtypeStruct((8, LANES), F32),
        ],
        scratch_shapes=[pltpu.VMEM((1, LANES), F32)],
        compiler_params=_cparams(("arbitrary",)),
        name="mixer_out_router",
    )(*ins)


def _dispatch_kernel(dest_ref, hn_ref, xs_in_ref, xs_ref, sem):
    del xs_in_ref
    tt = hn_ref.shape[0]

    def row_copy(r, d):
        return pltpu.make_async_copy(hn_ref.at[pl.ds(r, 1)], xs_ref.at[pl.ds(d, 1)], sem)

    def issue(r, c):
        row_copy(r, dest_ref[0, 0, 2 * r]).start()
        row_copy(r, dest_ref[0, 0, 2 * r + 1]).start()
        return c

    lax.fori_loop(0, tt, issue, 0, unroll=8)

    def drain(r, c):
        row_copy(0, 0).wait()
        row_copy(0, 0).wait()
        return c

    lax.fori_loop(0, tt, drain, 0, unroll=8)


def _dispatch(dest3, hn, xs0):
    T = hn.shape[0]
    tt = TOK_TILE
    return pl.pallas_call(
        _dispatch_kernel,
        grid=(T // tt,),
        in_specs=[
            pl.BlockSpec((1, 1, 2 * tt), lambda i: (i, 0, 0), memory_space=pltpu.SMEM),
            pl.BlockSpec((tt, D_MODEL), lambda i: (i, 0)),
            pl.BlockSpec(memory_space=pl.ANY),
        ],
        out_specs=pl.BlockSpec(memory_space=pl.ANY),
        out_shape=jax.ShapeDtypeStruct(xs0.shape, xs0.dtype),
        scratch_shapes=[pltpu.SemaphoreType.DMA],
        input_output_aliases={2: 0},
        compiler_params=_cparams(("arbitrary",)),
        name="moe_dispatch",
    )(dest3, hn, xs0)


def _expert_kernel(te_ref, nv_ref, xs_ref, wgu_ref, wd_ref, y_ref):
    del te_ref
    valid = pl.program_id(0) < nv_ref[0]

    @pl.when(valid)
    def _():
        x = xs_ref[...].astype(BF16)
        gu = jnp.dot(x, wgu_ref[0], preferred_element_type=F32)
        g = gu[:, :D_EXPERT]
        u = gu[:, D_EXPERT:]
        hid = (g * jax.nn.sigmoid(g) * u).astype(BF16)
        y_ref[...] = jnp.dot(hid, wd_ref[0], preferred_element_type=F32)

    @pl.when(jnp.logical_not(valid))
    def _():
        y_ref[...] = jnp.zeros_like(y_ref)


def _experts(tile_expert, nvalid, xs, wgu, wd):
    P = xs.shape[0]
    tm = EXP_TILE
    row = lambda i, te, nv: (jnp.minimum(i, nv[0] - 1), 0)
    return pl.pallas_call(
        _expert_kernel,
        grid_spec=pltpu.PrefetchScalarGridSpec(
            num_scalar_prefetch=2,
            grid=(P // tm,),
            in_specs=[
                pl.BlockSpec((tm, D_MODEL), row),
                pl.BlockSpec((1, D_MODEL, 2 * D_EXPERT), lambda i, te, nv: (te[i], 0, 0)),
                pl.BlockSpec((1, D_EXPERT, D_MODEL), lambda i, te, nv: (te[i], 0, 0)),
            ],
            out_specs=pl.BlockSpec((tm, D_MODEL), lambda i, te, nv: (i, 0)),
        ),
        out_shape=jax.ShapeDtypeStruct((P, D_MODEL), F32),
        compiler_params=_cparams(("arbitrary",)),
        name="moe_experts",
    )(tile_expert, nvalid, xs, wgu, wd)


def _combine_kernel(dest_ref, info_ref, h_ref, y_ref, out_ref, ybuf, sem):
    tt = h_ref.shape[0]

    def row_copy(s, r, d):
        return pltpu.make_async_copy(y_ref.at[pl.ds(d, 1)], ybuf.at[s, pl.ds(r, 1)], sem)

    def issue(r, c):
        row_copy(0, r, dest_ref[0, 0, 2 * r]).start()
        row_copy(1, r, dest_ref[0, 0, 2 * r + 1]).start()
        return c

    lax.fori_loop(0, tt, issue, 0, unroll=8)

    def drain(r, c):
        row_copy(0, 0, 0).wait()
        row_copy(1, 0, 0).wait()
        return c

    lax.fori_loop(0, tt, drain, 0, unroll=8)
    info = info_ref[...]
    out_ref[...] = h_ref[...] + info[:, 2:3] * ybuf[0] + info[:, 3:4] * ybuf[1]


def _combine(dest3, info, h, y):
    T = h.shape[0]
    tt = TOK_TILE
    return pl.pallas_call(
        _combine_kernel,
        grid=(T // tt,),
        in_specs=[
            pl.BlockSpec((1, 1, 2 * tt), lambda i: (i, 0, 0), memory_space=pltpu.SMEM),
            pl.BlockSpec((tt, LANES), lambda i: (i, 0)),
            pl.BlockSpec((tt, D_MODEL), lambda i: (i, 0)),
            pl.BlockSpec(memory_space=pl.ANY),
        ],
        out_specs=pl.BlockSpec((tt, D_MODEL), lambda i: (i, 0)),
        out_shape=jax.ShapeDtypeStruct((T, D_MODEL), F32),
        scratch_shapes=[pltpu.VMEM((2, tt, D_MODEL), F32), pltpu.SemaphoreType.DMA],
        compiler_params=_cparams(("arbitrary",)),
        name="moe_combine",
    )(dest3, info, h, y)


def _moe(h, hn, info, cnt, wgu, wd):
    T = h.shape[0]
    tm = EXP_TILE
    ntiles = (2 * T) // tm + N_EXPERTS
    eid = info[:, 0:2].astype(jnp.int32)
    rank = info[:, 4:6].astype(jnp.int32)
    counts = cnt[0, :N_EXPERTS].astype(jnp.int32)
    tiles_e = (counts + tm - 1) // tm
    cum = jnp.cumsum(tiles_e)
    start_row = (cum - tiles_e) * tm
    dest = jnp.take(start_row, eid) + rank
    dest3 = dest.reshape(T // TOK_TILE, 1, 2 * TOK_TILE)
    tile_expert = jnp.minimum(
        jnp.searchsorted(cum, jnp.arange(ntiles, dtype=jnp.int32), side="right"), N_EXPERTS - 1
    ).astype(jnp.int32)
    nvalid = cum[-1:].astype(jnp.int32)
    xs = _dispatch(dest3, hn, jnp.zeros((ntiles * tm, D_MODEL), F32))
    y = _experts(tile_expert, nvalid, xs, wgu, wd)
    return _combine(dest3, info, h, y)


def _gla_in_kernel(h_ref, w_ref, wgu_ref, bg_ref, q_ref, k_ref, g_ref, v_ref, r_ref):
    h = h_ref[...]
    hn = (h * lax.rsqrt(jnp.mean(h * h, axis=-1, keepdims=True) + EPS)).astype(BF16)
    z = jnp.dot(hn, w_ref[...], preferred_element_type=F32)
    nk = GLA_HEADS * GLA_DK
    nv = GLA_HEADS * GLA_DV
    o_v = 2 * nk
    o_g = o_v + nv
    o_r = o_g + LANES
    xg = jnp.dot(z[:, o_g:o_r].astype(BF16), wgu_ref[...], preferred_element_type=F32) + bg_ref[...]
    log_a = (jnp.minimum(xg, 0.0) - jnp.log(1.0 + jnp.exp(-jnp.abs(xg)))) * (1.0 / GLA_TAU)
    for hh in range(GLA_HEADS):
        q_ref[hh] = z[:, GLA_DK * hh:GLA_DK * (hh + 1)].astype(BF16)
        k_ref[hh] = z[:, nk + GLA_DK * hh:nk + GLA_DK * (hh + 1)].astype(BF16)
        g_ref[hh] = log_a[:, GLA_DK * hh:GLA_DK * (hh + 1)]
        v_ref[hh] = z[:, o_v + GLA_DV * hh:o_v + GLA_DV * (hh + 1)].astype(BF16)
    r_ref[...] = z[:, o_r:].astype(BF16)


def _gla_in(h, w, wgu, bg):
    T = h.shape[0]
    tt = TOK_TILE
    GH = GLA_HEADS
    const = lambda shape: pl.BlockSpec(shape, lambda i: (0,) * len(shape))
    hspec = lambda w_: pl.BlockSpec((GH, tt, w_), lambda i: (0, i, 0))
    return pl.pallas_call(
        _gla_in_kernel,
        grid=(T // tt,),
        in_specs=[pl.BlockSpec((tt, D_MODEL), lambda i: (i, 0)), const(w.shape), const(wgu.shape), const(bg.shape)],
        out_specs=[hspec(GLA_DK), hspec(GLA_DK), hspec(GLA_DK), hspec(GLA_DV),
                   pl.BlockSpec((tt, GH * GLA_DV), lambda i: (i, 0))],
        out_shape=[
            jax.ShapeDtypeStruct((GH, T, GLA_DK), BF16),
            jax.ShapeDtypeStruct((GH, T, GLA_DK), BF16),
            jax.ShapeDtypeStruct((GH, T, GLA_DK), F32),
            jax.ShapeDtypeStruct((GH, T, GLA_DV), BF16),
            jax.ShapeDtypeStruct((T, GH * GLA_DV), BF16),
        ],
        compiler_params=_cparams(("arbitrary",)),
        name="gla_in",
    )(h, w, wgu, bg)


def _gla_scan_kernel(q_ref, k_ref, g_ref, v_ref, stack_ref, o_ref, st_ref, *, tile):
    C = SCAN_CHUNK
    t = pl.program_id(2)

    @pl.when(t == 0)
    def _():
        st_ref[...] = jnp.zeros_like(st_ref)

    row = lax.broadcasted_iota(jnp.int32, (C, C), 0)
    col = lax.broadcasted_iota(jnp.int32, (C, C), 1)
    diag = row == col
    masks = [((row // (2 * c)) == (col // (2 * c))) & ((row % (2 * c)) >= c) & ((col % (2 * c)) < c)
             for c in GLA_LEVELS]
    rid = lax.broadcasted_iota(jnp.int32, (C, 1), 0)

    def chunk(ci, carry):
        r0 = pl.multiple_of(ci * C, C)
        q = q_ref[0, pl.ds(r0, C), :].astype(F32)
        k = k_ref[0, pl.ds(r0, C), :].astype(F32)
        g = g_ref[0, pl.ds(r0, C), :]
        v = v_ref[0, pl.ds(r0, C), :]
        k = jnp.where(t * tile + r0 + rid >= PAD_ROWS, k, 0.0)
        g_hi = g.astype(BF16)
        g_lo = (g - g_hi.astype(F32)).astype(BF16)
        sums2 = jnp.dot(stack_ref[...], jnp.concatenate([g_hi, g_lo], axis=1), preferred_element_type=F32)
        sums = sums2[:, :GLA_DK] + sums2[:, GLA_DK:]
        b = sums[0:C]
        b_last = b[C - 1:C]
        st = st_ref[...]
        o = _nt_dot((q * jnp.exp(b)).astype(BF16), st.astype(BF16))
        att = jnp.where(diag, _nt_dot(q.astype(BF16), k.astype(BF16)), 0.0)
        for li in range(len(GLA_LEVELS)):
            ref = sums[C * (li + 1):C * (li + 2)]
            qf = (q * jnp.exp(jnp.minimum(b - ref, 0.0))).astype(BF16)
            kb = (k * jnp.exp(jnp.minimum(ref - b, 0.0))).astype(BF16)
            att = att + jnp.where(masks[li], _nt_dot(qf, kb), 0.0)
        o = o + jnp.dot(att.astype(BF16), v, preferred_element_type=F32)
        o_ref[pl.ds(r0, C), :] = o.astype(BF16)
        kd = (k * jnp.exp(b_last - b)).astype(BF16)
        st_ref[...] = st * jnp.exp(b_last) + _tn_dot(v, kd)
        return carry

    lax.fori_loop(0, tile // C, chunk, 0)


def _gla_scan(q, k, g, v, stack, B, LK):
    GH, T, _ = q.shape
    tile = 640 if LK % 640 == 0 else LANES
    nt = LK // tile
    hspec = lambda w_: pl.BlockSpec((1, tile, w_), lambda b, h, t: (h, b * nt + t, 0))
    return pl.pallas_call(
        functools.partial(_gla_scan_kernel, tile=tile),
        grid=(B, GH, nt),
        in_specs=[hspec(GLA_DK), hspec(GLA_DK), hspec(GLA_DK), hspec(GLA_DV),
                  pl.BlockSpec(stack.shape, lambda b, h, t: (0, 0))],
        out_specs=pl.BlockSpec((tile, GLA_DV), lambda b, h, t: (b * nt + t, h)),
        out_shape=jax.ShapeDtypeStruct((T, GH * GLA_DV), BF16),
        scratch_shapes=[pltpu.VMEM((GLA_DV, GLA_DK), F32)],
        compiler_params=_cparams(("arbitrary", "arbitrary", "arbitrary")),
        name="gla_scan",
    )(q, k, g, v, stack)


def _gla_stack():
    C = SCAN_CHUNK
    t = jnp.arange(C)[:, None]
    u = jnp.arange(C)[None, :]
    mats = [u <= t]
    for c in GLA_LEVELS:
        boundary = (t // (2 * c)) * (2 * c) + c - 1
        mats.append(u <= boundary)
    return jnp.concatenate(mats, axis=0).astype(BF16)


def _router_weights(w_rg, b_rg, w_re, b_re):
    w = jnp.zeros((D_MODEL, LANES), F32).at[:, :N_EXPERTS].set(w_re).at[:, N_EXPERTS:N_EXPERTS + N_GROUPS].set(w_rg)
    b = jnp.zeros((1, LANES), F32).at[0, :N_EXPERTS].set(b_re).at[0, N_EXPERTS:N_EXPERTS + N_GROUPS].set(b_rg)
    w_hi = w.astype(BF16)
    w_lo = (w - w_hi.astype(F32)).astype(BF16)
    return w_hi, w_lo, b


def kernel(x, meta_tokens, norm_mix, norm_ffn, mla_w_in, mla_g_q, mla_w_uq, mla_g_kv, mla_w_ukv, mla_g_qn, mla_g_kn, mla_w_o, gla_w_in, gla_w_gate_up, gla_b_gate, gla_g_out, gla_w_o, moe_w_rg, moe_b_rg, moe_w_re, moe_b_re, moe_w_gate, moe_w_up, moe_w_down):
    B, S, D = x.shape
    assert D == D_MODEL and S % ATT_TILE == 0
    LK = META_BLOCK + S
    T = B * LK
    assert T % TOK_TILE == 0 and TOK_TILE % LANES == 0
    tt = TOK_TILE
    H = MLA_HEADS

    meta = jnp.broadcast_to(meta_tokens.astype(F32)[None], (B, N_META, D))
    h = jnp.concatenate([jnp.zeros((B, PAD_ROWS, D), F32), meta, x.astype(F32)], axis=1).reshape(T, D)

    rows = jnp.arange(LK)
    pos = jnp.where(rows < META_BLOCK, jnp.maximum(rows - PAD_ROWS, 0), rows - META_BLOCK + N_META)
    half = MLA_ROPE // 2
    inv = 1.0 / (ROPE_THETA ** (jnp.arange(half, dtype=F32) / half))
    ang = inv[:, None] * pos.astype(F32)[None, :]
    cos = jnp.tile(jnp.cos(ang), (1, B))
    sin = jnp.tile(jnp.sin(ang), (1, B))

    tri = (jnp.arange(tt)[None, :] < jnp.arange(tt)[:, None]).astype(BF16)

    winT = (mla_w_in[0] * norm_mix[0][:, None]).T.astype(BF16)
    wuqT = (mla_w_uq[0] * mla_g_q[0][:, None]).T.astype(BF16)
    wukv = (mla_w_ukv[0] * mla_g_kv[0][:, None]).reshape(MLA_KV_RANK, H, MLA_NOPE + MLA_V)
    wkT = wukv[:, :, :MLA_NOPE].reshape(MLA_KV_RANK, H * MLA_NOPE).T.astype(BF16)
    wv = jnp.zeros((MLA_KV_RANK, H, LANES), F32).at[:, :, :MLA_V].set(wukv[:, :, MLA_NOPE:])
    wv = wv.reshape(MLA_KV_RANK, H * LANES).astype(BF16)
    vone = jnp.zeros((1, H, LANES), F32).at[:, :, MLA_V].set(1.0).reshape(1, H * LANES)
    gq = jnp.broadcast_to((mla_g_qn[0] * (MLA_QK ** -0.5 * LOG2E))[:, None], (MLA_QK, tt)).astype(F32)
    gk = jnp.broadcast_to(mla_g_kn[0][:, None], (MLA_QK, tt)).astype(F32)
    q, kT, v = _mla_proj(h, winT, wuqT, wkT, wv, vone, gq, gk, cos, sin)
    att = _attention(q, kT, v, B, LK)

    wrh, wrl, br = _router_weights(moe_w_rg[0], moe_b_rg[0], moe_w_re[0], moe_b_re[0])
    h, hn, info, cnt = _mixer_out(h, att, mla_w_o[0].astype(BF16), norm_ffn[0][None, :], wrh, wrl, br, tri)
    wgu = jnp.concatenate([moe_w_gate[0], moe_w_up[0]], axis=-1).astype(BF16)
    h = _moe(h, hn, info, cnt, wgu, moe_w_down[0].astype(BF16))

    nk = GLA_HEADS * GLA_DK
    nv = GLA_HEADS * GLA_DV
    w1 = gla_w_in[0] * norm_mix[1][:, None]
    wg_pad = jnp.zeros((D, LANES), F32).at[:, :GLA_GATE_RANK].set(w1[:, 2 * nk + nv:2 * nk + nv + GLA_GATE_RANK])
    w_all = jnp.concatenate([w1[:, :nk] * (GLA_DK ** -0.5), w1[:, nk:2 * nk + nv], wg_pad,
                             w1[:, 2 * nk + nv + GLA_GATE_RANK:]], axis=1).astype(BF16)
    wgate = jnp.zeros((LANES, nk), F32).at[:GLA_GATE_RANK].set(gla_w_gate_up[0]).astype(BF16)
    gq_, gk_, gg_, gv_, gr_ = _gla_in(h, w_all, wgate, gla_b_gate[0][None, :])
    go = _gla_scan(gq_, gk_, gg_, gv_, _gla_stack(), B, LK)

    wrh, wrl, br = _router_weights(moe_w_rg[1], moe_b_rg[1], moe_w_re[1], moe_b_re[1])
    h, hn, info, cnt = _mixer_out(h, go, gla_w_o[0].astype(BF16), norm_ffn[1][None, :], wrh, wrl, br, tri,
                                  gla_extra=(gr_, gla_g_out[0][None, :]))
    wgu = jnp.concatenate([moe_w_gate[1], moe_w_up[1]], axis=-1).astype(BF16)
    h = _moe(h, hn, info, cnt, wgu, moe_w_down[1].astype(BF16))

    return h.reshape(B, LK, D)[:, META_BLOCK:].astype(x.dtype)
```

```python
import functools

import jax
import jax.numpy as jnp
from jax import lax
from jax.experimental import pallas as pl
from jax.experimental.pallas import tpu as pltpu

F32 = jnp.float32
BF16 = jnp.bfloat16

D_MODEL = 1024
CHUNK = 64
N_META = 16
MLA_HEADS = 16
MLA_Q_RANK = 384
MLA_KV_RANK = 256
MLA_NOPE = 64
MLA_ROPE = 32
MLA_V = 64
MLA_QK = MLA_NOPE + MLA_ROPE
ROPE_THETA = 10000.0
GLA_HEADS = 4
GLA_DK = 128
GLA_DV = 256
GLA_GATE_RANK = 16
GLA_TAU = 16.0
N_GROUPS = 4
EXPERTS_PER_GROUP = 8
N_EXPERTS = N_GROUPS * EXPERTS_PER_GROUP
D_EXPERT = 256
EPS = 1e-6

LANES = 128
META_BLOCK = 128
PAD_ROWS = META_BLOCK - N_META
NEG = -1e30
LOG2E = 1.4426950408889634

TOK_TILE = 512
ATT_TILE = 512
ATT_SAFE_BOUND = 50.0
SCAN_CHUNK = 64
EXP_TILE = 128
CLASS_STRIDE = EXPERTS_PER_GROUP * EXPERTS_PER_GROUP
N_CLASS_SLOTS = N_GROUPS * CLASS_STRIDE
N_CLASSES = N_GROUPS * (EXPERTS_PER_GROUP * (EXPERTS_PER_GROUP - 1) // 2)
ROW_EXT = D_MODEL + LANES
VMEM_LIMIT = 56 * 1024 * 1024

GLA_LEVELS = (32, 16, 8, 4, 2, 1)


def _cparams(sem):
    return pltpu.CompilerParams(dimension_semantics=sem, vmem_limit_bytes=VMEM_LIMIT)


def _nt_dot(a, b):
    return lax.dot_general(a, b, (((1,), (1,)), ((), ())), preferred_element_type=F32)


def _tn_dot(a, b):
    return lax.dot_general(a, b, (((0,), (0,)), ((), ())), preferred_element_type=F32)


def _mla_proj_kernel(h_ref, winT_ref, wuqT_ref, wkT_ref, wv_ref, vone_ref, gq_ref, gk_ref,
                     cos_ref, sin_ref, q_ref, kT_ref, v_ref, stats_ref):
    tt = h_ref.shape[0]
    h = h_ref[...]
    hn = (h * lax.rsqrt(jnp.mean(h * h, axis=-1, keepdims=True) + EPS)).astype(BF16)
    zT = _nt_dot(winT_ref[...], hn)
    cq = zT[0:MLA_Q_RANK]
    ckv = zT[MLA_Q_RANK:MLA_Q_RANK + MLA_KV_RANK]
    kr = zT[MLA_Q_RANK + MLA_KV_RANK:]
    cqn = (cq * lax.rsqrt(jnp.mean(cq * cq, axis=0, keepdims=True) + EPS)).astype(BF16)
    ckvn = ckv * lax.rsqrt(jnp.mean(ckv * ckv, axis=0, keepdims=True) + EPS)
    qT = jnp.dot(wuqT_ref[...], cqn, preferred_element_type=F32)
    kT = jnp.dot(wkT_ref[...], ckvn.astype(BF16), preferred_element_type=F32)
    v = jnp.dot(ckvn.T.astype(BF16), wv_ref[...], preferred_element_type=F32) + vone_ref[...]
    cos = cos_ref[...]
    sin = sin_ref[...]
    gq = gq_ref[...]
    gk = gk_ref[...]
    half = MLA_ROPE // 2
    zpad = jnp.zeros((LANES - MLA_QK - 1, tt), F32)
    kpad = jnp.concatenate([jnp.ones((1, tt), F32), zpad], axis=0)
    spad = jnp.zeros((6, tt), F32)
    kr_ss = jnp.sum(kr * kr, axis=0, keepdims=True)
    for hh in range(MLA_HEADS):
        qh = qT[MLA_QK * hh:MLA_QK * (hh + 1)]
        qn = qh * lax.rsqrt(jnp.sum(qh * qh, axis=0, keepdims=True) * (1.0 / MLA_QK) + EPS) * gq
        x1 = qn[MLA_NOPE:MLA_NOPE + half]
        x2 = qn[MLA_NOPE + half:]
        qmain = jnp.concatenate([qn[:MLA_NOPE], x1 * cos - x2 * sin, x2 * cos + x1 * sin], axis=0)
        qsq = jnp.sum(qmain * qmain, axis=0, keepdims=True)
        q_ref[hh] = jnp.concatenate([qmain, -jnp.sqrt(qsq), zpad], axis=0).T.astype(BF16)
        kn = kT[MLA_NOPE * hh:MLA_NOPE * (hh + 1)]
        rk = lax.rsqrt((jnp.sum(kn * kn, axis=0, keepdims=True) + kr_ss) * (1.0 / MLA_QK) + EPS)
        knn = kn * rk * gk[:MLA_NOPE]
        krn = kr * rk * gk[MLA_NOPE:]
        y1 = krn[:half]
        y2 = krn[half:]
        kmain = jnp.concatenate([knn, y1 * cos - y2 * sin, y2 * cos + y1 * sin], axis=0)
        stats_ref[hh] = jnp.concatenate([jnp.sum(kmain * kmain, axis=0, keepdims=True), qsq, spad], axis=0)
        kfull = jnp.concatenate([kmain, kpad], axis=0).astype(BF16)
        for u in range(tt // LANES):
            kT_ref[hh, u] = kfull[:, LANES * u:LANES * (u + 1)]
        v_ref[hh] = v[:, LANES * hh:LANES * (hh + 1)].astype(BF16)


def _mla_proj(h, winT, wuqT, wkT, wv, vone, gq, gk, cos, sin):
    T = h.shape[0]
    tt = TOK_TILE
    H = MLA_HEADS
    const = lambda shape: pl.BlockSpec(shape, lambda i: (0,) * len(shape))
    return pl.pallas_call(
        _mla_proj_kernel,
        grid=(T // tt,),
        in_specs=[
            pl.BlockSpec((tt, D_MODEL), lambda i: (i, 0)),
            const(winT.shape), const(wuqT.shape), const(wkT.shape), const(wv.shape), const(vone.shape),
            const(gq.shape), const(gk.shape),
            pl.BlockSpec((MLA_ROPE // 2, tt), lambda i: (0, i)),
            pl.BlockSpec((MLA_ROPE // 2, tt), lambda i: (0, i)),
        ],
        out_specs=[
            pl.BlockSpec((H, tt, LANES), lambda i: (0, i, 0)),
            pl.BlockSpec((H, tt // LANES, LANES, LANES), lambda i: (0, i, 0, 0)),
            pl.BlockSpec((H, tt, LANES), lambda i: (0, i, 0)),
            pl.BlockSpec((H, 8, tt), lambda i: (0, 0, i)),
        ],
        out_shape=[
            jax.ShapeDtypeStruct((H, T, LANES), BF16),
            jax.ShapeDtypeStruct((H, T // LANES, LANES, LANES), BF16),
            jax.ShapeDtypeStruct((H, T, LANES), BF16),
            jax.ShapeDtypeStruct((H, 8, T), F32),
        ],
        compiler_params=_cparams(("arbitrary",)),
        name="mla_proj",
    )(h, winT, wuqT, wkT, wv, vone, gq, gk, cos, sin)


def _attn_kernel(q_ref, kT_ref, v_ref, stats_ref, o_ref, *, nf):
    tq = ATT_TILE
    nsub = tq // LANES
    heads = range(2)
    row = lax.broadcasted_iota(jnp.int32, (tq, tq), 0)
    col = lax.broadcasted_iota(jnp.int32, (tq, tq), 1)
    diag_bias = jnp.where((col // CHUNK) <= (row // CHUNK), 0.0, NEG).astype(F32)
    col0 = lax.broadcasted_iota(jnp.int32, (1, LANES), 1)
    bias0 = jnp.where(col0 >= PAD_ROWS, 0.0, NEG).astype(F32)

    def ktile(hh, blk):
        return jnp.concatenate([kT_ref[hh, blk + u] for u in range(nsub)], axis=1)

    def vtile(hh, t):
        return v_ref[hh, pl.ds(pl.multiple_of(META_BLOCK + t * tq, LANES), tq), :]

    def finish(acc):
        return (acc[:, :MLA_V] / acc[:, MLA_V:MLA_V + 1]).astype(BF16)

    def put(r0, n, hh, acc):
        o_ref[pl.ds(r0, n), MLA_V * hh:MLA_V * (hh + 1)] = finish(acc)

    sq_max = [jnp.max(stats_ref[hh], axis=1, keepdims=True) for hh in heads]
    worst = jnp.sqrt(jnp.maximum(sq_max[0][0:1] * sq_max[0][1:2], sq_max[1][0:1] * sq_max[1][1:2]))

    def shifted():
        lane = lax.broadcasted_iota(jnp.int32, (1, LANES), 1)
        col_scale = [jnp.where(lane == MLA_QK, jnp.sqrt(sq_max[hh][0:1]), 1.0).astype(BF16) for hh in heads]

        def with_shift(q, hh):
            return q * col_scale[hh]

        def part(q, kt, vv, bias):
            s = jnp.dot(q, kt, preferred_element_type=F32)
            if bias is not None:
                s = s + bias
            return jnp.dot(jnp.exp2(s).astype(BF16), vv, preferred_element_type=F32)

        for hh in heads:
            q0 = with_shift(q_ref[hh, 0:META_BLOCK, :], hh)
            put(0, META_BLOCK, hh, part(q0, kT_ref[hh, 0], v_ref[hh, 0:META_BLOCK, :], bias0))

        chains = [(hh, r) for hh in heads for r in range(2)]

        def qpair(ii, carry):
            r0 = pl.multiple_of(META_BLOCK + 2 * ii * tq, LANES)
            qs = [with_shift(q_ref[hh, pl.ds(r0 + r * tq, tq), :], hh) for hh, r in chains]

            def kv(j, acc):
                c0 = pl.multiple_of(META_BLOCK + 2 * j * tq, LANES)
                kts = [jnp.concatenate([kT_ref[hh, 1 + 2 * j * nsub + u] for u in range(2 * nsub)], axis=1)
                       for hh in heads]
                vts = [v_ref[hh, pl.ds(c0, 2 * tq), :] for hh in heads]
                return tuple(acc[c] + part(qs[c], kts[hh], vts[hh], None) for c, (hh, r) in enumerate(chains))

            accs = lax.fori_loop(0, ii, kv, tuple(jnp.zeros((tq, LANES), F32) for _ in chains))
            for c, (hh, r) in enumerate(chains):
                acc = accs[c] + part(qs[c], kT_ref[hh, 0], v_ref[hh, 0:META_BLOCK, :], bias0)
                if r == 1:
                    acc = acc + part(qs[c], ktile(hh, 1 + 2 * ii * nsub), vtile(hh, 2 * ii), None)
                acc = acc + part(qs[c], ktile(hh, 1 + (2 * ii + r) * nsub), vtile(hh, 2 * ii + r), diag_bias)
                put(r0 + r * tq, tq, hh, acc)
            return carry

        lax.fori_loop(0, nf // 2, qpair, 0)

    def online():
        def step(q, kt, vv, m, acc, bias):
            s = jnp.dot(q, kt, preferred_element_type=F32)
            if bias is not None:
                s = s + bias
            m_new = jnp.maximum(m, jnp.max(s, axis=-1, keepdims=True))
            p = jnp.exp2(s - m_new)
            acc = jnp.exp2(m - m_new) * acc + jnp.dot(p.astype(BF16), vv, preferred_element_type=F32)
            return m_new, acc

        def first(q, hh, n):
            return step(q, kT_ref[hh, 0], v_ref[hh, 0:META_BLOCK, :], jnp.full((n, 1), NEG, F32),
                        jnp.zeros((n, LANES), F32), bias0)

        for hh in heads:
            put(0, META_BLOCK, hh, first(q_ref[hh, 0:META_BLOCK, :], hh, META_BLOCK)[1])

        def qtile(i, carry):
            r0 = pl.multiple_of(META_BLOCK + i * tq, LANES)
            qs = [q_ref[hh, pl.ds(r0, tq), :] for hh in heads]
            state = []
            for hh in heads:
                state += list(first(qs[hh], hh, tq))

            def kv(j, c):
                out = []
                for hh in heads:
                    out += list(step(qs[hh], ktile(hh, 1 + j * nsub), vtile(hh, j), c[2 * hh], c[2 * hh + 1], None))
                return tuple(out)

            state = lax.fori_loop(0, i, kv, tuple(state))
            for hh in heads:
                m, acc = step(qs[hh], ktile(hh, 1 + i * nsub), vtile(hh, i), state[2 * hh], state[2 * hh + 1],
                              diag_bias)
                put(r0, tq, hh, acc)
            return carry

        lax.fori_loop(0, nf, qtile, 0)

    lax.cond(worst[0, 0] <= ATT_SAFE_BOUND, shifted, online)


def _attention(q, kT, v, stats, B, LK):
    H, T, _ = q.shape
    nf = (LK - META_BLOCK) // ATT_TILE
    nblk = LK // LANES
    return pl.pallas_call(
        functools.partial(_attn_kernel, nf=nf),
        grid=(B, H // 2),
        in_specs=[
            pl.BlockSpec((2, LK, LANES), lambda b, hp: (hp, b, 0)),
            pl.BlockSpec((2, nblk, LANES, LANES), lambda b, hp: (hp, b, 0, 0)),
            pl.BlockSpec((2, LK, LANES), lambda b, hp: (hp, b, 0)),
            pl.BlockSpec((2, 8, LK), lambda b, hp: (hp, 0, b)),
        ],
        out_specs=pl.BlockSpec((LK, 2 * MLA_V), lambda b, hp: (b, hp)),
        out_shape=jax.ShapeDtypeStruct((T, H * MLA_V), BF16),
        compiler_params=_cparams(("arbitrary", "arbitrary")),
        name="mla_attention",
    )(q, kT, v, stats)


def _route(hnew, gffn_ref, wrh_ref, wrl_ref, br_ref, tri_ref, hn_ref, info_ref, cnt_ref, carry_ref):
    tt = hnew.shape[0]
    hn = hnew * lax.rsqrt(jnp.mean(hnew * hnew, axis=-1, keepdims=True) + EPS) * gffn_ref[...]
    hn_hi = hn.astype(BF16)
    hn_lo = (hn - hn_hi.astype(F32)).astype(BF16)
    wrh = wrh_ref[...]
    logits = (jnp.dot(hn_hi, wrh, preferred_element_type=F32)
              + jnp.dot(hn_lo, wrh, preferred_element_type=F32)
              + jnp.dot(hn_hi, wrl_ref[...], preferred_element_type=F32)) + br_ref[...]

    lane = lax.broadcasted_iota(jnp.int32, (tt, LANES), 1)
    lane_f = lane.astype(F32)
    big = float(LANES)
    gmask = (lane >= N_EXPERTS) & (lane < N_EXPERTS + N_GROUPS)
    gl = jnp.where(gmask, logits, NEG)
    gmax = jnp.max(gl, axis=-1, keepdims=True)
    gsel = jnp.min(jnp.where(gl == gmax, lane_f, big), axis=-1, keepdims=True) - float(N_EXPERTS)
    pg = 1.0 / jnp.sum(jnp.exp(gl - gmax), axis=-1, keepdims=True)
    egrp = (lane // EXPERTS_PER_GROUP).astype(F32)
    emask = (lane < N_EXPERTS) & (egrp == gsel)
    el = jnp.where(emask, logits, NEG)
    m1 = jnp.max(el, axis=-1, keepdims=True)
    i1 = jnp.min(jnp.where(el == m1, lane_f, big), axis=-1, keepdims=True)
    el2 = jnp.where(lane_f == i1, NEG, el)
    m2 = jnp.max(el2, axis=-1, keepdims=True)
    i2 = jnp.min(jnp.where(el2 == m2, lane_f, big), axis=-1, keepdims=True)
    t21 = jnp.exp(m2 - m1)
    w1 = pg / (1.0 + t21)
    w2 = w1 * t21

    lo = jnp.minimum(i1, i2)
    hi = jnp.maximum(i1, i2)
    first_is_lo = i1 < i2
    wa = jnp.where(first_is_lo, w1, w2)
    wb = jnp.where(first_is_lo, w2, w1)
    gbase = gsel * float(EXPERTS_PER_GROUP)
    cls = gsel * float(CLASS_STRIDE) + (lo - gbase) * float(EXPERTS_PER_GROUP) + (hi - gbase)
    lane2 = lax.broadcasted_iota(jnp.int32, (tt, N_CLASS_SLOTS), 1).astype(F32)
    sel = lane2 == cls
    oh = jnp.where(sel, 1.0, 0.0)
    before = jnp.dot(tri_ref[...], oh.astype(BF16), preferred_element_type=F32) + carry_ref[...]
    rank = jnp.sum(jnp.where(sel, before, 0.0), axis=-1, keepdims=True)
    carry_ref[...] = carry_ref[...] + jnp.sum(oh, axis=0, keepdims=True)
    info = jnp.where(lane == 0, cls, jnp.where(lane == 1, rank, jnp.where(lane == 2, wa, jnp.where(
        lane == 3, wb, 0.0))))
    hn_ref[:, :D_MODEL] = hn
    hn_ref[:, D_MODEL:] = info
    info_ref[...] = info
    cnt_ref[...] = jnp.broadcast_to(carry_ref[...], cnt_ref.shape)


def _mla_out_kernel(h_ref, a_ref, wo_ref, gffn_ref, wrh_ref, wrl_ref, br_ref, tri_ref,
                    hout_ref, hn_ref, info_ref, cnt_ref, carry_ref):
    @pl.when(pl.program_id(0) == 0)
    def _():
        carry_ref[...] = jnp.zeros_like(carry_ref)

    hnew = h_ref[...] + jnp.dot(a_ref[...], wo_ref[...], preferred_element_type=F32)
    hout_ref[...] = hnew
    _route(hnew, gffn_ref, wrh_ref, wrl_ref, br_ref, tri_ref, hn_ref, info_ref, cnt_ref, carry_ref)


def _gla_out_kernel(h_ref, a_ref, r_ref, gout_ref, wo_ref, gffn_ref, wrh_ref, wrl_ref, br_ref, tri_ref,
                    hout_ref, hn_ref, info_ref, cnt_ref, carry_ref):
    @pl.when(pl.program_id(0) == 0)
    def _():
        carry_ref[...] = jnp.zeros_like(carry_ref)

    o = a_ref[...].astype(F32)
    r = r_ref[...].astype(F32)
    gout = gout_ref[...]
    parts = []
    for hh in range(GLA_HEADS):
        oh = o[:, GLA_DV * hh:GLA_DV * (hh + 1)]
        parts.append(oh * lax.rsqrt(jnp.mean(oh * oh, axis=-1, keepdims=True) + EPS) * gout)
    a = (jnp.concatenate(parts, axis=1) * (r * jax.nn.sigmoid(r))).astype(BF16)
    hnew = h_ref[...] + jnp.dot(a, wo_ref[...], preferred_element_type=F32)
    hout_ref[...] = hnew
    _route(hnew, gffn_ref, wrh_ref, wrl_ref, br_ref, tri_ref, hn_ref, info_ref, cnt_ref, carry_ref)


def _mixer_out(h, a, wo, gffn, wrh, wrl, br, tri, gla_extra=None):
    T = h.shape[0]
    tt = TOK_TILE
    tile = lambda w: pl.BlockSpec((tt, w), lambda i: (i, 0))
    const = lambda shape: pl.BlockSpec(shape, lambda i: (0,) * len(shape))
    if gla_extra is None:
        kern = _mla_out_kernel
        ins = [h, a]
        in_specs = [tile(D_MODEL), tile(a.shape[1])]
    else:
        r, gout = gla_extra
        kern = _gla_out_kernel
        ins = [h, a, r, gout]
        in_specs = [tile(D_MODEL), tile(a.shape[1]), tile(r.shape[1]), const(gout.shape)]
    ins += [wo, gffn, wrh, wrl, br, tri]
    in_specs += [const(wo.shape), const(gffn.shape), const(wrh.shape), const(wrl.shape), const(br.shape),
                 const(tri.shape)]
    return pl.pallas_call(
        kern,
        grid=(T // tt,),
        in_specs=in_specs,
        out_specs=[tile(D_MODEL), tile(ROW_EXT), tile(LANES), pl.BlockSpec((8, N_CLASS_SLOTS), lambda i: (0, 0))],
        out_shape=[
            jax.ShapeDtypeStruct((T, D_MODEL), F32),
            jax.ShapeDtypeStruct((T, ROW_EXT), F32),
            jax.ShapeDtypeStruct((T, LANES), F32),
            jax.ShapeDtypeStruct((8, N_CLASS_SLOTS), F32),
        ],
        scratch_shapes=[pltpu.VMEM((1, N_CLASS_SLOTS), F32)],
        compiler_params=_cparams(("arbitrary",)),
        name="mixer_out_router",
    )(*ins)


def _dispatch_kernel(start_ref, cr_ref, hn_ref, xs_in_ref, xs_ref, sem):
    del xs_in_ref
    tt = hn_ref.shape[0]

    def row_copy(r, d):
        return pltpu.make_async_copy(hn_ref.at[pl.ds(r, 1)], xs_ref.at[pl.ds(d, 1)], sem)

    def issue(r, c):
        row_copy(r, start_ref[cr_ref[0, 0, 2 * r]] + cr_ref[0, 0, 2 * r + 1]).start()
        return c

    lax.fori_loop(0, tt, issue, 0, unroll=8)

    def drain(r, c):
        row_copy(0, 0).wait()
        return c

    lax.fori_loop(0, tt, drain, 0, unroll=8)


def _dispatch(start_row, cr3, hn, xs0):
    T = hn.shape[0]
    tt = TOK_TILE
    return pl.pallas_call(
        _dispatch_kernel,
        grid_spec=pltpu.PrefetchScalarGridSpec(
            num_scalar_prefetch=1,
            grid=(T // tt,),
            in_specs=[
                pl.BlockSpec((1, 1, 2 * tt), lambda i, st: (i, 0, 0), memory_space=pltpu.SMEM),
                pl.BlockSpec((tt, ROW_EXT), lambda i, st: (i, 0)),
                pl.BlockSpec(memory_space=pl.ANY),
            ],
            out_specs=pl.BlockSpec(memory_space=pl.ANY),
            scratch_shapes=[pltpu.SemaphoreType.DMA],
        ),
        out_shape=jax.ShapeDtypeStruct(xs0.shape, xs0.dtype),
        input_output_aliases={3: 0},
        compiler_params=_cparams(("arbitrary",)),
        name="moe_dispatch",
    )(start_row, cr3, hn, xs0)


def _expert_kernel(ea_ref, eb_ref, nv_ref, xs_ref, wgua_ref, wda_ref, wgub_ref, wdb_ref, y_ref):
    del ea_ref, eb_ref
    valid = pl.program_id(0) < nv_ref[0]

    @pl.when(valid)
    def _():
        x = xs_ref[:, :D_MODEL].astype(BF16)
        gua = jnp.dot(x, wgua_ref[0], preferred_element_type=F32)
        gub = jnp.dot(x, wgub_ref[0], preferred_element_type=F32)

        def hidden(gu, w):
            g = gu[:, :D_EXPERT]
            return (g * jax.nn.sigmoid(g) * gu[:, D_EXPERT:] * w).astype(BF16)

        ha = hidden(gua, xs_ref[:, D_MODEL + 2:D_MODEL + 3])
        hb = hidden(gub, xs_ref[:, D_MODEL + 3:D_MODEL + 4])
        y_ref[...] = (jnp.dot(ha, wda_ref[0], preferred_element_type=F32)
                      + jnp.dot(hb, wdb_ref[0], preferred_element_type=F32))

    @pl.when(jnp.logical_not(valid))
    def _():
        y_ref[...] = jnp.zeros_like(y_ref)


def _experts(tile_ea, tile_eb, nvalid, xs, wgu, wd):
    P = xs.shape[0]
    tm = EXP_TILE
    wa = lambda i, ea, eb, nv: (ea[i], 0, 0)
    wb = lambda i, ea, eb, nv: (eb[i], 0, 0)
    return pl.pallas_call(
        _expert_kernel,
        grid_spec=pltpu.PrefetchScalarGridSpec(
            num_scalar_prefetch=3,
            grid=(P // tm,),
            in_specs=[
                pl.BlockSpec((tm, ROW_EXT), lambda i, ea, eb, nv: (jnp.minimum(i, nv[0] - 1), 0)),
                pl.BlockSpec((1, D_MODEL, 2 * D_EXPERT), wa),
                pl.BlockSpec((1, D_EXPERT, D_MODEL), wa),
                pl.BlockSpec((1, D_MODEL, 2 * D_EXPERT), wb),
                pl.BlockSpec((1, D_EXPERT, D_MODEL), wb),
            ],
            out_specs=pl.BlockSpec((tm, D_MODEL), lambda i, ea, eb, nv: (i, 0)),
        ),
        out_shape=jax.ShapeDtypeStruct((P, D_MODEL), F32),
        compiler_params=_cparams(("arbitrary",)),
        name="moe_experts",
    )(tile_ea, tile_eb, nvalid, xs, wgu, wd, wgu, wd)


def _combine_kernel(start_ref, cr_ref, h_ref, y_ref, out_ref, ybuf, sem):
    tt = h_ref.shape[0]

    def row_copy(r, d):
        return pltpu.make_async_copy(y_ref.at[pl.ds(d, 1)], ybuf.at[pl.ds(r, 1)], sem)

    def issue(r, c):
        row_copy(r, start_ref[cr_ref[0, 0, 2 * r]] + cr_ref[0, 0, 2 * r + 1]).start()
        return c

    lax.fori_loop(0, tt, issue, 0, unroll=8)

    def drain(r, c):
        row_copy(0, 0).wait()
        return c

    lax.fori_loop(0, tt, drain, 0, unroll=8)
    out_ref[...] = h_ref[...] + ybuf[...]


def _combine(start_row, cr3, h, y):
    T = h.shape[0]
    tt = TOK_TILE
    return pl.pallas_call(
        _combine_kernel,
        grid_spec=pltpu.PrefetchScalarGridSpec(
            num_scalar_prefetch=1,
            grid=(T // tt,),
            in_specs=[
                pl.BlockSpec((1, 1, 2 * tt), lambda i, st: (i, 0, 0), memory_space=pltpu.SMEM),
                pl.BlockSpec((tt, D_MODEL), lambda i, st: (i, 0)),
                pl.BlockSpec(memory_space=pl.ANY),
            ],
            out_specs=pl.BlockSpec((tt, D_MODEL), lambda i, st: (i, 0)),
            scratch_shapes=[pltpu.VMEM((tt, D_MODEL), F32), pltpu.SemaphoreType.DMA],
        ),
        out_shape=jax.ShapeDtypeStruct((T, D_MODEL), F32),
        compiler_params=_cparams(("arbitrary",)),
        name="moe_combine",
    )(start_row, cr3, h, y)


def _moe(h, hn_ext, info, cnt, wgu, wd, xs_init):
    T = h.shape[0]
    tm = EXP_TILE
    ntiles = xs_init.shape[0] // tm
    counts = cnt[0].astype(jnp.int32)
    tiles_c = (counts + tm - 1) // tm
    cum = jnp.cumsum(tiles_c)
    start_row = ((cum - tiles_c) * tm).astype(jnp.int32)
    tile_class = jnp.minimum(
        jnp.sum(jnp.arange(ntiles, dtype=jnp.int32)[:, None] >= cum[None, :], axis=1), N_CLASS_SLOTS - 1
    ).astype(jnp.int32)
    group_base = (tile_class // CLASS_STRIDE) * EXPERTS_PER_GROUP
    tile_ea = group_base + (tile_class % CLASS_STRIDE) // EXPERTS_PER_GROUP
    tile_eb = group_base + tile_class % EXPERTS_PER_GROUP
    nvalid = cum[-1:].astype(jnp.int32)
    cr3 = info[:, 0:2].astype(jnp.int32).reshape(T // TOK_TILE, 1, 2 * TOK_TILE)
    xs = _dispatch(start_row, cr3, hn_ext, xs_init)
    y = _experts(tile_ea, tile_eb, nvalid, xs, wgu, wd)
    return _combine(start_row, cr3, h, y), xs


def _gla_in_kernel(h_ref, w_ref, wgu_ref, bg_ref, q_ref, k_ref, g_ref, v_ref, r_ref):
    h = h_ref[...]
    hn = (h * lax.rsqrt(jnp.mean(h * h, axis=-1, keepdims=True) + EPS)).astype(BF16)
    z = jnp.dot(hn, w_ref[...], preferred_element_type=F32)
    nk = GLA_HEADS * GLA_DK
    nv = GLA_HEADS * GLA_DV
    o_v = 2 * nk
    o_g = o_v + nv
    o_r = o_g + LANES
    xg = jnp.dot(z[:, o_g:o_r].astype(BF16), wgu_ref[...], preferred_element_type=F32) + bg_ref[...]
    log_a = (jnp.minimum(xg, 0.0) - jnp.log(1.0 + jnp.exp(-jnp.abs(xg)))) * (1.0 / GLA_TAU)
    for hh in range(GLA_HEADS):
        q_ref[hh] = z[:, GLA_DK * hh:GLA_DK * (hh + 1)].astype(BF16)
        k_ref[hh] = z[:, nk + GLA_DK * hh:nk + GLA_DK * (hh + 1)].astype(BF16)
        g_ref[hh] = log_a[:, GLA_DK * hh:GLA_DK * (hh + 1)]
        v_ref[hh] = z[:, o_v + GLA_DV * hh:o_v + GLA_DV * (hh + 1)].astype(BF16)
    r_ref[...] = z[:, o_r:].astype(BF16)


def _gla_in(h, w, wgu, bg):
    T = h.shape[0]
    tt = TOK_TILE
    GH = GLA_HEADS
    const = lambda shape: pl.BlockSpec(shape, lambda i: (0,) * len(shape))
    hspec = lambda w_: pl.BlockSpec((GH, tt, w_), lambda i: (0, i, 0))
    return pl.pallas_call(
        _gla_in_kernel,
        grid=(T // tt,),
        in_specs=[pl.BlockSpec((tt, D_MODEL), lambda i: (i, 0)), const(w.shape), const(wgu.shape), const(bg.shape)],
        out_specs=[hspec(GLA_DK), hspec(GLA_DK), hspec(GLA_DK), hspec(GLA_DV),
                   pl.BlockSpec((tt, GH * GLA_DV), lambda i: (i, 0))],
        out_shape=[
            jax.ShapeDtypeStruct((GH, T, GLA_DK), BF16),
            jax.ShapeDtypeStruct((GH, T, GLA_DK), BF16),
            jax.ShapeDtypeStruct((GH, T, GLA_DK), F32),
            jax.ShapeDtypeStruct((GH, T, GLA_DV), BF16),
            jax.ShapeDtypeStruct((T, GH * GLA_DV), BF16),
        ],
        compiler_params=_cparams(("arbitrary",)),
        name="gla_in",
    )(h, w, wgu, bg)


def _gla_scan_kernel(q_ref, k_ref, g_ref, v_ref, stack_ref, o_ref, st_ref, *, tile):
    C = SCAN_CHUNK
    t = pl.program_id(1)

    @pl.when(t == 0)
    def _():
        st_ref[...] = jnp.zeros_like(st_ref)

    row = lax.broadcasted_iota(jnp.int32, (C, C), 0)
    col = lax.broadcasted_iota(jnp.int32, (C, C), 1)
    diag = row == col
    masks = [((row // (2 * c)) == (col // (2 * c))) & ((row % (2 * c)) >= c) & ((col % (2 * c)) < c)
             for c in GLA_LEVELS]
    rid = lax.broadcasted_iota(jnp.int32, (C, 1), 0)

    def chunk(ci, carry):
        r0 = pl.multiple_of(ci * C, C)
        real = t * tile + r0 + rid >= PAD_ROWS
        for hh in range(GLA_HEADS):
            q = q_ref[hh, pl.ds(r0, C), :].astype(F32)
            k = jnp.where(real, k_ref[hh, pl.ds(r0, C), :].astype(F32), 0.0)
            g = g_ref[hh, pl.ds(r0, C), :]
            v = v_ref[hh, pl.ds(r0, C), :]
            g_hi = g.astype(BF16)
            g_lo = (g - g_hi.astype(F32)).astype(BF16)
            sums2 = jnp.dot(stack_ref[...], jnp.concatenate([g_hi, g_lo], axis=1), preferred_element_type=F32)
            sums = sums2[:, :GLA_DK] + sums2[:, GLA_DK:]
            b = sums[0:C]
            b_last = b[C - 1:C]
            st = st_ref[hh]
            o = _nt_dot((q * jnp.exp(b)).astype(BF16), st.astype(BF16))
            att = jnp.where(diag, _nt_dot(q.astype(BF16), k.astype(BF16)), 0.0)
            for li in range(len(GLA_LEVELS)):
                ref = sums[C * (li + 1):C * (li + 2)]
                qf = (q * jnp.exp(jnp.minimum(b - ref, 0.0))).astype(BF16)
                kb = (k * jnp.exp(jnp.minimum(ref - b, 0.0))).astype(BF16)
                att = att + jnp.where(masks[li], _nt_dot(qf, kb), 0.0)
            o = o + jnp.dot(att.astype(BF16), v, preferred_element_type=F32)
            o_ref[pl.ds(r0, C), GLA_DV * hh:GLA_DV * (hh + 1)] = o.astype(BF16)
            kd = (k * jnp.exp(b_last - b)).astype(BF16)
            st_ref[hh] = st * jnp.exp(b_last) + _tn_dot(v, kd)
        return carry

    lax.fori_loop(0, tile // C, chunk, 0)


def _gla_scan(q, k, g, v, stack, B, LK):
    GH, T, _ = q.shape
    tile = 640 if LK % 640 == 0 else LANES
    nt = LK // tile
    hspec = lambda w_: pl.BlockSpec((GH, tile, w_), lambda b, t: (0, b * nt + t, 0))
    return pl.pallas_call(
        functools.partial(_gla_scan_kernel, tile=tile),
        grid=(B, nt),
        in_specs=[hspec(GLA_DK), hspec(GLA_DK), hspec(GLA_DK), hspec(GLA_DV),
                  pl.BlockSpec(stack.shape, lambda b, t: (0, 0))],
        out_specs=pl.BlockSpec((tile, GH * GLA_DV), lambda b, t: (b * nt + t, 0)),
        out_shape=jax.ShapeDtypeStruct((T, GH * GLA_DV), BF16),
        scratch_shapes=[pltpu.VMEM((GH, GLA_DV, GLA_DK), F32)],
        compiler_params=_cparams(("arbitrary", "arbitrary")),
        name="gla_scan",
    )(q, k, g, v, stack)


def _gla_stack():
    C = SCAN_CHUNK
    t = jnp.arange(C)[:, None]
    u = jnp.arange(C)[None, :]
    mats = [u <= t]
    for c in GLA_LEVELS:
        boundary = (t // (2 * c)) * (2 * c) + c - 1
        mats.append(u <= boundary)
    return jnp.concatenate(mats, axis=0).astype(BF16)


def _router_weights(w_rg, b_rg, w_re, b_re):
    w = jnp.zeros((D_MODEL, LANES), F32).at[:, :N_EXPERTS].set(w_re).at[:, N_EXPERTS:N_EXPERTS + N_GROUPS].set(w_rg)
    b = jnp.zeros((1, LANES), F32).at[0, :N_EXPERTS].set(b_re).at[0, N_EXPERTS:N_EXPERTS + N_GROUPS].set(b_rg)
    w_hi = w.astype(BF16)
    w_lo = (w - w_hi.astype(F32)).astype(BF16)
    return w_hi, w_lo, b


def kernel(x, meta_tokens, norm_mix, norm_ffn, mla_w_in, mla_g_q, mla_w_uq, mla_g_kv, mla_w_ukv, mla_g_qn, mla_g_kn, mla_w_o, gla_w_in, gla_w_gate_up, gla_b_gate, gla_g_out, gla_w_o, moe_w_rg, moe_b_rg, moe_w_re, moe_b_re, moe_w_gate, moe_w_up, moe_w_down):
    B, S, D = x.shape
    assert D == D_MODEL and S % (2 * ATT_TILE) == 0
    LK = META_BLOCK + S
    T = B * LK
    assert T % TOK_TILE == 0 and TOK_TILE % LANES == 0
    tt = TOK_TILE
    H = MLA_HEADS

    meta = jnp.broadcast_to(meta_tokens.astype(F32)[None], (B, N_META, D))
    h = jnp.concatenate([jnp.zeros((B, PAD_ROWS, D), F32), meta, x.astype(F32)], axis=1).reshape(T, D)

    rows = jnp.arange(LK)
    pos = jnp.where(rows < META_BLOCK, jnp.maximum(rows - PAD_ROWS, 0), rows - META_BLOCK + N_META)
    half = MLA_ROPE // 2
    inv = 1.0 / (ROPE_THETA ** (jnp.arange(half, dtype=F32) / half))
    ang = inv[:, None] * pos.astype(F32)[None, :]
    cos = jnp.tile(jnp.cos(ang), (1, B))
    sin = jnp.tile(jnp.sin(ang), (1, B))

    tri = (jnp.arange(tt)[None, :] < jnp.arange(tt)[:, None]).astype(BF16)

    winT = (mla_w_in[0] * norm_mix[0][:, None]).T.astype(BF16)
    wuqT = (mla_w_uq[0] * mla_g_q[0][:, None]).T.astype(BF16)
    wukv = (mla_w_ukv[0] * mla_g_kv[0][:, None]).reshape(MLA_KV_RANK, H, MLA_NOPE + MLA_V)
    wkT = wukv[:, :, :MLA_NOPE].reshape(MLA_KV_RANK, H * MLA_NOPE).T.astype(BF16)
    wv = jnp.zeros((MLA_KV_RANK, H, LANES), F32).at[:, :, :MLA_V].set(wukv[:, :, MLA_NOPE:])
    wv = wv.reshape(MLA_KV_RANK, H * LANES).astype(BF16)
    vone = jnp.zeros((1, H, LANES), F32).at[:, :, MLA_V].set(1.0).reshape(1, H * LANES)
    gq = jnp.broadcast_to((mla_g_qn[0] * (MLA_QK ** -0.5 * LOG2E))[:, None], (MLA_QK, tt)).astype(F32)
    gk = jnp.broadcast_to(mla_g_kn[0][:, None], (MLA_QK, tt)).astype(F32)
    q, kT, v, stats = _mla_proj(h, winT, wuqT, wkT, wv, vone, gq, gk, cos, sin)
    att = _attention(q, kT, v, stats, B, LK)

    wrh, wrl, br = _router_weights(moe_w_rg[0], moe_b_rg[0], moe_w_re[0], moe_b_re[0])
    h, hn, info, cnt = _mixer_out(h, att, mla_w_o[0].astype(BF16), norm_ffn[0][None, :], wrh, wrl, br, tri)
    wgu = jnp.concatenate([moe_w_gate[0], moe_w_up[0]], axis=-1).astype(BF16)
    xs0 = jnp.zeros(((T // EXP_TILE + N_CLASSES) * EXP_TILE, ROW_EXT), F32)
    h, xs1 = _moe(h, hn, info, cnt, wgu, moe_w_down[0].astype(BF16), xs0)

    nk = GLA_HEADS * GLA_DK
    nv = GLA_HEADS * GLA_DV
    w1 = gla_w_in[0] * norm_mix[1][:, None]
    wg_pad = jnp.zeros((D, LANES), F32).at[:, :GLA_GATE_RANK].set(w1[:, 2 * nk + nv:2 * nk + nv + GLA_GATE_RANK])
    w_all = jnp.concatenate([w1[:, :nk] * (GLA_DK ** -0.5), w1[:, nk:2 * nk + nv], wg_pad,
                             w1[:, 2 * nk + nv + GLA_GATE_RANK:]], axis=1).astype(BF16)
    wgate = jnp.zeros((LANES, nk), F32).at[:GLA_GATE_RANK].set(gla_w_gate_up[0]).astype(BF16)
    gq_, gk_, gg_, gv_, gr_ = _gla_in(h, w_all, wgate, gla_b_gate[0][None, :])
    go = _gla_scan(gq_, gk_, gg_, gv_, _gla_stack(), B, LK)

    wrh, wrl, br = _router_weights(moe_w_rg[1], moe_b_rg[1], moe_w_re[1], moe_b_re[1])
    h, hn, info, cnt = _mixer_out(h, go, gla_w_o[0].astype(BF16), norm_ffn[1][None, :], wrh, wrl, br, tri,
                                  gla_extra=(gr_, gla_g_out[0][None, :]))
    wgu = jnp.concatenate([moe_w_gate[1], moe_w_up[1]], axis=-1).astype(BF16)
    h, _ = _moe(h, hn, info, cnt, wgu, moe_w_down[1].astype(BF16), xs1)

    return h.reshape(B, LK, D)[:, META_BLOCK:].astype(x.dtype)
```

```python
import functools

import jax
import jax.numpy as jnp
from jax import lax
from jax.experimental import pallas as pl
from jax.experimental.pallas import tpu as pltpu

F32 = jnp.float32
BF16 = jnp.bfloat16

D_MODEL = 1024
CHUNK = 64
N_META = 16
MLA_HEADS = 16
MLA_Q_RANK = 384
MLA_KV_RANK = 256
MLA_NOPE = 64
MLA_ROPE = 32
MLA_V = 64
MLA_QK = MLA_NOPE + MLA_ROPE
ROPE_THETA = 10000.0
GLA_HEADS = 4
GLA_DK = 128
GLA_DV = 256
GLA_GATE_RANK = 16
GLA_TAU = 16.0
N_GROUPS = 4
EXPERTS_PER_GROUP = 8
N_EXPERTS = N_GROUPS * EXPERTS_PER_GROUP
D_EXPERT = 256
EPS = 1e-6

LANES = 128
META_BLOCK = 128
PAD_ROWS = META_BLOCK - N_META
NEG = -1e30
LOG2E = 1.4426950408889634

TOK_TILE = 512
ATT_TILE = 512
ATT_SAFE_BOUND = 50.0
SCAN_CHUNK = 64
EXP_TILE = 128
CLASS_STRIDE = EXPERTS_PER_GROUP * EXPERTS_PER_GROUP
N_CLASS_SLOTS = N_GROUPS * CLASS_STRIDE
N_CLASSES = N_GROUPS * (EXPERTS_PER_GROUP * (EXPERTS_PER_GROUP - 1) // 2)
ROW_EXT = D_MODEL + LANES
VMEM_LIMIT = 56 * 1024 * 1024

GLA_LEVELS = (32, 16, 8, 4, 2, 1)


def _cparams(sem):
    return pltpu.CompilerParams(dimension_semantics=sem, vmem_limit_bytes=VMEM_LIMIT)


def _nt_dot(a, b):
    return lax.dot_general(a, b, (((1,), (1,)), ((), ())), preferred_element_type=F32)


def _tn_dot(a, b):
    return lax.dot_general(a, b, (((0,), (0,)), ((), ())), preferred_element_type=F32)


def _mla_proj_kernel(h_ref, winT_ref, wuqT_ref, wkT_ref, wv_ref, vone_ref, gq_ref, gk_ref,
                     cos_ref, sin_ref, q_ref, kT_ref, v_ref, stats_ref):
    tt = h_ref.shape[0]
    h = h_ref[...]
    hn = (h * lax.rsqrt(jnp.mean(h * h, axis=-1, keepdims=True) + EPS)).astype(BF16)
    zT = _nt_dot(winT_ref[...], hn)
    cq = zT[0:MLA_Q_RANK]
    ckv = zT[MLA_Q_RANK:MLA_Q_RANK + MLA_KV_RANK]
    kr = zT[MLA_Q_RANK + MLA_KV_RANK:]
    cqn = (cq * lax.rsqrt(jnp.mean(cq * cq, axis=0, keepdims=True) + EPS)).astype(BF16)
    ckvn = ckv * lax.rsqrt(jnp.mean(ckv * ckv, axis=0, keepdims=True) + EPS)
    qT = jnp.dot(wuqT_ref[...], cqn, preferred_element_type=F32)
    kT = jnp.dot(wkT_ref[...], ckvn.astype(BF16), preferred_element_type=F32)
    v = jnp.dot(ckvn.T.astype(BF16), wv_ref[...], preferred_element_type=F32) + vone_ref[...]
    cos = cos_ref[...]
    sin = sin_ref[...]
    gq = gq_ref[...]
    gk = gk_ref[...]
    half = MLA_ROPE // 2
    zpad = jnp.zeros((LANES - MLA_QK - 1, tt), F32)
    kpad = jnp.concatenate([jnp.ones((1, tt), F32), zpad], axis=0)
    spad = jnp.zeros((6, tt), F32)
    kr_ss = jnp.sum(kr * kr, axis=0, keepdims=True)
    for hh in range(MLA_HEADS):
        qh = qT[MLA_QK * hh:MLA_QK * (hh + 1)]
        qn = qh * lax.rsqrt(jnp.sum(qh * qh, axis=0, keepdims=True) * (1.0 / MLA_QK) + EPS) * gq
        x1 = qn[MLA_NOPE:MLA_NOPE + half]
        x2 = qn[MLA_NOPE + half:]
        qmain = jnp.concatenate([qn[:MLA_NOPE], x1 * cos - x2 * sin, x2 * cos + x1 * sin], axis=0)
        qsq = jnp.sum(qmain * qmain, axis=0, keepdims=True)
        q_ref[hh] = jnp.concatenate([qmain, -jnp.sqrt(qsq), zpad], axis=0).T.astype(BF16)
        kn = kT[MLA_NOPE * hh:MLA_NOPE * (hh + 1)]
        rk = lax.rsqrt((jnp.sum(kn * kn, axis=0, keepdims=True) + kr_ss) * (1.0 / MLA_QK) + EPS)
        knn = kn * rk * gk[:MLA_NOPE]
        krn = kr * rk * gk[MLA_NOPE:]
        y1 = krn[:half]
        y2 = krn[half:]
        kmain = jnp.concatenate([knn, y1 * cos - y2 * sin, y2 * cos + y1 * sin], axis=0)
        stats_ref[hh] = jnp.concatenate([jnp.sum(kmain * kmain, axis=0, keepdims=True), qsq, spad], axis=0)
        kfull = jnp.concatenate([kmain, kpad], axis=0).astype(BF16)
        for u in range(tt // LANES):
            kT_ref[hh, u] = kfull[:, LANES * u:LANES * (u + 1)]
        v_ref[hh] = v[:, LANES * hh:LANES * (hh + 1)].astype(BF16)


def _mla_proj(h, winT, wuqT, wkT, wv, vone, gq, gk, cos, sin):
    T = h.shape[0]
    tt = TOK_TILE
    H = MLA_HEADS
    const = lambda shape: pl.BlockSpec(shape, lambda i: (0,) * len(shape))
    return pl.pallas_call(
        _mla_proj_kernel,
        grid=(T // tt,),
        in_specs=[
            pl.BlockSpec((tt, D_MODEL), lambda i: (i, 0)),
            const(winT.shape), const(wuqT.shape), const(wkT.shape), const(wv.shape), const(vone.shape),
            const(gq.shape), const(gk.shape),
            pl.BlockSpec((MLA_ROPE // 2, tt), lambda i: (0, i)),
            pl.BlockSpec((MLA_ROPE // 2, tt), lambda i: (0, i)),
        ],
        out_specs=[
            pl.BlockSpec((H, tt, LANES), lambda i: (0, i, 0)),
            pl.BlockSpec((H, tt // LANES, LANES, LANES), lambda i: (0, i, 0, 0)),
            pl.BlockSpec((H, tt, LANES), lambda i: (0, i, 0)),
            pl.BlockSpec((H, 8, tt), lambda i: (0, 0, i)),
        ],
        out_shape=[
            jax.ShapeDtypeStruct((H, T, LANES), BF16),
            jax.ShapeDtypeStruct((H, T // LANES, LANES, LANES), BF16),
            jax.ShapeDtypeStruct((H, T, LANES), BF16),
            jax.ShapeDtypeStruct((H, 8, T), F32),
        ],
        compiler_params=_cparams(("arbitrary",)),
        name="mla_proj",
    )(h, winT, wuqT, wkT, wv, vone, gq, gk, cos, sin)


def _attn_kernel(q_ref, kT_ref, v_ref, stats_ref, o_ref, *, nf):
    tq = ATT_TILE
    nsub = tq // LANES
    heads = range(2)
    row = lax.broadcasted_iota(jnp.int32, (tq, tq), 0)
    col = lax.broadcasted_iota(jnp.int32, (tq, tq), 1)
    diag_bias = jnp.where((col // CHUNK) <= (row // CHUNK), 0.0, NEG).astype(F32)
    col0 = lax.broadcasted_iota(jnp.int32, (1, LANES), 1)
    bias0 = jnp.where(col0 >= PAD_ROWS, 0.0, NEG).astype(F32)

    def ktile(hh, blk):
        return jnp.concatenate([kT_ref[hh, blk + u] for u in range(nsub)], axis=1)

    def vtile(hh, t):
        return v_ref[hh, pl.ds(pl.multiple_of(META_BLOCK + t * tq, LANES), tq), :]

    def finish(acc):
        return (acc[:, :MLA_V] / acc[:, MLA_V:MLA_V + 1]).astype(BF16)

    def put(r0, n, hh, acc):
        o_ref[pl.ds(r0, n), MLA_V * hh:MLA_V * (hh + 1)] = finish(acc)

    sq_max = [jnp.max(stats_ref[hh], axis=1, keepdims=True) for hh in heads]
    worst = jnp.sqrt(jnp.maximum(sq_max[0][0:1] * sq_max[0][1:2], sq_max[1][0:1] * sq_max[1][1:2]))

    def shifted():
        lane = lax.broadcasted_iota(jnp.int32, (1, LANES), 1)
        col_scale = [jnp.where(lane == MLA_QK, jnp.sqrt(sq_max[hh][0:1]), 1.0).astype(BF16) for hh in heads]

        def with_shift(q, hh):
            return q * col_scale[hh]

        def part(q, kt, vv, bias):
            s = jnp.dot(q, kt, preferred_element_type=F32)
            if bias is not None:
                s = s + bias
            return jnp.dot(jnp.exp2(s).astype(BF16), vv, preferred_element_type=F32)

        for hh in heads:
            q0 = with_shift(q_ref[hh, 0:META_BLOCK, :], hh)
            put(0, META_BLOCK, hh, part(q0, kT_ref[hh, 0], v_ref[hh, 0:META_BLOCK, :], bias0))

        chains = [(hh, r) for hh in heads for r in range(2)]

        def qpair(ii, carry):
            r0 = pl.multiple_of(META_BLOCK + 2 * ii * tq, LANES)
            qs = [with_shift(q_ref[hh, pl.ds(r0 + r * tq, tq), :], hh) for hh, r in chains]

            def kv(j, acc):
                c0 = pl.multiple_of(META_BLOCK + 2 * j * tq, LANES)
                kts = [jnp.concatenate([kT_ref[hh, 1 + 2 * j * nsub + u] for u in range(2 * nsub)], axis=1)
                       for hh in heads]
                vts = [v_ref[hh, pl.ds(c0, 2 * tq), :] for hh in heads]
                return tuple(acc[c] + part(qs[c], kts[hh], vts[hh], None) for c, (hh, r) in enumerate(chains))

            accs = lax.fori_loop(0, ii, kv, tuple(jnp.zeros((tq, LANES), F32) for _ in chains))
            for c, (hh, r) in enumerate(chains):
                acc = accs[c] + part(qs[c], kT_ref[hh, 0], v_ref[hh, 0:META_BLOCK, :], bias0)
                if r == 1:
                    acc = acc + part(qs[c], ktile(hh, 1 + 2 * ii * nsub), vtile(hh, 2 * ii), None)
                acc = acc + part(qs[c], ktile(hh, 1 + (2 * ii + r) * nsub), vtile(hh, 2 * ii + r), diag_bias)
                put(r0 + r * tq, tq, hh, acc)
            return carry

        lax.fori_loop(0, nf // 2, qpair, 0)

    def online():
        def step(q, kt, vv, m, acc, bias):
            s = jnp.dot(q, kt, preferred_element_type=F32)
            if bias is not None:
                s = s + bias
            m_new = jnp.maximum(m, jnp.max(s, axis=-1, keepdims=True))
            p = jnp.exp2(s - m_new)
            acc = jnp.exp2(m - m_new) * acc + jnp.dot(p.astype(BF16), vv, preferred_element_type=F32)
            return m_new, acc

        def first(q, hh, n):
            return step(q, kT_ref[hh, 0], v_ref[hh, 0:META_BLOCK, :], jnp.full((n, 1), NEG, F32),
                        jnp.zeros((n, LANES), F32), bias0)

        for hh in heads:
            put(0, META_BLOCK, hh, first(q_ref[hh, 0:META_BLOCK, :], hh, META_BLOCK)[1])

        def qtile(i, carry):
            r0 = pl.multiple_of(META_BLOCK + i * tq, LANES)
            qs = [q_ref[hh, pl.ds(r0, tq), :] for hh in heads]
            state = []
            for hh in heads:
                state += list(first(qs[hh], hh, tq))

            def kv(j, c):
                out = []
                for hh in heads:
                    out += list(step(qs[hh], ktile(hh, 1 + j * nsub), vtile(hh, j), c[2 * hh], c[2 * hh + 1], None))
                return tuple(out)

            state = lax.fori_loop(0, i, kv, tuple(state))
            for hh in heads:
                m, acc = step(qs[hh], ktile(hh, 1 + i * nsub), vtile(hh, i), state[2 * hh], state[2 * hh + 1],
                              diag_bias)
                put(r0, tq, hh, acc)
            return carry

        lax.fori_loop(0, nf, qtile, 0)

    lax.cond(worst[0, 0] <= ATT_SAFE_BOUND, shifted, online)


def _attention(q, kT, v, stats, B, LK):
    H, T, _ = q.shape
    nf = (LK - META_BLOCK) // ATT_TILE
    nblk = LK // LANES
    return pl.pallas_call(
        functools.partial(_attn_kernel, nf=nf),
        grid=(B, H // 2),
        in_specs=[
            pl.BlockSpec((2, LK, LANES), lambda b, hp: (hp, b, 0)),
            pl.BlockSpec((2, nblk, LANES, LANES), lambda b, hp: (hp, b, 0, 0)),
            pl.BlockSpec((2, LK, LANES), lambda b, hp: (hp, b, 0)),
            pl.BlockSpec((2, 8, LK), lambda b, hp: (hp, 0, b)),
        ],
        out_specs=pl.BlockSpec((LK, 2 * MLA_V), lambda b, hp: (b, hp)),
        out_shape=jax.ShapeDtypeStruct((T, H * MLA_V), BF16),
        compiler_params=_cparams(("arbitrary", "arbitrary")),
        name="mla_attention",
    )(q, kT, v, stats)


def _route(hnew, gffn_ref, wrh_ref, wrl_ref, br_ref, tri_ref, hn_ref, info_ref, cnt_ref, carry_ref):
    tt = hnew.shape[0]
    hn = hnew * lax.rsqrt(jnp.mean(hnew * hnew, axis=-1, keepdims=True) + EPS) * gffn_ref[...]
    hn_hi = hn.astype(BF16)
    hn_lo = (hn - hn_hi.astype(F32)).astype(BF16)
    wrh = wrh_ref[...]
    logits = (jnp.dot(hn_hi, wrh, preferred_element_type=F32)
              + jnp.dot(hn_lo, wrh, preferred_element_type=F32)
              + jnp.dot(hn_hi, wrl_ref[...], preferred_element_type=F32)) + br_ref[...]

    lane = lax.broadcasted_iota(jnp.int32, (tt, LANES), 1)
    lane_f = lane.astype(F32)
    big = float(LANES)
    gmask = (lane >= N_EXPERTS) & (lane < N_EXPERTS + N_GROUPS)
    gl = jnp.where(gmask, logits, NEG)
    gmax = jnp.max(gl, axis=-1, keepdims=True)
    gsel = jnp.min(jnp.where(gl == gmax, lane_f, big), axis=-1, keepdims=True) - float(N_EXPERTS)
    pg = 1.0 / jnp.sum(jnp.exp(gl - gmax), axis=-1, keepdims=True)
    egrp = (lane // EXPERTS_PER_GROUP).astype(F32)
    emask = (lane < N_EXPERTS) & (egrp == gsel)
    el = jnp.where(emask, logits, NEG)
    m1 = jnp.max(el, axis=-1, keepdims=True)
    i1 = jnp.min(jnp.where(el == m1, lane_f, big), axis=-1, keepdims=True)
    el2 = jnp.where(lane_f == i1, NEG, el)
    m2 = jnp.max(el2, axis=-1, keepdims=True)
    i2 = jnp.min(jnp.where(el2 == m2, lane_f, big), axis=-1, keepdims=True)
    t21 = jnp.exp(m2 - m1)
    w1 = pg / (1.0 + t21)
    w2 = w1 * t21

    lo = jnp.minimum(i1, i2)
    hi = jnp.maximum(i1, i2)
    first_is_lo = i1 < i2
    wa = jnp.where(first_is_lo, w1, w2)
    wb = jnp.where(first_is_lo, w2, w1)
    gbase = gsel * float(EXPERTS_PER_GROUP)
    cls = gsel * float(CLASS_STRIDE) + (lo - gbase) * float(EXPERTS_PER_GROUP) + (hi - gbase)
    lane2 = lax.broadcasted_iota(jnp.int32, (tt, N_CLASS_SLOTS), 1).astype(F32)
    sel = lane2 == cls
    oh = jnp.where(sel, 1.0, 0.0)
    before = jnp.dot(tri_ref[...], oh.astype(BF16), preferred_element_type=F32) + carry_ref[...]
    rank = jnp.sum(jnp.where(sel, before, 0.0), axis=-1, keepdims=True)
    carry_ref[...] = carry_ref[...] + jnp.sum(oh, axis=0, keepdims=True)
    info = jnp.where(lane == 0, cls, jnp.where(lane == 1, rank, jnp.where(lane == 2, wa, jnp.where(
        lane == 3, wb, 0.0))))
    hn_ref[:, :D_MODEL] = hn
    hn_ref[:, D_MODEL:] = info
    info_ref[...] = info
    cnt_ref[...] = jnp.broadcast_to(carry_ref[...], cnt_ref.shape)


def _mla_out_kernel(h_ref, a_ref, wo_ref, gffn_ref, wrh_ref, wrl_ref, br_ref, tri_ref,
                    hout_ref, hn_ref, info_ref, cnt_ref, carry_ref):
    @pl.when(pl.program_id(0) == 0)
    def _():
        carry_ref[...] = jnp.zeros_like(carry_ref)

    hnew = h_ref[...] + jnp.dot(a_ref[...], wo_ref[...], preferred_element_type=F32)
    hout_ref[...] = hnew
    _route(hnew, gffn_ref, wrh_ref, wrl_ref, br_ref, tri_ref, hn_ref, info_ref, cnt_ref, carry_ref)


def _gla_out_kernel(h_ref, a_ref, r_ref, gout_ref, wo_ref, gffn_ref, wrh_ref, wrl_ref, br_ref, tri_ref,
                    hout_ref, hn_ref, info_ref, cnt_ref, carry_ref):
    @pl.when(pl.program_id(0) == 0)
    def _():
        carry_ref[...] = jnp.zeros_like(carry_ref)

    o = a_ref[...].astype(F32)
    r = r_ref[...].astype(F32)
    gout = gout_ref[...]
    parts = []
    for hh in range(GLA_HEADS):
        oh = o[:, GLA_DV * hh:GLA_DV * (hh + 1)]
        parts.append(oh * lax.rsqrt(jnp.mean(oh * oh, axis=-1, keepdims=True) + EPS) * gout)
    a = (jnp.concatenate(parts, axis=1) * (r * jax.nn.sigmoid(r))).astype(BF16)
    hnew = h_ref[...] + jnp.dot(a, wo_ref[...], preferred_element_type=F32)
    hout_ref[...] = hnew
    _route(hnew, gffn_ref, wrh_ref, wrl_ref, br_ref, tri_ref, hn_ref, info_ref, cnt_ref, carry_ref)


def _mixer_out(h, a, wo, gffn, wrh, wrl, br, tri, gla_extra=None):
    T = h.shape[0]
    tt = TOK_TILE
    tile = lambda w: pl.BlockSpec((tt, w), lambda i: (i, 0))
    const = lambda shape: pl.BlockSpec(shape, lambda i: (0,) * len(shape))
    if gla_extra is None:
        kern = _mla_out_kernel
        ins = [h, a]
        in_specs = [tile(D_MODEL), tile(a.shape[1])]
    else:
        r, gout = gla_extra
        kern = _gla_out_kernel
        ins = [h, a, r, gout]
        in_specs = [tile(D_MODEL), tile(a.shape[1]), tile(r.shape[1]), const(gout.shape)]
    ins += [wo, gffn, wrh, wrl, br, tri]
    in_specs += [const(wo.shape), const(gffn.shape), const(wrh.shape), const(wrl.shape), const(br.shape),
                 const(tri.shape)]
    return pl.pallas_call(
        kern,
        grid=(T // tt,),
        in_specs=in_specs,
        out_specs=[tile(D_MODEL), tile(ROW_EXT), tile(LANES), pl.BlockSpec((8, N_CLASS_SLOTS), lambda i: (0, 0))],
        out_shape=[
            jax.ShapeDtypeStruct((T, D_MODEL), F32),
            jax.ShapeDtypeStruct((T, ROW_EXT), F32),
            jax.ShapeDtypeStruct((T, LANES), F32),
            jax.ShapeDtypeStruct((8, N_CLASS_SLOTS), F32),
        ],
        scratch_shapes=[pltpu.VMEM((1, N_CLASS_SLOTS), F32)],
        compiler_params=_cparams(("arbitrary",)),
        name="mixer_out_router",
    )(*ins)


def _dispatch_kernel(start_ref, cr_ref, hn_ref, xs_in_ref, xs_ref, sem):
    del xs_in_ref
    tt = hn_ref.shape[0]

    def row_copy(r, d):
        return pltpu.make_async_copy(hn_ref.at[pl.ds(r, 1)], xs_ref.at[pl.ds(d, 1)], sem)

    def issue(r, c):
        row_copy(r, start_ref[cr_ref[0, 0, 2 * r]] + cr_ref[0, 0, 2 * r + 1]).start()
        return c

    lax.fori_loop(0, tt, issue, 0, unroll=8)

    def drain(r, c):
        row_copy(0, 0).wait()
        return c

    lax.fori_loop(0, tt, drain, 0, unroll=8)


def _dispatch(start_row, cr3, hn, xs0):
    T = hn.shape[0]
    tt = TOK_TILE
    return pl.pallas_call(
        _dispatch_kernel,
        grid_spec=pltpu.PrefetchScalarGridSpec(
            num_scalar_prefetch=1,
            grid=(T // tt,),
            in_specs=[
                pl.BlockSpec((1, 1, 2 * tt), lambda i, st: (i, 0, 0), memory_space=pltpu.SMEM),
                pl.BlockSpec((tt, ROW_EXT), lambda i, st: (i, 0)),
                pl.BlockSpec(memory_space=pl.ANY),
            ],
            out_specs=pl.BlockSpec(memory_space=pl.ANY),
            scratch_shapes=[pltpu.SemaphoreType.DMA],
        ),
        out_shape=jax.ShapeDtypeStruct(xs0.shape, xs0.dtype),
        input_output_aliases={3: 0},
        compiler_params=_cparams(("arbitrary",)),
        name="moe_dispatch",
    )(start_row, cr3, hn, xs0)


def _expert_kernel(ea_ref, eb_ref, nv_ref, xs_ref, wgua_ref, wda_ref, wgub_ref, wdb_ref, y_ref):
    del ea_ref, eb_ref
    valid = pl.program_id(0) < nv_ref[0]

    @pl.when(valid)
    def _():
        x = xs_ref[:, :D_MODEL].astype(BF16)
        gua = jnp.dot(x, wgua_ref[0], preferred_element_type=F32)
        gub = jnp.dot(x, wgub_ref[0], preferred_element_type=F32)

        def hidden(gu, w):
            g = gu[:, :D_EXPERT]
            return (g * jax.nn.sigmoid(g) * gu[:, D_EXPERT:] * w).astype(BF16)

        ha = hidden(gua, xs_ref[:, D_MODEL + 2:D_MODEL + 3])
        hb = hidden(gub, xs_ref[:, D_MODEL + 3:D_MODEL + 4])
        y_ref[...] = (jnp.dot(ha, wda_ref[0], preferred_element_type=F32)
                      + jnp.dot(hb, wdb_ref[0], preferred_element_type=F32))

    @pl.when(jnp.logical_not(valid))
    def _():
        y_ref[...] = jnp.zeros_like(y_ref)


def _experts(tile_ea, tile_eb, nvalid, xs, wgu, wd):
    P = xs.shape[0]
    tm = EXP_TILE
    wa = lambda i, ea, eb, nv: (ea[i], 0, 0)
    wb = lambda i, ea, eb, nv: (eb[i], 0, 0)
    return pl.pallas_call(
        _expert_kernel,
        grid_spec=pltpu.PrefetchScalarGridSpec(
            num_scalar_prefetch=3,
            grid=(P // tm,),
            in_specs=[
                pl.BlockSpec((tm, ROW_EXT), lambda i, ea, eb, nv: (jnp.minimum(i, nv[0] - 1), 0)),
                pl.BlockSpec((1, D_MODEL, 2 * D_EXPERT), wa),
                pl.BlockSpec((1, D_EXPERT, D_MODEL), wa),
                pl.BlockSpec((1, D_MODEL, 2 * D_EXPERT), wb),
                pl.BlockSpec((1, D_EXPERT, D_MODEL), wb),
            ],
            out_specs=pl.BlockSpec((tm, D_MODEL), lambda i, ea, eb, nv: (i, 0)),
        ),
        out_shape=jax.ShapeDtypeStruct((P, D_MODEL), F32),
        compiler_params=_cparams(("arbitrary",)),
        name="moe_experts",
    )(tile_ea, tile_eb, nvalid, xs, wgu, wd, wgu, wd)


def _combine_kernel(start_ref, cr_ref, h_ref, y_ref, out_ref, ybuf, sem):
    tt = h_ref.shape[0]

    def row_copy(r, d):
        return pltpu.make_async_copy(y_ref.at[pl.ds(d, 1)], ybuf.at[pl.ds(r, 1)], sem)

    def issue(r, c):
        row_copy(r, start_ref[cr_ref[0, 0, 2 * r]] + cr_ref[0, 0, 2 * r + 1]).start()
        return c

    lax.fori_loop(0, tt, issue, 0, unroll=8)

    def drain(r, c):
        row_copy(0, 0).wait()
        return c

    lax.fori_loop(0, tt, drain, 0, unroll=8)
    out_ref[...] = h_ref[...] + ybuf[...]


def _combine(start_row, cr3, h, y):
    T = h.shape[0]
    tt = TOK_TILE
    return pl.pallas_call(
        _combine_kernel,
        grid_spec=pltpu.PrefetchScalarGridSpec(
            num_scalar_prefetch=1,
            grid=(T // tt,),
            in_specs=[
                pl.BlockSpec((1, 1, 2 * tt), lambda i, st: (i, 0, 0), memory_space=pltpu.SMEM),
                pl.BlockSpec((tt, D_MODEL), lambda i, st: (i, 0)),
                pl.BlockSpec(memory_space=pl.ANY),
            ],
            out_specs=pl.BlockSpec((tt, D_MODEL), lambda i, st: (i, 0)),
            scratch_shapes=[pltpu.VMEM((tt, D_MODEL), F32), pltpu.SemaphoreType.DMA],
        ),
        out_shape=jax.ShapeDtypeStruct((T, D_MODEL), F32),
        compiler_params=_cparams(("arbitrary",)),
        name="moe_combine",
    )(start_row, cr3, h, y)


def _moe(h, hn_ext, info, cnt, wgu, wd, xs_init):
    T = h.shape[0]
    tm = EXP_TILE
    ntiles = xs_init.shape[0] // tm
    counts = cnt[0].astype(jnp.int32)
    tiles_c = (counts + tm - 1) // tm
    cum = jnp.cumsum(tiles_c)
    start_row = ((cum - tiles_c) * tm).astype(jnp.int32)
    tile_class = jnp.minimum(
        jnp.sum(jnp.arange(ntiles, dtype=jnp.int32)[:, None] >= cum[None, :], axis=1), N_CLASS_SLOTS - 1
    ).astype(jnp.int32)
    group_base = (tile_class // CLASS_STRIDE) * EXPERTS_PER_GROUP
    tile_ea = group_base + (tile_class % CLASS_STRIDE) // EXPERTS_PER_GROUP
    tile_eb = group_base + tile_class % EXPERTS_PER_GROUP
    nvalid = cum[-1:].astype(jnp.int32)
    cr3 = info[:, 0:2].astype(jnp.int32).reshape(T // TOK_TILE, 1, 2 * TOK_TILE)
    xs = _dispatch(start_row, cr3, hn_ext, xs_init)
    y = _experts(tile_ea, tile_eb, nvalid, xs, wgu, wd)
    return _combine(start_row, cr3, h, y), xs


def _gla_in_kernel(h_ref, w_ref, wgu_ref, bg_ref, q_ref, k_ref, g_ref, v_ref, r_ref):
    h = h_ref[...]
    hn = (h * lax.rsqrt(jnp.mean(h * h, axis=-1, keepdims=True) + EPS)).astype(BF16)
    z = jnp.dot(hn, w_ref[...], preferred_element_type=F32)
    nk = GLA_HEADS * GLA_DK
    nv = GLA_HEADS * GLA_DV
    o_v = 2 * nk
    o_g = o_v + nv
    o_r = o_g + LANES
    xg = jnp.dot(z[:, o_g:o_r].astype(BF16), wgu_ref[...], preferred_element_type=F32) + bg_ref[...]
    log_a = (jnp.minimum(xg, 0.0) - jnp.log(1.0 + jnp.exp(-jnp.abs(xg)))) * (1.0 / GLA_TAU)
    for hh in range(GLA_HEADS):
        q_ref[hh] = z[:, GLA_DK * hh:GLA_DK * (hh + 1)].astype(BF16)
        k_ref[hh] = z[:, nk + GLA_DK * hh:nk + GLA_DK * (hh + 1)].astype(BF16)
        g_ref[hh] = log_a[:, GLA_DK * hh:GLA_DK * (hh + 1)]
        v_ref[hh] = z[:, o_v + GLA_DV * hh:o_v + GLA_DV * (hh + 1)].astype(BF16)
    r_ref[...] = z[:, o_r:].astype(BF16)


def _gla_in(h, w, wgu, bg):
    T = h.shape[0]
    tt = TOK_TILE
    GH = GLA_HEADS
    const = lambda shape: pl.BlockSpec(shape, lambda i: (0,) * len(shape))
    hspec = lambda w_: pl.BlockSpec((GH, tt, w_), lambda i: (0, i, 0))
    return pl.pallas_call(
        _gla_in_kernel,
        grid=(T // tt,),
        in_specs=[pl.BlockSpec((tt, D_MODEL), lambda i: (i, 0)), const(w.shape), const(wgu.shape), const(bg.shape)],
        out_specs=[hspec(GLA_DK), hspec(GLA_DK), hspec(GLA_DK), hspec(GLA_DV),
                   pl.BlockSpec((tt, GH * GLA_DV), lambda i: (i, 0))],
        out_shape=[
            jax.ShapeDtypeStruct((GH, T, GLA_DK), BF16),
            jax.ShapeDtypeStruct((GH, T, GLA_DK), BF16),
            jax.ShapeDtypeStruct((GH, T, GLA_DK), F32),
            jax.ShapeDtypeStruct((GH, T, GLA_DV), BF16),
            jax.ShapeDtypeStruct((T, GH * GLA_DV), BF16),
        ],
        compiler_params=_cparams(("arbitrary",)),
        name="gla_in",
    )(h, w, wgu, bg)


def _bdot(a, b, contract_b):
    return lax.dot_general(a, b, (((2,), (contract_b,)), ((0,), (0,))), preferred_element_type=F32)


def _gla_scan_kernel(q_ref, k_ref, g_ref, v_ref, stack_ref, o_ref, st_ref, oin_ref, qe_ref, kd_ref, dec_ref, *,
                     tile):
    C = SCAN_CHUNK
    nc = tile // C
    t = pl.program_id(1)

    @pl.when(t == 0)
    def _():
        st_ref[...] = jnp.zeros_like(st_ref)

    row = lax.broadcasted_iota(jnp.int32, (C, C), 0)
    col = lax.broadcasted_iota(jnp.int32, (C, C), 1)
    diag = (row == col)[None]
    masks = [(((row // (2 * c)) == (col // (2 * c))) & ((row % (2 * c)) >= c) & ((col % (2 * c)) < c))[None]
             for c in GLA_LEVELS]
    real = (t * tile + lax.broadcasted_iota(jnp.int32, (tile, 1), 0) >= PAD_ROWS).reshape(nc, C, 1)
    stack = stack_ref[...]

    for hh in range(GLA_HEADS):
        q = q_ref[hh].astype(F32).reshape(nc, C, GLA_DK)
        k = jnp.where(real, k_ref[hh].astype(F32).reshape(nc, C, GLA_DK), 0.0)
        g = g_ref[hh].reshape(nc, C, GLA_DK)
        v = v_ref[hh].reshape(nc, C, GLA_DV)
        g_hi = g.astype(BF16)
        g_lo = (g - g_hi.astype(F32)).astype(BF16)
        sums2 = _bdot(stack, jnp.concatenate([g_hi, g_lo], axis=2), 1)
        sums = sums2[:, :, :GLA_DK] + sums2[:, :, GLA_DK:]
        b = sums[:, 0:C]
        b_last = b[:, C - 1:C]
        att = jnp.where(diag, _bdot(q.astype(BF16), k.astype(BF16), 2), 0.0)
        for li in range(len(GLA_LEVELS)):
            ref = sums[:, C * (li + 1):C * (li + 2)]
            qf = (q * jnp.exp(jnp.minimum(b - ref, 0.0))).astype(BF16)
            kb = (k * jnp.exp(jnp.minimum(ref - b, 0.0))).astype(BF16)
            att = att + jnp.where(masks[li], _bdot(qf, kb, 2), 0.0)
        oin_ref[hh] = _bdot(att.astype(BF16), v, 1).reshape(tile, GLA_DV)
        qe_ref[hh] = (q * jnp.exp(b)).astype(BF16).reshape(tile, GLA_DK)
        kd_ref[hh] = (k * jnp.exp(b_last - b)).astype(BF16).reshape(tile, GLA_DK)
        dec_ref[hh] = jnp.broadcast_to(jnp.exp(b_last), (nc, 8, GLA_DK))

    def chunk(ci, carry):
        r0 = pl.multiple_of(ci * C, C)
        for hh in range(GLA_HEADS):
            st = st_ref[hh]
            o = oin_ref[hh, pl.ds(r0, C), :] + _nt_dot(qe_ref[hh, pl.ds(r0, C), :], st.astype(BF16))
            o_ref[pl.ds(r0, C), GLA_DV * hh:GLA_DV * (hh + 1)] = o.astype(BF16)
            st_ref[hh] = st * dec_ref[hh, ci][0:1] + _tn_dot(v_ref[hh, pl.ds(r0, C), :], kd_ref[hh, pl.ds(r0, C), :])
        return carry

    lax.fori_loop(0, nc, chunk, 0)


def _gla_scan(q, k, g, v, stack, B, LK):
    GH, T, _ = q.shape
    tile = 640 if LK % 640 == 0 else LANES
    nt = LK // tile
    nc = tile // SCAN_CHUNK
    stack3 = jnp.broadcast_to(stack[None], (nc,) + stack.shape)
    hspec = lambda w_: pl.BlockSpec((GH, tile, w_), lambda b, t: (0, b * nt + t, 0))
    return pl.pallas_call(
        functools.partial(_gla_scan_kernel, tile=tile),
        grid=(B, nt),
        in_specs=[hspec(GLA_DK), hspec(GLA_DK), hspec(GLA_DK), hspec(GLA_DV),
                  pl.BlockSpec(stack3.shape, lambda b, t: (0, 0, 0))],
        out_specs=pl.BlockSpec((tile, GH * GLA_DV), lambda b, t: (b * nt + t, 0)),
        out_shape=jax.ShapeDtypeStruct((T, GH * GLA_DV), BF16),
        scratch_shapes=[
            pltpu.VMEM((GH, GLA_DV, GLA_DK), F32),
            pltpu.VMEM((GH, tile, GLA_DV), F32),
            pltpu.VMEM((GH, tile, GLA_DK), BF16),
            pltpu.VMEM((GH, tile, GLA_DK), BF16),
            pltpu.VMEM((GH, nc, 8, GLA_DK), F32),
        ],
        compiler_params=_cparams(("arbitrary", "arbitrary")),
        name="gla_scan",
    )(q, k, g, v, stack3)


def _gla_stack():
    C = SCAN_CHUNK
    t = jnp.arange(C)[:, None]
    u = jnp.arange(C)[None, :]
    mats = [u <= t]
    for c in GLA_LEVELS:
        boundary = (t // (2 * c)) * (2 * c) + c - 1
        mats.append(u <= boundary)
    return jnp.concatenate(mats, axis=0).astype(BF16)


def _router_weights(w_rg, b_rg, w_re, b_re):
    w = jnp.zeros((D_MODEL, LANES), F32).at[:, :N_EXPERTS].set(w_re).at[:, N_EXPERTS:N_EXPERTS + N_GROUPS].set(w_rg)
    b = jnp.zeros((1, LANES), F32).at[0, :N_EXPERTS].set(b_re).at[0, N_EXPERTS:N_EXPERTS + N_GROUPS].set(b_rg)
    w_hi = w.astype(BF16)
    w_lo = (w - w_hi.astype(F32)).astype(BF16)
    return w_hi, w_lo, b


def kernel(x, meta_tokens, norm_mix, norm_ffn, mla_w_in, mla_g_q, mla_w_uq, mla_g_kv, mla_w_ukv, mla_g_qn, mla_g_kn, mla_w_o, gla_w_in, gla_w_gate_up, gla_b_gate, gla_g_out, gla_w_o, moe_w_rg, moe_b_rg, moe_w_re, moe_b_re, moe_w_gate, moe_w_up, moe_w_down):
    B, S, D = x.shape
    assert D == D_MODEL and S % (2 * ATT_TILE) == 0
    LK = META_BLOCK + S
    T = B * LK
    assert T % TOK_TILE == 0 and TOK_TILE % LANES == 0
    tt = TOK_TILE
    H = MLA_HEADS

    meta = jnp.broadcast_to(meta_tokens.astype(F32)[None], (B, N_META, D))
    h = jnp.zeros((B, LK, D), F32).at[:, PAD_ROWS:META_BLOCK].set(meta).at[:, META_BLOCK:].set(x.astype(F32))
    h = h.reshape(T, D)

    rows = jnp.arange(LK)
    pos = jnp.where(rows < META_BLOCK, jnp.maximum(rows - PAD_ROWS, 0), rows - META_BLOCK + N_META)
    half = MLA_ROPE // 2
    inv = 1.0 / (ROPE_THETA ** (jnp.arange(half, dtype=F32) / half))
    ang = inv[:, None] * pos.astype(F32)[None, :]
    cos = jnp.tile(jnp.cos(ang), (1, B))
    sin = jnp.tile(jnp.sin(ang), (1, B))

    tri = (jnp.arange(tt)[None, :] < jnp.arange(tt)[:, None]).astype(BF16)

    winT = (mla_w_in[0] * norm_mix[0][:, None]).T.astype(BF16)
    wuqT = (mla_w_uq[0] * mla_g_q[0][:, None]).T.astype(BF16)
    wukv = (mla_w_ukv[0] * mla_g_kv[0][:, None]).reshape(MLA_KV_RANK, H, MLA_NOPE + MLA_V)
    wkT = wukv[:, :, :MLA_NOPE].reshape(MLA_KV_RANK, H * MLA_NOPE).T.astype(BF16)
    wv = jnp.zeros((MLA_KV_RANK, H, LANES), F32).at[:, :, :MLA_V].set(wukv[:, :, MLA_NOPE:])
    wv = wv.reshape(MLA_KV_RANK, H * LANES).astype(BF16)
    vone = jnp.zeros((1, H, LANES), F32).at[:, :, MLA_V].set(1.0).reshape(1, H * LANES)
    gq = jnp.broadcast_to((mla_g_qn[0] * (MLA_QK ** -0.5 * LOG2E))[:, None], (MLA_QK, tt)).astype(F32)
    gk = jnp.broadcast_to(mla_g_kn[0][:, None], (MLA_QK, tt)).astype(F32)
    q, kT, v, stats = _mla_proj(h, winT, wuqT, wkT, wv, vone, gq, gk, cos, sin)
    att = _attention(q, kT, v, stats, B, LK)

    wrh, wrl, br = _router_weights(moe_w_rg[0], moe_b_rg[0], moe_w_re[0], moe_b_re[0])
    h, hn, info, cnt = _mixer_out(h, att, mla_w_o[0].astype(BF16), norm_ffn[0][None, :], wrh, wrl, br, tri)
    wgu = jnp.concatenate([moe_w_gate[0], moe_w_up[0]], axis=-1).astype(BF16)
    xs0 = jnp.zeros(((T // EXP_TILE + N_CLASSES) * EXP_TILE, ROW_EXT), F32)
    h, xs1 = _moe(h, hn, info, cnt, wgu, moe_w_down[0].astype(BF16), xs0)

    nk = GLA_HEADS * GLA_DK
    nv = GLA_HEADS * GLA_DV
    w1 = gla_w_in[0] * norm_mix[1][:, None]
    wg_pad = jnp.zeros((D, LANES), F32).at[:, :GLA_GATE_RANK].set(w1[:, 2 * nk + nv:2 * nk + nv + GLA_GATE_RANK])
    w_all = jnp.concatenate([w1[:, :nk] * (GLA_DK ** -0.5), w1[:, nk:2 * nk + nv], wg_pad,
                             w1[:, 2 * nk + nv + GLA_GATE_RANK:]], axis=1).astype(BF16)
    wgate = jnp.zeros((LANES, nk), F32).at[:GLA_GATE_RANK].set(gla_w_gate_up[0]).astype(BF16)
    gq_, gk_, gg_, gv_, gr_ = _gla_in(h, w_all, wgate, gla_b_gate[0][None, :])
    go = _gla_scan(gq_, gk_, gg_, gv_, _gla_stack(), B, LK)

    wrh, wrl, br = _router_weights(moe_w_rg[1], moe_b_rg[1], moe_w_re[1], moe_b_re[1])
    h, hn, info, cnt = _mixer_out(h, go, gla_w_o[0].astype(BF16), norm_ffn[1][None, :], wrh, wrl, br, tri,
                                  gla_extra=(gr_, gla_g_out[0][None, :]))
    wgu = jnp.concatenate([moe_w_gate[1], moe_w_up[1]], axis=-1).astype(BF16)
    h, _ = _moe(h, hn, info, cnt, wgu, moe_w_down[1].astype(BF16), xs1)

    return h.reshape(B, LK, D)[:, META_BLOCK:].astype(x.dtype)
```

```python
import functools

import jax
import jax.numpy as jnp
from jax import lax
from jax.experimental import pallas as pl
from jax.experimental.pallas import tpu as pltpu

F32 = jnp.float32
BF16 = jnp.bfloat16

D_MODEL = 1024
CHUNK = 64
N_META = 16
MLA_HEADS = 16
MLA_Q_RANK = 384
MLA_KV_RANK = 256
MLA_NOPE = 64
MLA_ROPE = 32
MLA_V = 64
MLA_QK = MLA_NOPE + MLA_ROPE
ROPE_THETA = 10000.0
GLA_HEADS = 4
GLA_DK = 128
GLA_DV = 256
GLA_GATE_RANK = 16
GLA_TAU = 16.0
N_GROUPS = 4
EXPERTS_PER_GROUP = 8
N_EXPERTS = N_GROUPS * EXPERTS_PER_GROUP
D_EXPERT = 256
EPS = 1e-6

LANES = 128
SUBLANES = 8
META_BLOCK = 128
PAD_ROWS = META_BLOCK - N_META
NEG = -1e30
LOG2E = 1.4426950408889634

TOK_TILE = 512
ATT_TILE = 512
ATT_SAFE_BOUND = 50.0
SCAN_CHUNK = 64
EXP_TILE = 128
CLASS_STRIDE = EXPERTS_PER_GROUP * EXPERTS_PER_GROUP
N_CLASS_SLOTS = N_GROUPS * CLASS_STRIDE
N_CLASSES = N_GROUPS * (EXPERTS_PER_GROUP * (EXPERTS_PER_GROUP - 1) // 2)
ROW_EXT = D_MODEL + LANES
VMEM_LIMIT = 56 * 1024 * 1024

GLA_LEVELS = (32, 16, 8, 4, 2, 1)


def _cparams(sem):
    return pltpu.CompilerParams(dimension_semantics=sem, vmem_limit_bytes=VMEM_LIMIT)


def _nt_dot(a, b):
    return lax.dot_general(a, b, (((1,), (1,)), ((), ())), preferred_element_type=F32)


def _tn_dot(a, b):
    return lax.dot_general(a, b, (((0,), (0,)), ((), ())), preferred_element_type=F32)


def _mla_proj_kernel(h_ref, winT_ref, wuqT_ref, wkT_ref, wv_ref, vone_ref, gq_ref, gk_ref,
                     cos_ref, sin_ref, q_ref, kT_ref, v_ref, stats_ref):
    tt = h_ref.shape[0]
    h = h_ref[...]
    hn = (h * lax.rsqrt(jnp.mean(h * h, axis=-1, keepdims=True) + EPS)).astype(BF16)
    zT = _nt_dot(winT_ref[...], hn)
    cq = zT[0:MLA_Q_RANK]
    ckv = zT[MLA_Q_RANK:MLA_Q_RANK + MLA_KV_RANK]
    kr = zT[MLA_Q_RANK + MLA_KV_RANK:]
    cqn = (cq * lax.rsqrt(jnp.mean(cq * cq, axis=0, keepdims=True) + EPS)).astype(BF16)
    ckvn = ckv * lax.rsqrt(jnp.mean(ckv * ckv, axis=0, keepdims=True) + EPS)
    qT = jnp.dot(wuqT_ref[...], cqn, preferred_element_type=F32)
    kT = jnp.dot(wkT_ref[...], ckvn.astype(BF16), preferred_element_type=F32)
    v = jnp.dot(ckvn.T.astype(BF16), wv_ref[...], preferred_element_type=F32) + vone_ref[...]
    cos = cos_ref[...]
    sin = sin_ref[...]
    gq = gq_ref[...]
    gk = gk_ref[...]
    half = MLA_ROPE // 2
    zpad = jnp.zeros((LANES - MLA_QK - 1, tt), F32)
    kpad = jnp.concatenate([jnp.ones((1, tt), F32), zpad], axis=0)
    spad = jnp.zeros((6, tt), F32)
    kr_ss = jnp.sum(kr * kr, axis=0, keepdims=True)
    for hh in range(MLA_HEADS):
        qh = qT[MLA_QK * hh:MLA_QK * (hh + 1)]
        qn = qh * lax.rsqrt(jnp.sum(qh * qh, axis=0, keepdims=True) * (1.0 / MLA_QK) + EPS) * gq
        x1 = qn[MLA_NOPE:MLA_NOPE + half]
        x2 = qn[MLA_NOPE + half:]
        qmain = jnp.concatenate([qn[:MLA_NOPE], x1 * cos - x2 * sin, x2 * cos + x1 * sin], axis=0)
        qsq = jnp.sum(qmain * qmain, axis=0, keepdims=True)
        q_ref[hh] = jnp.concatenate([qmain, -jnp.sqrt(qsq), zpad], axis=0).T.astype(BF16)
        kn = kT[MLA_NOPE * hh:MLA_NOPE * (hh + 1)]
        rk = lax.rsqrt((jnp.sum(kn * kn, axis=0, keepdims=True) + kr_ss) * (1.0 / MLA_QK) + EPS)
        knn = kn * rk * gk[:MLA_NOPE]
        krn = kr * rk * gk[MLA_NOPE:]
        y1 = krn[:half]
        y2 = krn[half:]
        kmain = jnp.concatenate([knn, y1 * cos - y2 * sin, y2 * cos + y1 * sin], axis=0)
        stats_ref[hh] = jnp.concatenate([jnp.sum(kmain * kmain, axis=0, keepdims=True), qsq, spad], axis=0)
        kfull = jnp.concatenate([kmain, kpad], axis=0).astype(BF16)
        for u in range(tt // LANES):
            kT_ref[hh, u] = kfull[:, LANES * u:LANES * (u + 1)]
        v_ref[hh] = v[:, LANES * hh:LANES * (hh + 1)].astype(BF16)


def _mla_proj(h, winT, wuqT, wkT, wv, vone, gq, gk, cos, sin):
    T = h.shape[0]
    tt = TOK_TILE
    H = MLA_HEADS
    const = lambda shape: pl.BlockSpec(shape, lambda i: (0,) * len(shape))
    return pl.pallas_call(
        _mla_proj_kernel,
        grid=(T // tt,),
        in_specs=[
            pl.BlockSpec((tt, D_MODEL), lambda i: (i, 0)),
            const(winT.shape), const(wuqT.shape), const(wkT.shape), const(wv.shape), const(vone.shape),
            const(gq.shape), const(gk.shape),
            pl.BlockSpec((MLA_ROPE // 2, tt), lambda i: (0, i)),
            pl.BlockSpec((MLA_ROPE // 2, tt), lambda i: (0, i)),
        ],
        out_specs=[
            pl.BlockSpec((H, tt, LANES), lambda i: (0, i, 0)),
            pl.BlockSpec((H, tt // LANES, LANES, LANES), lambda i: (0, i, 0, 0)),
            pl.BlockSpec((H, tt, LANES), lambda i: (0, i, 0)),
            pl.BlockSpec((H, 8, tt), lambda i: (0, 0, i)),
        ],
        out_shape=[
            jax.ShapeDtypeStruct((H, T, LANES), BF16),
            jax.ShapeDtypeStruct((H, T // LANES, LANES, LANES), BF16),
            jax.ShapeDtypeStruct((H, T, LANES), BF16),
            jax.ShapeDtypeStruct((H, 8, T), F32),
        ],
        compiler_params=_cparams(("arbitrary",)),
        name="mla_proj",
    )(h, winT, wuqT, wkT, wv, vone, gq, gk, cos, sin)


def _attn_kernel(q_ref, kT_ref, v_ref, stats_ref, o_ref, *, nf):
    tq = ATT_TILE
    nsub = tq // LANES
    heads = range(2)
    row = lax.broadcasted_iota(jnp.int32, (tq, tq), 0)
    col = lax.broadcasted_iota(jnp.int32, (tq, tq), 1)
    diag_bias = jnp.where((col // CHUNK) <= (row // CHUNK), 0.0, NEG).astype(F32)
    col0 = lax.broadcasted_iota(jnp.int32, (1, LANES), 1)
    bias0 = jnp.where(col0 >= PAD_ROWS, 0.0, NEG).astype(F32)

    def ktile(hh, blk):
        return jnp.concatenate([kT_ref[hh, blk + u] for u in range(nsub)], axis=1)

    def vtile(hh, t):
        return v_ref[hh, pl.ds(pl.multiple_of(META_BLOCK + t * tq, LANES), tq), :]

    def finish(acc):
        return (acc[:, :MLA_V] / acc[:, MLA_V:MLA_V + 1]).astype(BF16)

    def put(r0, n, hh, acc):
        o_ref[pl.ds(r0, n), MLA_V * hh:MLA_V * (hh + 1)] = finish(acc)

    sq_max = [jnp.max(stats_ref[hh], axis=1, keepdims=True) for hh in heads]
    worst = jnp.sqrt(jnp.maximum(sq_max[0][0:1] * sq_max[0][1:2], sq_max[1][0:1] * sq_max[1][1:2]))

    def shifted():
        lane = lax.broadcasted_iota(jnp.int32, (1, LANES), 1)
        col_scale = [jnp.where(lane == MLA_QK, jnp.sqrt(sq_max[hh][0:1]), 1.0).astype(BF16) for hh in heads]

        def with_shift(q, hh):
            return q * col_scale[hh]

        def part(q, kt, vv, bias):
            s = jnp.dot(q, kt, preferred_element_type=F32)
            if bias is not None:
                s = s + bias
            return jnp.dot(jnp.exp2(s).astype(BF16), vv, preferred_element_type=F32)

        for hh in heads:
            q0 = with_shift(q_ref[hh, 0:META_BLOCK, :], hh)
            put(0, META_BLOCK, hh, part(q0, kT_ref[hh, 0], v_ref[hh, 0:META_BLOCK, :], bias0))

        chains = [(hh, r) for hh in heads for r in range(2)]

        def qpair(ii, carry):
            r0 = pl.multiple_of(META_BLOCK + 2 * ii * tq, LANES)
            qs = [with_shift(q_ref[hh, pl.ds(r0 + r * tq, tq), :], hh) for hh, r in chains]

            def kv(j, acc):
                c0 = pl.multiple_of(META_BLOCK + 2 * j * tq, LANES)
                kts = [jnp.concatenate([kT_ref[hh, 1 + 2 * j * nsub + u] for u in range(2 * nsub)], axis=1)
                       for hh in heads]
                vts = [v_ref[hh, pl.ds(c0, 2 * tq), :] for hh in heads]
                return tuple(acc[c] + part(qs[c], kts[hh], vts[hh], None) for c, (hh, r) in enumerate(chains))

            accs = lax.fori_loop(0, ii, kv, tuple(jnp.zeros((tq, LANES), F32) for _ in chains))
            for c, (hh, r) in enumerate(chains):
                acc = accs[c] + part(qs[c], kT_ref[hh, 0], v_ref[hh, 0:META_BLOCK, :], bias0)
                if r == 1:
                    acc = acc + part(qs[c], ktile(hh, 1 + 2 * ii * nsub), vtile(hh, 2 * ii), None)
                acc = acc + part(qs[c], ktile(hh, 1 + (2 * ii + r) * nsub), vtile(hh, 2 * ii + r), diag_bias)
                put(r0 + r * tq, tq, hh, acc)
            return carry

        lax.fori_loop(0, nf // 2, qpair, 0)

    def online():
        def step(q, kt, vv, m, acc, bias):
            s = jnp.dot(q, kt, preferred_element_type=F32)
            if bias is not None:
                s = s + bias
            m_new = jnp.maximum(m, jnp.max(s, axis=-1, keepdims=True))
            p = jnp.exp2(s - m_new)
            acc = jnp.exp2(m - m_new) * acc + jnp.dot(p.astype(BF16), vv, preferred_element_type=F32)
            return m_new, acc

        def first(q, hh, n):
            return step(q, kT_ref[hh, 0], v_ref[hh, 0:META_BLOCK, :], jnp.full((n, 1), NEG, F32),
                        jnp.zeros((n, LANES), F32), bias0)

        for hh in heads:
            put(0, META_BLOCK, hh, first(q_ref[hh, 0:META_BLOCK, :], hh, META_BLOCK)[1])

        def qtile(i, carry):
            r0 = pl.multiple_of(META_BLOCK + i * tq, LANES)
            qs = [q_ref[hh, pl.ds(r0, tq), :] for hh in heads]
            state = []
            for hh in heads:
                state += list(first(qs[hh], hh, tq))

            def kv(j, c):
                out = []
                for hh in heads:
                    out += list(step(qs[hh], ktile(hh, 1 + j * nsub), vtile(hh, j), c[2 * hh], c[2 * hh + 1], None))
                return tuple(out)

            state = lax.fori_loop(0, i, kv, tuple(state))
            for hh in heads:
                m, acc = step(qs[hh], ktile(hh, 1 + i * nsub), vtile(hh, i), state[2 * hh], state[2 * hh + 1],
                              diag_bias)
                put(r0, tq, hh, acc)
            return carry

        lax.fori_loop(0, nf, qtile, 0)

    lax.cond(worst[0, 0] <= ATT_SAFE_BOUND, shifted, online)


def _attention(q, kT, v, stats, B, LK):
    H, T, _ = q.shape
    nf = (LK - META_BLOCK) // ATT_TILE
    nblk = LK // LANES
    return pl.pallas_call(
        functools.partial(_attn_kernel, nf=nf),
        grid=(B, H // 2),
        in_specs=[
            pl.BlockSpec((2, LK, LANES), lambda b, hp: (hp, b, 0)),
            pl.BlockSpec((2, nblk, LANES, LANES), lambda b, hp: (hp, b, 0, 0)),
            pl.BlockSpec((2, LK, LANES), lambda b, hp: (hp, b, 0)),
            pl.BlockSpec((2, 8, LK), lambda b, hp: (hp, 0, b)),
        ],
        out_specs=pl.BlockSpec((LK, 2 * MLA_V), lambda b, hp: (b, hp)),
        out_shape=jax.ShapeDtypeStruct((T, H * MLA_V), BF16),
        compiler_params=_cparams(("arbitrary", "arbitrary")),
        name="mla_attention",
    )(q, kT, v, stats)


def _route(hnew, gffn_ref, wrh_ref, wrl_ref, br_ref, tri_ref, hn_ref, info_ref, cnt_ref, carry_ref):
    tt = hnew.shape[0]
    hn = hnew * lax.rsqrt(jnp.mean(hnew * hnew, axis=-1, keepdims=True) + EPS) * gffn_ref[...]
    hn_hi = hn.astype(BF16)
    hn_lo = (hn - hn_hi.astype(F32)).astype(BF16)
    wrh = wrh_ref[...]
    logits = (jnp.dot(hn_hi, wrh, preferred_element_type=F32)
              + jnp.dot(hn_lo, wrh, preferred_element_type=F32)
              + jnp.dot(hn_hi, wrl_ref[...], preferred_element_type=F32)) + br_ref[...]

    lane = lax.broadcasted_iota(jnp.int32, (tt, LANES), 1)
    lane_f = lane.astype(F32)
    big = float(LANES)
    gmask = (lane >= N_EXPERTS) & (lane < N_EXPERTS + N_GROUPS)
    gl = jnp.where(gmask, logits, NEG)
    gmax = jnp.max(gl, axis=-1, keepdims=True)
    gsel = jnp.min(jnp.where(gl == gmax, lane_f, big), axis=-1, keepdims=True) - float(N_EXPERTS)
    pg = 1.0 / jnp.sum(jnp.exp(gl - gmax), axis=-1, keepdims=True)
    egrp = (lane // EXPERTS_PER_GROUP).astype(F32)
    emask = (lane < N_EXPERTS) & (egrp == gsel)
    el = jnp.where(emask, logits, NEG)
    m1 = jnp.max(el, axis=-1, keepdims=True)
    i1 = jnp.min(jnp.where(el == m1, lane_f, big), axis=-1, keepdims=True)
    el2 = jnp.where(lane_f == i1, NEG, el)
    m2 = jnp.max(el2, axis=-1, keepdims=True)
    i2 = jnp.min(jnp.where(el2 == m2, lane_f, big), axis=-1, keepdims=True)
    t21 = jnp.exp(m2 - m1)
    w1 = pg / (1.0 + t21)
    w2 = w1 * t21

    lo = jnp.minimum(i1, i2)
    hi = jnp.maximum(i1, i2)
    first_is_lo = i1 < i2
    wa = jnp.where(first_is_lo, w1, w2)
    wb = jnp.where(first_is_lo, w2, w1)
    gbase = gsel * float(EXPERTS_PER_GROUP)
    cls = gsel * float(CLASS_STRIDE) + (lo - gbase) * float(EXPERTS_PER_GROUP) + (hi - gbase)
    lane2 = lax.broadcasted_iota(jnp.int32, (tt, N_CLASS_SLOTS), 1).astype(F32)
    sel = lane2 == cls
    oh = jnp.where(sel, 1.0, 0.0)
    before = jnp.dot(tri_ref[...], oh.astype(BF16), preferred_element_type=F32) + carry_ref[...]
    rank = jnp.sum(jnp.where(sel, before, 0.0), axis=-1, keepdims=True)
    carry_ref[...] = carry_ref[...] + jnp.sum(oh, axis=0, keepdims=True)
    info = jnp.where(lane == 0, cls, jnp.where(lane == 1, rank, jnp.where(lane == 2, wa, jnp.where(
        lane == 3, wb, 0.0))))
    hn_ref[:, :D_MODEL] = hn
    hn_ref[:, D_MODEL:] = info
    info_ref[...] = info
    cnt_ref[...] = jnp.broadcast_to(carry_ref[...], cnt_ref.shape)


def _mla_out_kernel(h_ref, a_ref, wo_ref, gffn_ref, wrh_ref, wrl_ref, br_ref, tri_ref,
                    hout_ref, hn_ref, info_ref, cnt_ref, carry_ref):
    @pl.when(pl.program_id(0) == 0)
    def _():
        carry_ref[...] = jnp.zeros_like(carry_ref)

    hnew = h_ref[...] + jnp.dot(a_ref[...], wo_ref[...], preferred_element_type=F32)
    hout_ref[...] = hnew
    _route(hnew, gffn_ref, wrh_ref, wrl_ref, br_ref, tri_ref, hn_ref, info_ref, cnt_ref, carry_ref)


def _gla_out_kernel(h_ref, a_ref, r_ref, gout_ref, wo_ref, gffn_ref, wrh_ref, wrl_ref, br_ref, tri_ref,
                    hout_ref, hn_ref, info_ref, cnt_ref, carry_ref):
    @pl.when(pl.program_id(0) == 0)
    def _():
        carry_ref[...] = jnp.zeros_like(carry_ref)

    o = a_ref[...].astype(F32)
    r = r_ref[...].astype(F32)
    gout = gout_ref[...]
    parts = []
    for hh in range(GLA_HEADS):
        oh = o[:, GLA_DV * hh:GLA_DV * (hh + 1)]
        parts.append(oh * lax.rsqrt(jnp.mean(oh * oh, axis=-1, keepdims=True) + EPS) * gout)
    a = (jnp.concatenate(parts, axis=1) * (r * jax.nn.sigmoid(r))).astype(BF16)
    hnew = h_ref[...] + jnp.dot(a, wo_ref[...], preferred_element_type=F32)
    hout_ref[...] = hnew
    _route(hnew, gffn_ref, wrh_ref, wrl_ref, br_ref, tri_ref, hn_ref, info_ref, cnt_ref, carry_ref)


def _mixer_out(h, a, wo, gffn, wrh, wrl, br, tri, gla_extra=None):
    T = h.shape[0]
    tt = TOK_TILE
    tile = lambda w: pl.BlockSpec((tt, w), lambda i: (i, 0))
    const = lambda shape: pl.BlockSpec(shape, lambda i: (0,) * len(shape))
    if gla_extra is None:
        kern = _mla_out_kernel
        ins = [h, a]
        in_specs = [tile(D_MODEL), tile(a.shape[1])]
    else:
        r, gout = gla_extra
        kern = _gla_out_kernel
        ins = [h, a, r, gout]
        in_specs = [tile(D_MODEL), tile(a.shape[1]), tile(r.shape[1]), const(gout.shape)]
    ins += [wo, gffn, wrh, wrl, br, tri]
    in_specs += [const(wo.shape), const(gffn.shape), const(wrh.shape), const(wrl.shape), const(br.shape),
                 const(tri.shape)]
    return pl.pallas_call(
        kern,
        grid=(T // tt,),
        in_specs=in_specs,
        out_specs=[tile(D_MODEL), tile(ROW_EXT), tile(LANES), pl.BlockSpec((8, N_CLASS_SLOTS), lambda i: (0, 0))],
        out_shape=[
            jax.ShapeDtypeStruct((T, D_MODEL), F32),
            jax.ShapeDtypeStruct((T, ROW_EXT), F32),
            jax.ShapeDtypeStruct((T, LANES), F32),
            jax.ShapeDtypeStruct((8, N_CLASS_SLOTS), F32),
        ],
        scratch_shapes=[pltpu.VMEM((1, N_CLASS_SLOTS), F32)],
        compiler_params=_cparams(("arbitrary",)),
        name="mixer_out_router",
    )(*ins)


def _group(d):
    return lax.shift_right_logical(d, SUBLANES.bit_length() - 1)


def _in_group(d):
    return d & (SUBLANES - 1)


def _dispatch_kernel(dest_ref, hn_ref, xs_in_ref, xs_ref, sem):
    del xs_in_ref
    ng = hn_ref.shape[0]

    def row_copy(g, j, d):
        return pltpu.make_async_copy(hn_ref.at[g, pl.ds(j, 1)],
                                     xs_ref.at[_group(d), pl.ds(_in_group(d), 1)], sem)

    def issue(g, c):
        for j in range(SUBLANES):
            row_copy(g, j, dest_ref[0, 0, g * SUBLANES + j]).start()
        return c

    lax.fori_loop(0, ng, issue, 0)

    def drain(r, c):
        row_copy(0, 0, 0).wait()
        return c

    lax.fori_loop(0, ng * SUBLANES, drain, 0, unroll=8)


def _dispatch(dest3, hn, xs0):
    T = hn.shape[0]
    tt = TOK_TILE
    P = xs0.shape[0]
    out = pl.pallas_call(
        _dispatch_kernel,
        grid=(T // tt,),
        in_specs=[
            pl.BlockSpec((1, 1, tt), lambda i: (i, 0, 0), memory_space=pltpu.SMEM),
            pl.BlockSpec((tt // SUBLANES, SUBLANES, ROW_EXT), lambda i: (i, 0, 0)),
            pl.BlockSpec(memory_space=pl.ANY),
        ],
        out_specs=pl.BlockSpec(memory_space=pl.ANY),
        out_shape=jax.ShapeDtypeStruct((P // SUBLANES, SUBLANES, ROW_EXT), xs0.dtype),
        scratch_shapes=[pltpu.SemaphoreType.DMA],
        input_output_aliases={2: 0},
        compiler_params=_cparams(("arbitrary",)),
        name="moe_dispatch",
    )(dest3, hn.reshape(T // SUBLANES, SUBLANES, ROW_EXT), xs0.reshape(P // SUBLANES, SUBLANES, ROW_EXT))
    return out.reshape(P, ROW_EXT)


def _expert_kernel(ea_ref, eb_ref, nv_ref, xs_ref, *refs):
    del ea_ref, eb_ref
    w_refs, y_ref = refs[:-1], refs[-1]
    tm = EXP_TILE
    valid = 2 * pl.program_id(0) < nv_ref[0]

    def hidden(gu, w):
        g = gu[:, :D_EXPERT]
        return (g * jax.nn.sigmoid(g) * gu[:, D_EXPERT:] * w).astype(BF16)

    @pl.when(valid)
    def _():
        for u in range(2):
            wgua_ref, wda_ref, wgub_ref, wdb_ref = w_refs[4 * u:4 * u + 4]
            rows = slice(u * tm, (u + 1) * tm)
            x = xs_ref[rows, :D_MODEL].astype(BF16)
            gua = jnp.dot(x, wgua_ref[0], preferred_element_type=F32)
            gub = jnp.dot(x, wgub_ref[0], preferred_element_type=F32)
            ha = hidden(gua, xs_ref[rows, D_MODEL + 2:D_MODEL + 3])
            hb = hidden(gub, xs_ref[rows, D_MODEL + 3:D_MODEL + 4])
            y_ref[rows, :] = (jnp.dot(ha, wda_ref[0], preferred_element_type=F32)
                              + jnp.dot(hb, wdb_ref[0], preferred_element_type=F32))

    @pl.when(jnp.logical_not(valid))
    def _():
        y_ref[...] = jnp.zeros_like(y_ref)


def _experts(tile_ea, tile_eb, nvalid, xs, wgu, wd):
    P = xs.shape[0]
    tm = EXP_TILE
    assert (P // tm) % 2 == 0
    w_specs = []
    for u in range(2):
        for table in (0, 1):
            idx = lambda i, ea, eb, nv, u=u, table=table: ((ea, eb)[table][2 * i + u], 0, 0)
            w_specs += [pl.BlockSpec((1, D_MODEL, 2 * D_EXPERT), idx), pl.BlockSpec((1, D_EXPERT, D_MODEL), idx)]
    return pl.pallas_call(
        _expert_kernel,
        grid_spec=pltpu.PrefetchScalarGridSpec(
            num_scalar_prefetch=3,
            grid=(P // (2 * tm),),
            in_specs=[pl.BlockSpec((2 * tm, ROW_EXT),
                                   lambda i, ea, eb, nv: (jnp.minimum(i, (nv[0] - 1) // 2), 0))] + w_specs,
            out_specs=pl.BlockSpec((2 * tm, D_MODEL), lambda i, ea, eb, nv: (i, 0)),
        ),
        out_shape=jax.ShapeDtypeStruct((P, D_MODEL), F32),
        compiler_params=_cparams(("arbitrary",)),
        name="moe_experts",
    )(tile_ea, tile_eb, nvalid, xs, *([wgu, wd] * 4))


def _combine_kernel(dest_ref, h_ref, y_ref, out_ref, ybuf, sem):
    ng = ybuf.shape[0]

    def row_copy(g, j, d):
        return pltpu.make_async_copy(y_ref.at[_group(d), pl.ds(_in_group(d), 1)],
                                     ybuf.at[g, pl.ds(j, 1)], sem)

    def issue(g, c):
        for j in range(SUBLANES):
            row_copy(g, j, dest_ref[0, 0, g * SUBLANES + j]).start()
        return c

    lax.fori_loop(0, ng, issue, 0)

    def drain(r, c):
        row_copy(0, 0, 0).wait()
        return c

    lax.fori_loop(0, ng * SUBLANES, drain, 0, unroll=8)
    out_ref[...] = h_ref[...] + ybuf[...].reshape(out_ref.shape)


def _combine(dest3, h, y):
    T = h.shape[0]
    tt = TOK_TILE
    P = y.shape[0]
    return pl.pallas_call(
        _combine_kernel,
        grid=(T // tt,),
        in_specs=[
            pl.BlockSpec((1, 1, tt), lambda i: (i, 0, 0), memory_space=pltpu.SMEM),
            pl.BlockSpec((tt, D_MODEL), lambda i: (i, 0)),
            pl.BlockSpec(memory_space=pl.ANY),
        ],
        out_specs=pl.BlockSpec((tt, D_MODEL), lambda i: (i, 0)),
        out_shape=jax.ShapeDtypeStruct((T, D_MODEL), F32),
        scratch_shapes=[pltpu.VMEM((tt // SUBLANES, SUBLANES, D_MODEL), F32), pltpu.SemaphoreType.DMA],
        compiler_params=_cparams(("arbitrary",)),
        name="moe_combine",
    )(dest3, h, y.reshape(P // SUBLANES, SUBLANES, D_MODEL))


def _moe(h, hn_ext, info, cnt, wgu, wd, xs_init):
    T = h.shape[0]
    tm = EXP_TILE
    ntiles = xs_init.shape[0] // tm
    counts = cnt[0].astype(jnp.int32)
    tiles_c = (counts + tm - 1) // tm
    cum = jnp.cumsum(tiles_c)
    start_row = ((cum - tiles_c) * tm).astype(jnp.int32)
    tile_class = jnp.minimum(
        jnp.sum(jnp.arange(ntiles, dtype=jnp.int32)[:, None] >= cum[None, :], axis=1), N_CLASS_SLOTS - 1
    ).astype(jnp.int32)
    group_base = (tile_class // CLASS_STRIDE) * EXPERTS_PER_GROUP
    tile_ea = group_base + (tile_class % CLASS_STRIDE) // EXPERTS_PER_GROUP
    tile_eb = group_base + tile_class % EXPERTS_PER_GROUP
    nvalid = cum[-1:].astype(jnp.int32)
    cr = info[:, 0:2].astype(jnp.int32)
    dest3 = (jnp.take(start_row, cr[:, 0]) + cr[:, 1]).reshape(T // TOK_TILE, 1, TOK_TILE)
    xs = _dispatch(dest3, hn_ext, xs_init)
    y = _experts(tile_ea, tile_eb, nvalid, xs, wgu, wd)
    return _combine(dest3, h, y), xs


def _gla_in_kernel(h_ref, w_ref, wgu_ref, bg_ref, q_ref, k_ref, g_ref, v_ref, r_ref):
    h = h_ref[...]
    hn = (h * lax.rsqrt(jnp.mean(h * h, axis=-1, keepdims=True) + EPS)).astype(BF16)
    z = jnp.dot(hn, w_ref[...], preferred_element_type=F32)
    nk = GLA_HEADS * GLA_DK
    nv = GLA_HEADS * GLA_DV
    o_v = 2 * nk
    o_g = o_v + nv
    o_r = o_g + LANES
    xg = jnp.dot(z[:, o_g:o_r].astype(BF16), wgu_ref[...], preferred_element_type=F32) + bg_ref[...]
    log_a = (jnp.minimum(xg, 0.0) - jnp.log(1.0 + jnp.exp(-jnp.abs(xg)))) * (1.0 / GLA_TAU)
    for hh in range(GLA_HEADS):
        q_ref[hh] = z[:, GLA_DK * hh:GLA_DK * (hh + 1)].astype(BF16)
        k_ref[hh] = z[:, nk + GLA_DK * hh:nk + GLA_DK * (hh + 1)].astype(BF16)
        g_ref[hh] = log_a[:, GLA_DK * hh:GLA_DK * (hh + 1)]
        v_ref[hh] = z[:, o_v + GLA_DV * hh:o_v + GLA_DV * (hh + 1)].astype(BF16)
    r_ref[...] = z[:, o_r:].astype(BF16)


def _gla_in(h, w, wgu, bg):
    T = h.shape[0]
    tt = TOK_TILE
    GH = GLA_HEADS
    const = lambda shape: pl.BlockSpec(shape, lambda i: (0,) * len(shape))
    hspec = lambda w_: pl.BlockSpec((GH, tt, w_), lambda i: (0, i, 0))
    return pl.pallas_call(
        _gla_in_kernel,
        grid=(T // tt,),
        in_specs=[pl.BlockSpec((tt, D_MODEL), lambda i: (i, 0)), const(w.shape), const(wgu.shape), const(bg.shape)],
        out_specs=[hspec(GLA_DK), hspec(GLA_DK), hspec(GLA_DK), hspec(GLA_DV),
                   pl.BlockSpec((tt, GH * GLA_DV), lambda i: (i, 0))],
        out_shape=[
            jax.ShapeDtypeStruct((GH, T, GLA_DK), BF16),
            jax.ShapeDtypeStruct((GH, T, GLA_DK), BF16),
            jax.ShapeDtypeStruct((GH, T, GLA_DK), F32),
            jax.ShapeDtypeStruct((GH, T, GLA_DV), BF16),
            jax.ShapeDtypeStruct((T, GH * GLA_DV), BF16),
        ],
        compiler_params=_cparams(("arbitrary",)),
        name="gla_in",
    )(h, w, wgu, bg)


def _bdot(a, b, contract_b):
    return lax.dot_general(a, b, (((2,), (contract_b,)), ((0,), (0,))), preferred_element_type=F32)


def _gla_scan_kernel(q_ref, k_ref, g_ref, v_ref, stack_ref, o_ref, st_ref, oin_ref, qe_ref, kd_ref, dec_ref, *,
                     tile):
    C = SCAN_CHUNK
    nc = tile // C
    t = pl.program_id(1)

    @pl.when(t == 0)
    def _():
        st_ref[...] = jnp.zeros_like(st_ref)

    row = lax.broadcasted_iota(jnp.int32, (C, C), 0)
    col = lax.broadcasted_iota(jnp.int32, (C, C), 1)
    diag = (row == col)[None]
    masks = [(((row // (2 * c)) == (col // (2 * c))) & ((row % (2 * c)) >= c) & ((col % (2 * c)) < c))[None]
             for c in GLA_LEVELS]
    real = (t * tile + lax.broadcasted_iota(jnp.int32, (tile, 1), 0) >= PAD_ROWS).reshape(nc, C, 1)
    stack = stack_ref[...]

    for hh in range(GLA_HEADS):
        q = q_ref[hh].astype(F32).reshape(nc, C, GLA_DK)
        k = jnp.where(real, k_ref[hh].astype(F32).reshape(nc, C, GLA_DK), 0.0)
        g = g_ref[hh].reshape(nc, C, GLA_DK)
        v = v_ref[hh].reshape(nc, C, GLA_DV)
        g_hi = g.astype(BF16)
        g_lo = (g - g_hi.astype(F32)).astype(BF16)
        sums2 = _bdot(stack, jnp.concatenate([g_hi, g_lo], axis=2), 1)
        sums = sums2[:, :, :GLA_DK] + sums2[:, :, GLA_DK:]
        b = sums[:, 0:C]
        b_last = b[:, C - 1:C]
        att = jnp.where(diag, _bdot(q.astype(BF16), k.astype(BF16), 2), 0.0)
        for li in range(len(GLA_LEVELS)):
            ref = sums[:, C * (li + 1):C * (li + 2)]
            qf = (q * jnp.exp(jnp.minimum(b - ref, 0.0))).astype(BF16)
            kb = (k * jnp.exp(jnp.minimum(ref - b, 0.0))).astype(BF16)
            att = att + jnp.where(masks[li], _bdot(qf, kb, 2), 0.0)
        oin_ref[hh] = _bdot(att.astype(BF16), v, 1).reshape(tile, GLA_DV)
        qe_ref[hh] = (q * jnp.exp(b)).astype(BF16).reshape(tile, GLA_DK)
        kd_ref[hh] = (k * jnp.exp(b_last - b)).astype(BF16).reshape(tile, GLA_DK)
        dec_ref[hh] = jnp.broadcast_to(jnp.exp(b_last), (nc, 8, GLA_DK))

    def chunk(ci, carry):
        r0 = pl.multiple_of(ci * C, C)
        for hh in range(GLA_HEADS):
            st = st_ref[hh]
            o = oin_ref[hh, pl.ds(r0, C), :] + _nt_dot(qe_ref[hh, pl.ds(r0, C), :], st.astype(BF16))
            o_ref[pl.ds(r0, C), GLA_DV * hh:GLA_DV * (hh + 1)] = o.astype(BF16)
            st_ref[hh] = st * dec_ref[hh, ci][0:1] + _tn_dot(v_ref[hh, pl.ds(r0, C), :], kd_ref[hh, pl.ds(r0, C), :])
        return carry

    lax.fori_loop(0, nc, chunk, 0)


def _gla_scan(q, k, g, v, stack, B, LK):
    GH, T, _ = q.shape
    tile = 640 if LK % 640 == 0 else LANES
    nt = LK // tile
    nc = tile // SCAN_CHUNK
    stack3 = jnp.broadcast_to(stack[None], (nc,) + stack.shape)
    hspec = lambda w_: pl.BlockSpec((GH, tile, w_), lambda b, t: (0, b * nt + t, 0))
    return pl.pallas_call(
        functools.partial(_gla_scan_kernel, tile=tile),
        grid=(B, nt),
        in_specs=[hspec(GLA_DK), hspec(GLA_DK), hspec(GLA_DK), hspec(GLA_DV),
                  pl.BlockSpec(stack3.shape, lambda b, t: (0, 0, 0))],
        out_specs=pl.BlockSpec((tile, GH * GLA_DV), lambda b, t: (b * nt + t, 0)),
        out_shape=jax.ShapeDtypeStruct((T, GH * GLA_DV), BF16),
        scratch_shapes=[
            pltpu.VMEM((GH, GLA_DV, GLA_DK), F32),
            pltpu.VMEM((GH, tile, GLA_DV), F32),
            pltpu.VMEM((GH, tile, GLA_DK), BF16),
            pltpu.VMEM((GH, tile, GLA_DK), BF16),
            pltpu.VMEM((GH, nc, 8, GLA_DK), F32),
        ],
        compiler_params=_cparams(("arbitrary", "arbitrary")),
        name="gla_scan",
    )(q, k, g, v, stack3)


def _gla_stack():
    C = SCAN_CHUNK
    t = jnp.arange(C)[:, None]
    u = jnp.arange(C)[None, :]
    mats = [u <= t]
    for c in GLA_LEVELS:
        boundary = (t // (2 * c)) * (2 * c) + c - 1
        mats.append(u <= boundary)
    return jnp.concatenate(mats, axis=0).astype(BF16)


def _router_weights(w_rg, b_rg, w_re, b_re):
    w = jnp.zeros((D_MODEL, LANES), F32).at[:, :N_EXPERTS].set(w_re).at[:, N_EXPERTS:N_EXPERTS + N_GROUPS].set(w_rg)
    b = jnp.zeros((1, LANES), F32).at[0, :N_EXPERTS].set(b_re).at[0, N_EXPERTS:N_EXPERTS + N_GROUPS].set(b_rg)
    w_hi = w.astype(BF16)
    w_lo = (w - w_hi.astype(F32)).astype(BF16)
    return w_hi, w_lo, b


def kernel(x, meta_tokens, norm_mix, norm_ffn, mla_w_in, mla_g_q, mla_w_uq, mla_g_kv, mla_w_ukv, mla_g_qn, mla_g_kn, mla_w_o, gla_w_in, gla_w_gate_up, gla_b_gate, gla_g_out, gla_w_o, moe_w_rg, moe_b_rg, moe_w_re, moe_b_re, moe_w_gate, moe_w_up, moe_w_down):
    B, S, D = x.shape
    assert D == D_MODEL and S % (2 * ATT_TILE) == 0
    LK = META_BLOCK + S
    T = B * LK
    assert T % TOK_TILE == 0 and TOK_TILE % LANES == 0
    tt = TOK_TILE
    H = MLA_HEADS

    meta = jnp.broadcast_to(meta_tokens.astype(F32)[None], (B, N_META, D))
    h = jnp.zeros((B, LK, D), F32).at[:, PAD_ROWS:META_BLOCK].set(meta).at[:, META_BLOCK:].set(x.astype(F32))
    h = h.reshape(T, D)

    rows = jnp.arange(LK)
    pos = jnp.where(rows < META_BLOCK, jnp.maximum(rows - PAD_ROWS, 0), rows - META_BLOCK + N_META)
    half = MLA_ROPE // 2
    inv = 1.0 / (ROPE_THETA ** (jnp.arange(half, dtype=F32) / half))
    ang = inv[:, None] * pos.astype(F32)[None, :]
    cos = jnp.tile(jnp.cos(ang), (1, B))
    sin = jnp.tile(jnp.sin(ang), (1, B))

    tri = (jnp.arange(tt)[None, :] < jnp.arange(tt)[:, None]).astype(BF16)

    winT = (mla_w_in[0] * norm_mix[0][:, None]).T.astype(BF16)
    wuqT = (mla_w_uq[0] * mla_g_q[0][:, None]).T.astype(BF16)
    wukv = (mla_w_ukv[0] * mla_g_kv[0][:, None]).reshape(MLA_KV_RANK, H, MLA_NOPE + MLA_V)
    wkT = wukv[:, :, :MLA_NOPE].reshape(MLA_KV_RANK, H * MLA_NOPE).T.astype(BF16)
    wv = jnp.zeros((MLA_KV_RANK, H, LANES), F32).at[:, :, :MLA_V].set(wukv[:, :, MLA_NOPE:])
    wv = wv.reshape(MLA_KV_RANK, H * LANES).astype(BF16)
    vone = jnp.zeros((1, H, LANES), F32).at[:, :, MLA_V].set(1.0).reshape(1, H * LANES)
    gq = jnp.broadcast_to((mla_g_qn[0] * (MLA_QK ** -0.5 * LOG2E))[:, None], (MLA_QK, tt)).astype(F32)
    gk = jnp.broadcast_to(mla_g_kn[0][:, None], (MLA_QK, tt)).astype(F32)
    q, kT, v, stats = _mla_proj(h, winT, wuqT, wkT, wv, vone, gq, gk, cos, sin)
    att = _attention(q, kT, v, stats, B, LK)

    wrh, wrl, br = _router_weights(moe_w_rg[0], moe_b_rg[0], moe_w_re[0], moe_b_re[0])
    h, hn, info, cnt = _mixer_out(h, att, mla_w_o[0].astype(BF16), norm_ffn[0][None, :], wrh, wrl, br, tri)
    wgu = jnp.concatenate([moe_w_gate[0], moe_w_up[0]], axis=-1).astype(BF16)
    xs0 = jnp.zeros(((T // EXP_TILE + N_CLASSES) * EXP_TILE, ROW_EXT), F32)
    h, xs1 = _moe(h, hn, info, cnt, wgu, moe_w_down[0].astype(BF16), xs0)

    nk = GLA_HEADS * GLA_DK
    nv = GLA_HEADS * GLA_DV
    w1 = gla_w_in[0] * norm_mix[1][:, None]
    wg_pad = jnp.zeros((D, LANES), F32).at[:, :GLA_GATE_RANK].set(w1[:, 2 * nk + nv:2 * nk + nv + GLA_GATE_RANK])
    w_all = jnp.concatenate([w1[:, :nk] * (GLA_DK ** -0.5), w1[:, nk:2 * nk + nv], wg_pad,
                             w1[:, 2 * nk + nv + GLA_GATE_RANK:]], axis=1).astype(BF16)
    wgate = jnp.zeros((LANES, nk), F32).at[:GLA_GATE_RANK].set(gla_w_gate_up[0]).astype(BF16)
    gq_, gk_, gg_, gv_, gr_ = _gla_in(h, w_all, wgate, gla_b_gate[0][None, :])
    go = _gla_scan(gq_, gk_, gg_, gv_, _gla_stack(), B, LK)

    wrh, wrl, br = _router_weights(moe_w_rg[1], moe_b_rg[1], moe_w_re[1], moe_b_re[1])
    h, hn, info, cnt = _mixer_out(h, go, gla_w_o[0].astype(BF16), norm_ffn[1][None, :], wrh, wrl, br, tri,
                                  gla_extra=(gr_, gla_g_out[0][None, :]))
    wgu = jnp.concatenate([moe_w_gate[1], moe_w_up[1]], axis=-1).astype(BF16)
    h, _ = _moe(h, hn, info, cnt, wgu, moe_w_down[1].astype(BF16), xs1)

    return h.reshape(B, LK, D)[:, META_BLOCK:].astype(x.dtype)
```

```python
import functools

import jax
import jax.numpy as jnp
from jax import lax
from jax.experimental import pallas as pl
from jax.experimental.pallas import tpu as pltpu

F32 = jnp.float32
BF16 = jnp.bfloat16

D_MODEL = 1024
CHUNK = 64
N_META = 16
MLA_HEADS = 16
MLA_Q_RANK = 384
MLA_KV_RANK = 256
MLA_NOPE = 64
MLA_ROPE = 32
MLA_V = 64
MLA_QK = MLA_NOPE + MLA_ROPE
ROPE_THETA = 10000.0
GLA_HEADS = 4
GLA_DK = 128
GLA_DV = 256
GLA_GATE_RANK = 16
GLA_TAU = 16.0
N_GROUPS = 4
EXPERTS_PER_GROUP = 8
N_EXPERTS = N_GROUPS * EXPERTS_PER_GROUP
D_EXPERT = 256
EPS = 1e-6

LANES = 128
SUBLANES = 8
META_BLOCK = 128
PAD_ROWS = META_BLOCK - N_META
NEG = -1e30
LOG2E = 1.4426950408889634

TOK_TILE = 512
ATT_TILE = 512
ATT_SAFE_BOUND = 50.0
SCAN_CHUNK = 64
EXP_TILE = 128
CLASS_STRIDE = EXPERTS_PER_GROUP * EXPERTS_PER_GROUP
N_CLASS_SLOTS = N_GROUPS * CLASS_STRIDE
N_CLASSES = N_GROUPS * (EXPERTS_PER_GROUP * (EXPERTS_PER_GROUP - 1) // 2)
ROW_EXT = D_MODEL + LANES
VMEM_LIMIT = 56 * 1024 * 1024

GLA_LEVELS = (32, 16, 8, 4, 2, 1)


def _cparams(sem):
    return pltpu.CompilerParams(dimension_semantics=sem, vmem_limit_bytes=VMEM_LIMIT)


def _nt_dot(a, b):
    return lax.dot_general(a, b, (((1,), (1,)), ((), ())), preferred_element_type=F32)


def _tn_dot(a, b):
    return lax.dot_general(a, b, (((0,), (0,)), ((), ())), preferred_element_type=F32)


def _mla_proj_kernel(h_ref, winT_ref, wuqT_ref, wkT_ref, wv_ref, vone_ref, gq_ref, gk_ref,
                     cos_ref, sin_ref, q_ref, kT_ref, v_ref, stats_ref):
    tt = h_ref.shape[0]
    h = h_ref[...]
    hn = (h * lax.rsqrt(jnp.mean(h * h, axis=-1, keepdims=True) + EPS)).astype(BF16)
    zT = _nt_dot(winT_ref[...], hn)
    cq = zT[0:MLA_Q_RANK]
    ckv = zT[MLA_Q_RANK:MLA_Q_RANK + MLA_KV_RANK]
    kr = zT[MLA_Q_RANK + MLA_KV_RANK:]
    cqn = (cq * lax.rsqrt(jnp.mean(cq * cq, axis=0, keepdims=True) + EPS)).astype(BF16)
    ckvn = ckv * lax.rsqrt(jnp.mean(ckv * ckv, axis=0, keepdims=True) + EPS)
    qT = jnp.dot(wuqT_ref[...], cqn, preferred_element_type=F32)
    kT = jnp.dot(wkT_ref[...], ckvn.astype(BF16), preferred_element_type=F32)
    v = jnp.dot(ckvn.T.astype(BF16), wv_ref[...], preferred_element_type=F32) + vone_ref[...]
    cos = cos_ref[...]
    sin = sin_ref[...]
    gq = gq_ref[...]
    gk = gk_ref[...]
    half = MLA_ROPE // 2
    zpad = jnp.zeros((LANES - MLA_QK - 1, tt), F32)
    kpad = jnp.concatenate([jnp.ones((1, tt), F32), zpad], axis=0)
    spad = jnp.zeros((6, tt), F32)
    kr_ss = jnp.sum(kr * kr, axis=0, keepdims=True)
    for hh in range(MLA_HEADS):
        qh = qT[MLA_QK * hh:MLA_QK * (hh + 1)]
        qn = qh * lax.rsqrt(jnp.sum(qh * qh, axis=0, keepdims=True) * (1.0 / MLA_QK) + EPS) * gq
        x1 = qn[MLA_NOPE:MLA_NOPE + half]
        x2 = qn[MLA_NOPE + half:]
        qmain = jnp.concatenate([qn[:MLA_NOPE], x1 * cos - x2 * sin, x2 * cos + x1 * sin], axis=0)
        qsq = jnp.sum(qmain * qmain, axis=0, keepdims=True)
        q_ref[hh] = jnp.concatenate([qmain, -jnp.sqrt(qsq), zpad], axis=0).T.astype(BF16)
        kn = kT[MLA_NOPE * hh:MLA_NOPE * (hh + 1)]
        rk = lax.rsqrt((jnp.sum(kn * kn, axis=0, keepdims=True) + kr_ss) * (1.0 / MLA_QK) + EPS)
        knn = kn * rk * gk[:MLA_NOPE]
        krn = kr * rk * gk[MLA_NOPE:]
        y1 = krn[:half]
        y2 = krn[half:]
        kmain = jnp.concatenate([knn, y1 * cos - y2 * sin, y2 * cos + y1 * sin], axis=0)
        stats_ref[hh] = jnp.concatenate([jnp.sum(kmain * kmain, axis=0, keepdims=True), qsq, spad], axis=0)
        kfull = jnp.concatenate([kmain, kpad], axis=0).astype(BF16)
        for u in range(tt // LANES):
            kT_ref[hh, u] = kfull[:, LANES * u:LANES * (u + 1)]
        v_ref[hh] = v[:, LANES * hh:LANES * (hh + 1)].astype(BF16)


def _mla_proj(h, winT, wuqT, wkT, wv, vone, gq, gk, cos, sin):
    T = h.shape[0]
    tt = TOK_TILE
    H = MLA_HEADS
    const = lambda shape: pl.BlockSpec(shape, lambda i: (0,) * len(shape))
    return pl.pallas_call(
        _mla_proj_kernel,
        grid=(T // tt,),
        in_specs=[
            pl.BlockSpec((tt, D_MODEL), lambda i: (i, 0)),
            const(winT.shape), const(wuqT.shape), const(wkT.shape), const(wv.shape), const(vone.shape),
            const(gq.shape), const(gk.shape),
            pl.BlockSpec((MLA_ROPE // 2, tt), lambda i: (0, i)),
            pl.BlockSpec((MLA_ROPE // 2, tt), lambda i: (0, i)),
        ],
        out_specs=[
            pl.BlockSpec((H, tt, LANES), lambda i: (0, i, 0)),
            pl.BlockSpec((H, tt // LANES, LANES, LANES), lambda i: (0, i, 0, 0)),
            pl.BlockSpec((H, tt, LANES), lambda i: (0, i, 0)),
            pl.BlockSpec((H, 8, tt), lambda i: (0, 0, i)),
        ],
        out_shape=[
            jax.ShapeDtypeStruct((H, T, LANES), BF16),
            jax.ShapeDtypeStruct((H, T // LANES, LANES, LANES), BF16),
            jax.ShapeDtypeStruct((H, T, LANES), BF16),
            jax.ShapeDtypeStruct((H, 8, T), F32),
        ],
        compiler_params=_cparams(("arbitrary",)),
        name="mla_proj",
    )(h, winT, wuqT, wkT, wv, vone, gq, gk, cos, sin)


def _attn_kernel(q_ref, kT_ref, v_ref, stats_ref, o_ref, *, nf):
    tq = ATT_TILE
    nsub = tq // LANES
    heads = range(2)
    row = lax.broadcasted_iota(jnp.int32, (tq, tq), 0)
    col = lax.broadcasted_iota(jnp.int32, (tq, tq), 1)
    diag_bias = jnp.where((col // CHUNK) <= (row // CHUNK), 0.0, NEG).astype(F32)
    col0 = lax.broadcasted_iota(jnp.int32, (1, LANES), 1)
    bias0 = jnp.where(col0 >= PAD_ROWS, 0.0, NEG).astype(F32)

    def ktile(hh, blk):
        return jnp.concatenate([kT_ref[hh, blk + u] for u in range(nsub)], axis=1)

    def vtile(hh, t):
        return v_ref[hh, pl.ds(pl.multiple_of(META_BLOCK + t * tq, LANES), tq), :]

    def finish(acc):
        return (acc[:, :MLA_V] / acc[:, MLA_V:MLA_V + 1]).astype(BF16)

    def put(r0, n, hh, acc):
        o_ref[pl.ds(r0, n), MLA_V * hh:MLA_V * (hh + 1)] = finish(acc)

    sq_max = [jnp.max(stats_ref[hh], axis=1, keepdims=True) for hh in heads]
    worst = jnp.sqrt(jnp.maximum(sq_max[0][0:1] * sq_max[0][1:2], sq_max[1][0:1] * sq_max[1][1:2]))

    def shifted():
        lane = lax.broadcasted_iota(jnp.int32, (1, LANES), 1)
        col_scale = [jnp.where(lane == MLA_QK, jnp.sqrt(sq_max[hh][0:1]), 1.0).astype(BF16) for hh in heads]

        def with_shift(q, hh):
            return q * col_scale[hh]

        def part(q, kt, vv, bias):
            s = jnp.dot(q, kt, preferred_element_type=F32)
            if bias is not None:
                s = s + bias
            return jnp.dot(jnp.exp2(s).astype(BF16), vv, preferred_element_type=F32)

        for hh in heads:
            q0 = with_shift(q_ref[hh, 0:META_BLOCK, :], hh)
            put(0, META_BLOCK, hh, part(q0, kT_ref[hh, 0], v_ref[hh, 0:META_BLOCK, :], bias0))

        chains = [(hh, r) for hh in heads for r in range(2)]

        def qpair(ii, carry):
            r0 = pl.multiple_of(META_BLOCK + 2 * ii * tq, LANES)
            qs = [with_shift(q_ref[hh, pl.ds(r0 + r * tq, tq), :], hh) for hh, r in chains]

            def kv(j, acc):
                c0 = pl.multiple_of(META_BLOCK + 2 * j * tq, LANES)
                kts = [jnp.concatenate([kT_ref[hh, 1 + 2 * j * nsub + u] for u in range(2 * nsub)], axis=1)
                       for hh in heads]
                vts = [v_ref[hh, pl.ds(c0, 2 * tq), :] for hh in heads]
                return tuple(acc[c] + part(qs[c], kts[hh], vts[hh], None) for c, (hh, r) in enumerate(chains))

            accs = lax.fori_loop(0, ii, kv, tuple(jnp.zeros((tq, LANES), F32) for _ in chains))
            for c, (hh, r) in enumerate(chains):
                acc = accs[c] + part(qs[c], kT_ref[hh, 0], v_ref[hh, 0:META_BLOCK, :], bias0)
                if r == 1:
                    acc = acc + part(qs[c], ktile(hh, 1 + 2 * ii * nsub), vtile(hh, 2 * ii), None)
                acc = acc + part(qs[c], ktile(hh, 1 + (2 * ii + r) * nsub), vtile(hh, 2 * ii + r), diag_bias)
                put(r0 + r * tq, tq, hh, acc)
            return carry

        lax.fori_loop(0, nf // 2, qpair, 0)

    def online():
        def step(q, kt, vv, m, acc, bias):
            s = jnp.dot(q, kt, preferred_element_type=F32)
            if bias is not None:
                s = s + bias
            m_new = jnp.maximum(m, jnp.max(s, axis=-1, keepdims=True))
            p = jnp.exp2(s - m_new)
            acc = jnp.exp2(m - m_new) * acc + jnp.dot(p.astype(BF16), vv, preferred_element_type=F32)
            return m_new, acc

        def first(q, hh, n):
            return step(q, kT_ref[hh, 0], v_ref[hh, 0:META_BLOCK, :], jnp.full((n, 1), NEG, F32),
                        jnp.zeros((n, LANES), F32), bias0)

        for hh in heads:
            put(0, META_BLOCK, hh, first(q_ref[hh, 0:META_BLOCK, :], hh, META_BLOCK)[1])

        def qtile(i, carry):
            r0 = pl.multiple_of(META_BLOCK + i * tq, LANES)
            qs = [q_ref[hh, pl.ds(r0, tq), :] for hh in heads]
            state = []
            for hh in heads:
                state += list(first(qs[hh], hh, tq))

            def kv(j, c):
                out = []
                for hh in heads:
                    out += list(step(qs[hh], ktile(hh, 1 + j * nsub), vtile(hh, j), c[2 * hh], c[2 * hh + 1], None))
                return tuple(out)

            state = lax.fori_loop(0, i, kv, tuple(state))
            for hh in heads:
                m, acc = step(qs[hh], ktile(hh, 1 + i * nsub), vtile(hh, i), state[2 * hh], state[2 * hh + 1],
                              diag_bias)
                put(r0, tq, hh, acc)
            return carry

        lax.fori_loop(0, nf, qtile, 0)

    lax.cond(worst[0, 0] <= ATT_SAFE_BOUND, shifted, online)


def _attention(q, kT, v, stats, B, LK):
    H, T, _ = q.shape
    nf = (LK - META_BLOCK) // ATT_TILE
    nblk = LK // LANES
    return pl.pallas_call(
        functools.partial(_attn_kernel, nf=nf),
        grid=(B, H // 2),
        in_specs=[
            pl.BlockSpec((2, LK, LANES), lambda b, hp: (hp, b, 0)),
            pl.BlockSpec((2, nblk, LANES, LANES), lambda b, hp: (hp, b, 0, 0)),
            pl.BlockSpec((2, LK, LANES), lambda b, hp: (hp, b, 0)),
            pl.BlockSpec((2, 8, LK), lambda b, hp: (hp, 0, b)),
        ],
        out_specs=pl.BlockSpec((LK, 2 * MLA_V), lambda b, hp: (b, hp)),
        out_shape=jax.ShapeDtypeStruct((T, H * MLA_V), BF16),
        compiler_params=_cparams(("arbitrary", "arbitrary")),
        name="mla_attention",
    )(q, kT, v, stats)


def _route(hnew, gffn_ref, wrh_ref, wrl_ref, br_ref, tri_ref, hn_ref, info_ref, cnt_ref, carry_ref):
    tt = hnew.shape[0]
    hn = hnew * lax.rsqrt(jnp.mean(hnew * hnew, axis=-1, keepdims=True) + EPS) * gffn_ref[...]
    hn_hi = hn.astype(BF16)
    hn_lo = (hn - hn_hi.astype(F32)).astype(BF16)
    wrh = wrh_ref[...]
    logits = (jnp.dot(hn_hi, wrh, preferred_element_type=F32)
              + jnp.dot(hn_lo, wrh, preferred_element_type=F32)
              + jnp.dot(hn_hi, wrl_ref[...], preferred_element_type=F32)) + br_ref[...]

    lane = lax.broadcasted_iota(jnp.int32, (tt, LANES), 1)
    lane_f = lane.astype(F32)
    big = float(LANES)
    gmask = (lane >= N_EXPERTS) & (lane < N_EXPERTS + N_GROUPS)
    gl = jnp.where(gmask, logits, NEG)
    gmax = jnp.max(gl, axis=-1, keepdims=True)
    gsel = jnp.min(jnp.where(gl == gmax, lane_f, big), axis=-1, keepdims=True) - float(N_EXPERTS)
    pg = 1.0 / jnp.sum(jnp.exp(gl - gmax), axis=-1, keepdims=True)
    egrp = (lane // EXPERTS_PER_GROUP).astype(F32)
    emask = (lane < N_EXPERTS) & (egrp == gsel)
    el = jnp.where(emask, logits, NEG)
    m1 = jnp.max(el, axis=-1, keepdims=True)
    i1 = jnp.min(jnp.where(el == m1, lane_f, big), axis=-1, keepdims=True)
    el2 = jnp.where(lane_f == i1, NEG, el)
    m2 = jnp.max(el2, axis=-1, keepdims=True)
    i2 = jnp.min(jnp.where(el2 == m2, lane_f, big), axis=-1, keepdims=True)
    t21 = jnp.exp(m2 - m1)
    w1 = pg / (1.0 + t21)
    w2 = w1 * t21

    lo = jnp.minimum(i1, i2)
    hi = jnp.maximum(i1, i2)
    first_is_lo = i1 < i2
    wa = jnp.where(first_is_lo, w1, w2)
    wb = jnp.where(first_is_lo, w2, w1)
    gbase = gsel * float(EXPERTS_PER_GROUP)
    cls = gsel * float(CLASS_STRIDE) + (lo - gbase) * float(EXPERTS_PER_GROUP) + (hi - gbase)
    lane2 = lax.broadcasted_iota(jnp.int32, (tt, N_CLASS_SLOTS), 1).astype(F32)
    sel = lane2 == cls
    oh = jnp.where(sel, 1.0, 0.0)
    before = jnp.dot(tri_ref[...], oh.astype(BF16), preferred_element_type=F32) + carry_ref[...]
    rank = jnp.sum(jnp.where(sel, before, 0.0), axis=-1, keepdims=True)
    carry_ref[...] = carry_ref[...] + jnp.sum(oh, axis=0, keepdims=True)
    info = jnp.where(lane == 0, cls, jnp.where(lane == 1, rank, jnp.where(lane == 2, wa, jnp.where(
        lane == 3, wb, 0.0))))
    hn_ref[:, :D_MODEL] = hn
    hn_ref[:, D_MODEL:] = info
    info_ref[...] = info
    cnt_ref[...] = jnp.broadcast_to(carry_ref[...], cnt_ref.shape)


def _mla_out_kernel(h_ref, a_ref, wo_ref, gffn_ref, wrh_ref, wrl_ref, br_ref, tri_ref,
                    hout_ref, hn_ref, info_ref, cnt_ref, carry_ref):
    @pl.when(pl.program_id(0) == 0)
    def _():
        carry_ref[...] = jnp.zeros_like(carry_ref)

    hnew = h_ref[...] + jnp.dot(a_ref[...], wo_ref[...], preferred_element_type=F32)
    hout_ref[...] = hnew
    _route(hnew, gffn_ref, wrh_ref, wrl_ref, br_ref, tri_ref, hn_ref, info_ref, cnt_ref, carry_ref)


def _gla_out_kernel(h_ref, a_ref, r_ref, gout_ref, wo_ref, gffn_ref, wrh_ref, wrl_ref, br_ref, tri_ref,
                    hout_ref, hn_ref, info_ref, cnt_ref, carry_ref):
    @pl.when(pl.program_id(0) == 0)
    def _():
        carry_ref[...] = jnp.zeros_like(carry_ref)

    o = a_ref[...].astype(F32)
    r = r_ref[...].astype(F32)
    gout = gout_ref[...]
    parts = []
    for hh in range(GLA_HEADS):
        oh = o[:, GLA_DV * hh:GLA_DV * (hh + 1)]
        parts.append(oh * lax.rsqrt(jnp.mean(oh * oh, axis=-1, keepdims=True) + EPS) * gout)
    a = (jnp.concatenate(parts, axis=1) * (r * jax.nn.sigmoid(r))).astype(BF16)
    hnew = h_ref[...] + jnp.dot(a, wo_ref[...], preferred_element_type=F32)
    hout_ref[...] = hnew
    _route(hnew, gffn_ref, wrh_ref, wrl_ref, br_ref, tri_ref, hn_ref, info_ref, cnt_ref, carry_ref)


def _mixer_out(h, a, wo, gffn, wrh, wrl, br, tri, gla_extra=None):
    T = h.shape[0]
    tt = TOK_TILE
    tile = lambda w: pl.BlockSpec((tt, w), lambda i: (i, 0))
    const = lambda shape: pl.BlockSpec(shape, lambda i: (0,) * len(shape))
    if gla_extra is None:
        kern = _mla_out_kernel
        ins = [h, a]
        in_specs = [tile(D_MODEL), tile(a.shape[1])]
    else:
        r, gout = gla_extra
        kern = _gla_out_kernel
        ins = [h, a, r, gout]
        in_specs = [tile(D_MODEL), tile(a.shape[1]), tile(r.shape[1]), const(gout.shape)]
    ins += [wo, gffn, wrh, wrl, br, tri]
    in_specs += [const(wo.shape), const(gffn.shape), const(wrh.shape), const(wrl.shape), const(br.shape),
                 const(tri.shape)]
    return pl.pallas_call(
        kern,
        grid=(T // tt,),
        in_specs=in_specs,
        out_specs=[tile(D_MODEL), tile(ROW_EXT), tile(LANES), pl.BlockSpec((8, N_CLASS_SLOTS), lambda i: (0, 0))],
        out_shape=[
            jax.ShapeDtypeStruct((T, D_MODEL), F32),
            jax.ShapeDtypeStruct((T, ROW_EXT), F32),
            jax.ShapeDtypeStruct((T, LANES), F32),
            jax.ShapeDtypeStruct((8, N_CLASS_SLOTS), F32),
        ],
        scratch_shapes=[pltpu.VMEM((1, N_CLASS_SLOTS), F32)],
        compiler_params=_cparams(("arbitrary",)),
        name="mixer_out_router",
    )(*ins)


def _group(d):
    return lax.shift_right_logical(d, SUBLANES.bit_length() - 1)


def _in_group(d):
    return d & (SUBLANES - 1)


def _dispatch_kernel(dest_ref, hn_ref, xs_in_ref, xs_ref, sem):
    del xs_in_ref
    ng = hn_ref.shape[0]

    def row_copy(g, j, d):
        return pltpu.make_async_copy(hn_ref.at[g, pl.ds(j, 1)],
                                     xs_ref.at[_group(d), pl.ds(_in_group(d), 1)], sem)

    def issue(g, c):
        for j in range(SUBLANES):
            row_copy(g, j, dest_ref[0, 0, g * SUBLANES + j]).start()
        return c

    lax.fori_loop(0, ng, issue, 0)

    def drain(r, c):
        row_copy(0, 0, 0).wait()
        return c

    lax.fori_loop(0, ng * SUBLANES, drain, 0, unroll=8)


def _dispatch(dest3, hn, xs0):
    T = hn.shape[0]
    tt = TOK_TILE
    P = xs0.shape[0]
    out = pl.pallas_call(
        _dispatch_kernel,
        grid=(T // tt,),
        in_specs=[
            pl.BlockSpec((1, 1, tt), lambda i: (i, 0, 0), memory_space=pltpu.SMEM),
            pl.BlockSpec((tt // SUBLANES, SUBLANES, ROW_EXT), lambda i: (i, 0, 0)),
            pl.BlockSpec(memory_space=pl.ANY),
        ],
        out_specs=pl.BlockSpec(memory_space=pl.ANY),
        out_shape=jax.ShapeDtypeStruct((P // SUBLANES, SUBLANES, ROW_EXT), xs0.dtype),
        scratch_shapes=[pltpu.SemaphoreType.DMA],
        input_output_aliases={2: 0},
        compiler_params=_cparams(("arbitrary",)),
        name="moe_dispatch",
    )(dest3, hn.reshape(T // SUBLANES, SUBLANES, ROW_EXT), xs0.reshape(P // SUBLANES, SUBLANES, ROW_EXT))
    return out.reshape(P, ROW_EXT)


def _expert_kernel(ea_ref, eb_ref, nv_ref, xs_ref, *refs):
    del ea_ref, eb_ref
    w_refs, y_ref = refs[:-1], refs[-1]
    tm = EXP_TILE
    valid = 2 * pl.program_id(0) < nv_ref[0]

    def hidden(gu, w):
        g = gu[:, :D_EXPERT]
        return (g * jax.nn.sigmoid(g) * gu[:, D_EXPERT:] * w).astype(BF16)

    @pl.when(valid)
    def _():
        for u in range(2):
            wgua_ref, wda_ref, wgub_ref, wdb_ref = w_refs[4 * u:4 * u + 4]
            rows = slice(u * tm, (u + 1) * tm)
            x = xs_ref[rows, :D_MODEL].astype(BF16)
            gua = jnp.dot(x, wgua_ref[0], preferred_element_type=F32)
            gub = jnp.dot(x, wgub_ref[0], preferred_element_type=F32)
            ha = hidden(gua, xs_ref[rows, D_MODEL + 2:D_MODEL + 3])
            hb = hidden(gub, xs_ref[rows, D_MODEL + 3:D_MODEL + 4])
            y_ref[rows, :] = (jnp.dot(ha, wda_ref[0], preferred_element_type=F32)
                              + jnp.dot(hb, wdb_ref[0], preferred_element_type=F32))

    @pl.when(jnp.logical_not(valid))
    def _():
        y_ref[...] = jnp.zeros_like(y_ref)


def _experts(tile_ea, tile_eb, nvalid, xs, wgu, wd):
    P = xs.shape[0]
    tm = EXP_TILE
    assert (P // tm) % 2 == 0
    w_specs = []
    for u in range(2):
        for table in (0, 1):
            idx = lambda i, ea, eb, nv, u=u, table=table: ((ea, eb)[table][2 * i + u], 0, 0)
            w_specs += [pl.BlockSpec((1, D_MODEL, 2 * D_EXPERT), idx), pl.BlockSpec((1, D_EXPERT, D_MODEL), idx)]
    return pl.pallas_call(
        _expert_kernel,
        grid_spec=pltpu.PrefetchScalarGridSpec(
            num_scalar_prefetch=3,
            grid=(P // (2 * tm),),
            in_specs=[pl.BlockSpec((2 * tm, ROW_EXT),
                                   lambda i, ea, eb, nv: (jnp.minimum(i, (nv[0] - 1) // 2), 0))] + w_specs,
            out_specs=pl.BlockSpec((2 * tm, D_MODEL), lambda i, ea, eb, nv: (i, 0)),
        ),
        out_shape=jax.ShapeDtypeStruct((P, D_MODEL), F32),
        compiler_params=_cparams(("arbitrary",)),
        name="moe_experts",
    )(tile_ea, tile_eb, nvalid, xs, *([wgu, wd] * 4))


def _combine_kernel(dest_ref, h_ref, y_ref, out_ref, ybuf, sem):
    ng = ybuf.shape[0]

    def row_copy(g, j, d):
        return pltpu.make_async_copy(y_ref.at[_group(d), pl.ds(_in_group(d), 1)],
                                     ybuf.at[g, pl.ds(j, 1)], sem)

    def issue(g, c):
        for j in range(SUBLANES):
            row_copy(g, j, dest_ref[0, 0, g * SUBLANES + j]).start()
        return c

    lax.fori_loop(0, ng, issue, 0)

    def drain(r, c):
        row_copy(0, 0, 0).wait()
        return c

    lax.fori_loop(0, ng * SUBLANES, drain, 0, unroll=8)
    out_ref[...] = h_ref[...] + ybuf[...].reshape(out_ref.shape)


def _combine(dest3, h, y):
    T = h.shape[0]
    tt = TOK_TILE
    P = y.shape[0]
    return pl.pallas_call(
        _combine_kernel,
        grid=(T // tt,),
        in_specs=[
            pl.BlockSpec((1, 1, tt), lambda i: (i, 0, 0), memory_space=pltpu.SMEM),
            pl.BlockSpec((tt, D_MODEL), lambda i: (i, 0)),
            pl.BlockSpec(memory_space=pl.ANY),
        ],
        out_specs=pl.BlockSpec((tt, D_MODEL), lambda i: (i, 0)),
        out_shape=jax.ShapeDtypeStruct((T, D_MODEL), F32),
        scratch_shapes=[pltpu.VMEM((tt // SUBLANES, SUBLANES, D_MODEL), F32), pltpu.SemaphoreType.DMA],
        compiler_params=_cparams(("arbitrary",)),
        name="moe_combine",
    )(dest3, h, y.reshape(P // SUBLANES, SUBLANES, D_MODEL))


def _moe(h, hn_ext, info, cnt, wgu, wd, xs_init):
    T = h.shape[0]
    tm = EXP_TILE
    ntiles = xs_init.shape[0] // tm
    counts = cnt[0].astype(jnp.int32)
    tiles_c = (counts + tm - 1) // tm
    cum = jnp.cumsum(tiles_c)
    start_row = ((cum - tiles_c) * tm).astype(jnp.int32)
    tile_class = jnp.minimum(
        jnp.sum(jnp.arange(ntiles, dtype=jnp.int32)[:, None] >= cum[None, :], axis=1), N_CLASS_SLOTS - 1
    ).astype(jnp.int32)
    group_base = (tile_class // CLASS_STRIDE) * EXPERTS_PER_GROUP
    tile_ea = group_base + (tile_class % CLASS_STRIDE) // EXPERTS_PER_GROUP
    tile_eb = group_base + tile_class % EXPERTS_PER_GROUP
    nvalid = cum[-1:].astype(jnp.int32)
    cr = info[:, 0:2].astype(jnp.int32)
    onehot = (cr[:, 0:1] == jnp.arange(N_CLASS_SLOTS, dtype=jnp.int32)[None, :]).astype(BF16)
    digits = jnp.stack([start_row // 256, start_row % 256], axis=1).astype(BF16)
    hl = jnp.dot(onehot, digits, preferred_element_type=F32).astype(jnp.int32)
    dest3 = (hl[:, 0] * 256 + hl[:, 1] + cr[:, 1]).reshape(T // TOK_TILE, 1, TOK_TILE)
    xs = _dispatch(dest3, hn_ext, xs_init)
    y = _experts(tile_ea, tile_eb, nvalid, xs, wgu, wd)
    return _combine(dest3, h, y), xs


def _gla_in_kernel(h_ref, w_ref, wgu_ref, bg_ref, q_ref, k_ref, g_ref, v_ref, r_ref):
    h = h_ref[...]
    hn = (h * lax.rsqrt(jnp.mean(h * h, axis=-1, keepdims=True) + EPS)).astype(BF16)
    z = jnp.dot(hn, w_ref[...], preferred_element_type=F32)
    nk = GLA_HEADS * GLA_DK
    nv = GLA_HEADS * GLA_DV
    o_v = 2 * nk
    o_g = o_v + nv
    o_r = o_g + LANES
    xg = jnp.dot(z[:, o_g:o_r].astype(BF16), wgu_ref[...], preferred_element_type=F32) + bg_ref[...]
    log_a = (jnp.minimum(xg, 0.0) - jnp.log(1.0 + jnp.exp(-jnp.abs(xg)))) * (1.0 / GLA_TAU)
    for hh in range(GLA_HEADS):
        q_ref[hh] = z[:, GLA_DK * hh:GLA_DK * (hh + 1)].astype(BF16)
        k_ref[hh] = z[:, nk + GLA_DK * hh:nk + GLA_DK * (hh + 1)].astype(BF16)
        g_ref[hh] = log_a[:, GLA_DK * hh:GLA_DK * (hh + 1)]
        v_ref[hh] = z[:, o_v + GLA_DV * hh:o_v + GLA_DV * (hh + 1)].astype(BF16)
    r_ref[...] = z[:, o_r:].astype(BF16)


def _gla_in(h, w, wgu, bg):
    T = h.shape[0]
    tt = TOK_TILE
    GH = GLA_HEADS
    const = lambda shape: pl.BlockSpec(shape, lambda i: (0,) * len(shape))
    hspec = lambda w_: pl.BlockSpec((GH, tt, w_), lambda i: (0, i, 0))
    return pl.pallas_call(
        _gla_in_kernel,
        grid=(T // tt,),
        in_specs=[pl.BlockSpec((tt, D_MODEL), lambda i: (i, 0)), const(w.shape), const(wgu.shape), const(bg.shape)],
        out_specs=[hspec(GLA_DK), hspec(GLA_DK), hspec(GLA_DK), hspec(GLA_DV),
                   pl.BlockSpec((tt, GH * GLA_DV), lambda i: (i, 0))],
        out_shape=[
            jax.ShapeDtypeStruct((GH, T, GLA_DK), BF16),
            jax.ShapeDtypeStruct((GH, T, GLA_DK), BF16),
            jax.ShapeDtypeStruct((GH, T, GLA_DK), F32),
            jax.ShapeDtypeStruct((GH, T, GLA_DV), BF16),
            jax.ShapeDtypeStruct((T, GH * GLA_DV), BF16),
        ],
        compiler_params=_cparams(("arbitrary",)),
        name="gla_in",
    )(h, w, wgu, bg)


def _bdot(a, b, contract_b):
    return lax.dot_general(a, b, (((2,), (contract_b,)), ((0,), (0,))), preferred_element_type=F32)


def _gla_scan_kernel(q_ref, k_ref, g_ref, v_ref, stack_ref, o_ref, st_ref, oin_ref, qe_ref, kd_ref, dec_ref, *,
                     tile):
    C = SCAN_CHUNK
    nc = tile // C
    t = pl.program_id(1)

    @pl.when(t == 0)
    def _():
        st_ref[...] = jnp.zeros_like(st_ref)

    row = lax.broadcasted_iota(jnp.int32, (C, C), 0)
    col = lax.broadcasted_iota(jnp.int32, (C, C), 1)
    diag = (row == col)[None]
    masks = [(((row // (2 * c)) == (col // (2 * c))) & ((row % (2 * c)) >= c) & ((col % (2 * c)) < c))[None]
             for c in GLA_LEVELS]
    real = (t * tile + lax.broadcasted_iota(jnp.int32, (tile, 1), 0) >= PAD_ROWS).reshape(nc, C, 1)
    stack = stack_ref[...]

    for hh in range(GLA_HEADS):
        q = q_ref[hh].astype(F32).reshape(nc, C, GLA_DK)
        k = jnp.where(real, k_ref[hh].astype(F32).reshape(nc, C, GLA_DK), 0.0)
        g = g_ref[hh].reshape(nc, C, GLA_DK)
        v = v_ref[hh].reshape(nc, C, GLA_DV)
        g_hi = g.astype(BF16)
        g_lo = (g - g_hi.astype(F32)).astype(BF16)
        sums2 = _bdot(stack, jnp.concatenate([g_hi, g_lo], axis=2), 1)
        sums = sums2[:, :, :GLA_DK] + sums2[:, :, GLA_DK:]
        b = sums[:, 0:C]
        b_last = b[:, C - 1:C]
        att = jnp.where(diag, _bdot(q.astype(BF16), k.astype(BF16), 2), 0.0)
        for li in range(len(GLA_LEVELS)):
            ref = sums[:, C * (li + 1):C * (li + 2)]
            qf = (q * jnp.exp(jnp.minimum(b - ref, 0.0))).astype(BF16)
            kb = (k * jnp.exp(jnp.minimum(ref - b, 0.0))).astype(BF16)
            att = att + jnp.where(masks[li], _bdot(qf, kb, 2), 0.0)
        oin_ref[hh] = _bdot(att.astype(BF16), v, 1).reshape(tile, GLA_DV)
        qe_ref[hh] = (q * jnp.exp(b)).astype(BF16).reshape(tile, GLA_DK)
        kd_ref[hh] = (k * jnp.exp(b_last - b)).astype(BF16).reshape(tile, GLA_DK)
        dec_ref[hh] = jnp.broadcast_to(jnp.exp(b_last), (nc, 8, GLA_DK))

    def chunk(ci, carry):
        r0 = pl.multiple_of(ci * C, C)
        for hh in range(GLA_HEADS):
            st = st_ref[hh]
            o = oin_ref[hh, pl.ds(r0, C), :] + _nt_dot(qe_ref[hh, pl.ds(r0, C), :], st.astype(BF16))
            o_ref[pl.ds(r0, C), GLA_DV * hh:GLA_DV * (hh + 1)] = o.astype(BF16)
            st_ref[hh] = st * dec_ref[hh, ci][0:1] + _tn_dot(v_ref[hh, pl.ds(r0, C), :], kd_ref[hh, pl.ds(r0, C), :])
        return carry

    lax.fori_loop(0, nc, chunk, 0)


def _gla_scan(q, k, g, v, stack, B, LK):
    GH, T, _ = q.shape
    tile = 640 if LK % 640 == 0 else LANES
    nt = LK // tile
    nc = tile // SCAN_CHUNK
    stack3 = jnp.broadcast_to(stack[None], (nc,) + stack.shape)
    hspec = lambda w_: pl.BlockSpec((GH, tile, w_), lambda b, t: (0, b * nt + t, 0))
    return pl.pallas_call(
        functools.partial(_gla_scan_kernel, tile=tile),
        grid=(B, nt),
        in_specs=[hspec(GLA_DK), hspec(GLA_DK), hspec(GLA_DK), hspec(GLA_DV),
                  pl.BlockSpec(stack3.shape, lambda b, t: (0, 0, 0))],
        out_specs=pl.BlockSpec((tile, GH * GLA_DV), lambda b, t: (b * nt + t, 0)),
        out_shape=jax.ShapeDtypeStruct((T, GH * GLA_DV), BF16),
        scratch_shapes=[
            pltpu.VMEM((GH, GLA_DV, GLA_DK), F32),
            pltpu.VMEM((GH, tile, GLA_DV), F32),
            pltpu.VMEM((GH, tile, GLA_DK), BF16),
            pltpu.VMEM((GH, tile, GLA_DK), BF16),
            pltpu.VMEM((GH, nc, 8, GLA_DK), F32),
        ],
        compiler_params=_cparams(("arbitrary", "arbitrary")),
        name="gla_scan",
    )(q, k, g, v, stack3)


def _gla_stack():
    C = SCAN_CHUNK
    t = jnp.arange(C)[:, None]
    u = jnp.arange(C)[None, :]
    mats = [u <= t]
    for c in GLA_LEVELS:
        boundary = (t // (2 * c)) * (2 * c) + c - 1
        mats.append(u <= boundary)
    return jnp.concatenate(mats, axis=0).astype(BF16)


def _router_weights(w_rg, b_rg, w_re, b_re):
    w = jnp.zeros((D_MODEL, LANES), F32).at[:, :N_EXPERTS].set(w_re).at[:, N_EXPERTS:N_EXPERTS + N_GROUPS].set(w_rg)
    b = jnp.zeros((1, LANES), F32).at[0, :N_EXPERTS].set(b_re).at[0, N_EXPERTS:N_EXPERTS + N_GROUPS].set(b_rg)
    w_hi = w.astype(BF16)
    w_lo = (w - w_hi.astype(F32)).astype(BF16)
    return w_hi, w_lo, b


def kernel(x, meta_tokens, norm_mix, norm_ffn, mla_w_in, mla_g_q, mla_w_uq, mla_g_kv, mla_w_ukv, mla_g_qn, mla_g_kn, mla_w_o, gla_w_in, gla_w_gate_up, gla_b_gate, gla_g_out, gla_w_o, moe_w_rg, moe_b_rg, moe_w_re, moe_b_re, moe_w_gate, moe_w_up, moe_w_down):
    B, S, D = x.shape
    assert D == D_MODEL and S % (2 * ATT_TILE) == 0
    LK = META_BLOCK + S
    T = B * LK
    assert T % TOK_TILE == 0 and TOK_TILE % LANES == 0
    tt = TOK_TILE
    H = MLA_HEADS

    meta = jnp.broadcast_to(meta_tokens.astype(F32)[None], (B, N_META, D))
    h = jnp.zeros((B, LK, D), F32).at[:, PAD_ROWS:META_BLOCK].set(meta).at[:, META_BLOCK:].set(x.astype(F32))
    h = h.reshape(T, D)

    rows = jnp.arange(LK)
    pos = jnp.where(rows < META_BLOCK, jnp.maximum(rows - PAD_ROWS, 0), rows - META_BLOCK + N_META)
    half = MLA_ROPE // 2
    inv = 1.0 / (ROPE_THETA ** (jnp.arange(half, dtype=F32) / half))
    ang = inv[:, None] * pos.astype(F32)[None, :]
    cos = jnp.tile(jnp.cos(ang), (1, B))
    sin = jnp.tile(jnp.sin(ang), (1, B))

    tri = (jnp.arange(tt)[None, :] < jnp.arange(tt)[:, None]).astype(BF16)

    winT = (mla_w_in[0] * norm_mix[0][:, None]).T.astype(BF16)
    wuqT = (mla_w_uq[0] * mla_g_q[0][:, None]).T.astype(BF16)
    wukv = (mla_w_ukv[0] * mla_g_kv[0][:, None]).reshape(MLA_KV_RANK, H, MLA_NOPE + MLA_V)
    wkT = wukv[:, :, :MLA_NOPE].reshape(MLA_KV_RANK, H * MLA_NOPE).T.astype(BF16)
    wv = jnp.zeros((MLA_KV_RANK, H, LANES), F32).at[:, :, :MLA_V].set(wukv[:, :, MLA_NOPE:])
    wv = wv.reshape(MLA_KV_RANK, H * LANES).astype(BF16)
    vone = jnp.zeros((1, H, LANES), F32).at[:, :, MLA_V].set(1.0).reshape(1, H * LANES)
    gq = jnp.broadcast_to((mla_g_qn[0] * (MLA_QK ** -0.5 * LOG2E))[:, None], (MLA_QK, tt)).astype(F32)
    gk = jnp.broadcast_to(mla_g_kn[0][:, None], (MLA_QK, tt)).astype(F32)
    q, kT, v, stats = _mla_proj(h, winT, wuqT, wkT, wv, vone, gq, gk, cos, sin)
    att = _attention(q, kT, v, stats, B, LK)

    wrh, wrl, br = _router_weights(moe_w_rg[0], moe_b_rg[0], moe_w_re[0], moe_b_re[0])
    h, hn, info, cnt = _mixer_out(h, att, mla_w_o[0].astype(BF16), norm_ffn[0][None, :], wrh, wrl, br, tri)
    wgu = jnp.concatenate([moe_w_gate[0], moe_w_up[0]], axis=-1).astype(BF16)
    xs0 = jnp.zeros(((T // EXP_TILE + N_CLASSES) * EXP_TILE, ROW_EXT), F32)
    h, xs1 = _moe(h, hn, info, cnt, wgu, moe_w_down[0].astype(BF16), xs0)

    nk = GLA_HEADS * GLA_DK
    nv = GLA_HEADS * GLA_DV
    w1 = gla_w_in[0] * norm_mix[1][:, None]
    wg_pad = jnp.zeros((D, LANES), F32).at[:, :GLA_GATE_RANK].set(w1[:, 2 * nk + nv:2 * nk + nv + GLA_GATE_RANK])
    w_all = jnp.concatenate([w1[:, :nk] * (GLA_DK ** -0.5), w1[:, nk:2 * nk + nv], wg_pad,
                             w1[:, 2 * nk + nv + GLA_GATE_RANK:]], axis=1).astype(BF16)
    wgate = jnp.zeros((LANES, nk), F32).at[:GLA_GATE_RANK].set(gla_w_gate_up[0]).astype(BF16)
    gq_, gk_, gg_, gv_, gr_ = _gla_in(h, w_all, wgate, gla_b_gate[0][None, :])
    go = _gla_scan(gq_, gk_, gg_, gv_, _gla_stack(), B, LK)

    wrh, wrl, br = _router_weights(moe_w_rg[1], moe_b_rg[1], moe_w_re[1], moe_b_re[1])
    h, hn, info, cnt = _mixer_out(h, go, gla_w_o[0].astype(BF16), norm_ffn[1][None, :], wrh, wrl, br, tri,
                                  gla_extra=(gr_, gla_g_out[0][None, :]))
    wgu = jnp.concatenate([moe_w_gate[1], moe_w_up[1]], axis=-1).astype(BF16)
    h, _ = _moe(h, hn, info, cnt, wgu, moe_w_down[1].astype(BF16), xs1)

    return h.reshape(B, LK, D)[:, META_BLOCK:].astype(x.dtype)
```

```python
import functools

import jax
import jax.numpy as jnp
from jax import lax
from jax.experimental import pallas as pl
from jax.experimental.pallas import tpu as pltpu

F32 = jnp.float32
BF16 = jnp.bfloat16

D_MODEL = 1024
CHUNK = 64
N_META = 16
MLA_HEADS = 16
MLA_Q_RANK = 384
MLA_KV_RANK = 256
MLA_NOPE = 64
MLA_ROPE = 32
MLA_V = 64
MLA_QK = MLA_NOPE + MLA_ROPE
ROPE_THETA = 10000.0
GLA_HEADS = 4
GLA_DK = 128
GLA_DV = 256
GLA_GATE_RANK = 16
GLA_TAU = 16.0
N_GROUPS = 4
EXPERTS_PER_GROUP = 8
N_EXPERTS = N_GROUPS * EXPERTS_PER_GROUP
D_EXPERT = 256
EPS = 1e-6

LANES = 128
SUBLANES = 8
META_BLOCK = 128
PAD_ROWS = META_BLOCK - N_META
NEG = -1e30
LOG2E = 1.4426950408889634

TOK_TILE = 512
ATT_TILE = 512
ATT_SAFE_BOUND = 50.0
SCAN_CHUNK = 64
EXP_TILE = 128
CLASS_STRIDE = EXPERTS_PER_GROUP * EXPERTS_PER_GROUP
N_CLASS_SLOTS = N_GROUPS * CLASS_STRIDE
N_CLASSES = N_GROUPS * (EXPERTS_PER_GROUP * (EXPERTS_PER_GROUP - 1) // 2)
ROW_EXT = D_MODEL + LANES
VMEM_LIMIT = 56 * 1024 * 1024

GLA_LEVELS = (32, 16, 8, 4, 2, 1)


def _cparams(sem):
    return pltpu.CompilerParams(dimension_semantics=sem, vmem_limit_bytes=VMEM_LIMIT)


def _nt_dot(a, b):
    return lax.dot_general(a, b, (((1,), (1,)), ((), ())), preferred_element_type=F32)


def _tn_dot(a, b):
    return lax.dot_general(a, b, (((0,), (0,)), ((), ())), preferred_element_type=F32)


def _mla_proj_kernel(h_ref, winT_ref, wuqT_ref, wkT_ref, wv_ref, vone_ref, gq_ref, gk_ref,
                     cos_ref, sin_ref, q_ref, kT_ref, v_ref, stats_ref):
    tt = h_ref.shape[0]
    h = h_ref[...]
    hn = (h * lax.rsqrt(jnp.mean(h * h, axis=-1, keepdims=True) + EPS)).astype(BF16)
    zT = _nt_dot(winT_ref[...], hn)
    cq = zT[0:MLA_Q_RANK]
    ckv = zT[MLA_Q_RANK:MLA_Q_RANK + MLA_KV_RANK]
    kr = zT[MLA_Q_RANK + MLA_KV_RANK:]
    cqn = (cq * lax.rsqrt(jnp.mean(cq * cq, axis=0, keepdims=True) + EPS)).astype(BF16)
    ckvn = ckv * lax.rsqrt(jnp.mean(ckv * ckv, axis=0, keepdims=True) + EPS)
    qT = jnp.dot(wuqT_ref[...], cqn, preferred_element_type=F32)
    kT = jnp.dot(wkT_ref[...], ckvn.astype(BF16), preferred_element_type=F32)
    v = jnp.dot(ckvn.T.astype(BF16), wv_ref[...], preferred_element_type=F32) + vone_ref[...]
    cos = cos_ref[...]
    sin = sin_ref[...]
    gq = gq_ref[...]
    gk = gk_ref[...]
    half = MLA_ROPE // 2
    zpad = jnp.zeros((LANES - MLA_QK - 1, tt), F32)
    kpad = jnp.concatenate([jnp.ones((1, tt), F32), zpad], axis=0)
    spad = jnp.zeros((6, tt), F32)
    kr_ss = jnp.sum(kr * kr, axis=0, keepdims=True)
    for hh in range(MLA_HEADS):
        qh = qT[MLA_QK * hh:MLA_QK * (hh + 1)]
        qn = qh * lax.rsqrt(jnp.sum(qh * qh, axis=0, keepdims=True) * (1.0 / MLA_QK) + EPS) * gq
        x1 = qn[MLA_NOPE:MLA_NOPE + half]
        x2 = qn[MLA_NOPE + half:]
        qmain = jnp.concatenate([qn[:MLA_NOPE], x1 * cos - x2 * sin, x2 * cos + x1 * sin], axis=0)
        qsq = jnp.sum(qmain * qmain, axis=0, keepdims=True)
        q_ref[hh] = jnp.concatenate([qmain, -jnp.sqrt(qsq), zpad], axis=0).T.astype(BF16)
        kn = kT[MLA_NOPE * hh:MLA_NOPE * (hh + 1)]
        rk = lax.rsqrt((jnp.sum(kn * kn, axis=0, keepdims=True) + kr_ss) * (1.0 / MLA_QK) + EPS)
        knn = kn * rk * gk[:MLA_NOPE]
        krn = kr * rk * gk[MLA_NOPE:]
        y1 = krn[:half]
        y2 = krn[half:]
        kmain = jnp.concatenate([knn, y1 * cos - y2 * sin, y2 * cos + y1 * sin], axis=0)
        stats_ref[hh] = jnp.concatenate([jnp.sum(kmain * kmain, axis=0, keepdims=True), qsq, spad], axis=0)
        kfull = jnp.concatenate([kmain, kpad], axis=0).astype(BF16)
        for u in range(tt // LANES):
            kT_ref[hh, u] = kfull[:, LANES * u:LANES * (u + 1)]
        v_ref[hh] = v[:, LANES * hh:LANES * (hh + 1)].astype(BF16)


def _mla_proj(h, winT, wuqT, wkT, wv, vone, gq, gk, cos, sin):
    T = h.shape[0]
    tt = TOK_TILE
    H = MLA_HEADS
    const = lambda shape: pl.BlockSpec(shape, lambda i: (0,) * len(shape))
    return pl.pallas_call(
        _mla_proj_kernel,
        grid=(T // tt,),
        in_specs=[
            pl.BlockSpec((tt, D_MODEL), lambda i: (i, 0)),
            const(winT.shape), const(wuqT.shape), const(wkT.shape), const(wv.shape), const(vone.shape),
            const(gq.shape), const(gk.shape),
            pl.BlockSpec((MLA_ROPE // 2, tt), lambda i: (0, i)),
            pl.BlockSpec((MLA_ROPE // 2, tt), lambda i: (0, i)),
        ],
        out_specs=[
            pl.BlockSpec((H, tt, LANES), lambda i: (0, i, 0)),
            pl.BlockSpec((H, tt // LANES, LANES, LANES), lambda i: (0, i, 0, 0)),
            pl.BlockSpec((H, tt, LANES), lambda i: (0, i, 0)),
            pl.BlockSpec((H, 8, tt), lambda i: (0, 0, i)),
        ],
        out_shape=[
            jax.ShapeDtypeStruct((H, T, LANES), BF16),
            jax.ShapeDtypeStruct((H, T // LANES, LANES, LANES), BF16),
            jax.ShapeDtypeStruct((H, T, LANES), BF16),
            jax.ShapeDtypeStruct((H, 8, T), F32),
        ],
        compiler_params=_cparams(("arbitrary",)),
        name="mla_proj",
    )(h, winT, wuqT, wkT, wv, vone, gq, gk, cos, sin)


def _attn_kernel(q_ref, kT_ref, v_ref, stats_ref, o_ref, *, nf):
    tq = ATT_TILE
    nsub = tq // LANES
    heads = range(2)
    row = lax.broadcasted_iota(jnp.int32, (tq, tq), 0)
    col = lax.broadcasted_iota(jnp.int32, (tq, tq), 1)
    diag_bias = jnp.where((col // CHUNK) <= (row // CHUNK), 0.0, NEG).astype(F32)
    col0 = lax.broadcasted_iota(jnp.int32, (1, LANES), 1)
    bias0 = jnp.where(col0 >= PAD_ROWS, 0.0, NEG).astype(F32)

    def ktile(hh, blk):
        return jnp.concatenate([kT_ref[hh, blk + u] for u in range(nsub)], axis=1)

    def vtile(hh, t):
        return v_ref[hh, pl.ds(pl.multiple_of(META_BLOCK + t * tq, LANES), tq), :]

    def finish(acc):
        return (acc[:, :MLA_V] / acc[:, MLA_V:MLA_V + 1]).astype(BF16)

    def put(r0, n, hh, acc):
        o_ref[pl.ds(r0, n), MLA_V * hh:MLA_V * (hh + 1)] = finish(acc)

    sq_max = [jnp.max(stats_ref[hh], axis=1, keepdims=True) for hh in heads]
    worst = jnp.sqrt(jnp.maximum(sq_max[0][0:1] * sq_max[0][1:2], sq_max[1][0:1] * sq_max[1][1:2]))

    def shifted():
        lane = lax.broadcasted_iota(jnp.int32, (1, LANES), 1)
        col_scale = [jnp.where(lane == MLA_QK, jnp.sqrt(sq_max[hh][0:1]), 1.0).astype(BF16) for hh in heads]

        def with_shift(q, hh):
            return q * col_scale[hh]

        def part(q, kt, vv, bias):
            s = jnp.dot(q, kt, preferred_element_type=F32)
            if bias is not None:
                s = s + bias
            return jnp.dot(jnp.exp2(s).astype(BF16), vv, preferred_element_type=F32)

        for hh in heads:
            q0 = with_shift(q_ref[hh, 0:META_BLOCK, :], hh)
            put(0, META_BLOCK, hh, part(q0, kT_ref[hh, 0], v_ref[hh, 0:META_BLOCK, :], bias0))

        chains = [(hh, r) for hh in heads for r in range(2)]

        def qpair(ii, carry):
            r0 = pl.multiple_of(META_BLOCK + 2 * ii * tq, LANES)
            qs = [with_shift(q_ref[hh, pl.ds(r0 + r * tq, tq), :], hh) for hh, r in chains]

            def kv(j, acc):
                c0 = pl.multiple_of(META_BLOCK + 2 * j * tq, LANES)
                kts = [jnp.concatenate([kT_ref[hh, 1 + 2 * j * nsub + u] for u in range(2 * nsub)], axis=1)
                       for hh in heads]
                vts = [v_ref[hh, pl.ds(c0, 2 * tq), :] for hh in heads]
                return tuple(acc[c] + part(qs[c], kts[hh], vts[hh], None) for c, (hh, r) in enumerate(chains))

            accs = lax.fori_loop(0, ii, kv, tuple(jnp.zeros((tq, LANES), F32) for _ in chains))
            for c, (hh, r) in enumerate(chains):
                acc = accs[c] + part(qs[c], kT_ref[hh, 0], v_ref[hh, 0:META_BLOCK, :], bias0)
                if r == 1:
                    acc = acc + part(qs[c], ktile(hh, 1 + 2 * ii * nsub), vtile(hh, 2 * ii), None)
                acc = acc + part(qs[c], ktile(hh, 1 + (2 * ii + r) * nsub), vtile(hh, 2 * ii + r), diag_bias)
                put(r0 + r * tq, tq, hh, acc)
            return carry

        lax.fori_loop(0, nf // 2, qpair, 0)

    def online():
        def step(q, kt, vv, m, acc, bias):
            s = jnp.dot(q, kt, preferred_element_type=F32)
            if bias is not None:
                s = s + bias
            m_new = jnp.maximum(m, jnp.max(s, axis=-1, keepdims=True))
            p = jnp.exp2(s - m_new)
            acc = jnp.exp2(m - m_new) * acc + jnp.dot(p.astype(BF16), vv, preferred_element_type=F32)
            return m_new, acc

        def first(q, hh, n):
            return step(q, kT_ref[hh, 0], v_ref[hh, 0:META_BLOCK, :], jnp.full((n, 1), NEG, F32),
                        jnp.zeros((n, LANES), F32), bias0)

        for hh in heads:
            put(0, META_BLOCK, hh, first(q_ref[hh, 0:META_BLOCK, :], hh, META_BLOCK)[1])

        def qtile(i, carry):
            r0 = pl.multiple_of(META_BLOCK + i * tq, LANES)
            qs = [q_ref[hh, pl.ds(r0, tq), :] for hh in heads]
            state = []
            for hh in heads:
                state += list(first(qs[hh], hh, tq))

            def kv(j, c):
                out = []
                for hh in heads:
                    out += list(step(qs[hh], ktile(hh, 1 + j * nsub), vtile(hh, j), c[2 * hh], c[2 * hh + 1], None))
                return tuple(out)

            state = lax.fori_loop(0, i, kv, tuple(state))
            for hh in heads:
                m, acc = step(qs[hh], ktile(hh, 1 + i * nsub), vtile(hh, i), state[2 * hh], state[2 * hh + 1],
                              diag_bias)
                put(r0, tq, hh, acc)
            return carry

        lax.fori_loop(0, nf, qtile, 0)

    lax.cond(worst[0, 0] <= ATT_SAFE_BOUND, shifted, online)


def _attention(q, kT, v, stats, B, LK):
    H, T, _ = q.shape
    nf = (LK - META_BLOCK) // ATT_TILE
    nblk = LK // LANES
    return pl.pallas_call(
        functools.partial(_attn_kernel, nf=nf),
        grid=(B, H // 2),
        in_specs=[
            pl.BlockSpec((2, LK, LANES), lambda b, hp: (hp, b, 0)),
            pl.BlockSpec((2, nblk, LANES, LANES), lambda b, hp: (hp, b, 0, 0)),
            pl.BlockSpec((2, LK, LANES), lambda b, hp: (hp, b, 0)),
            pl.BlockSpec((2, 8, LK), lambda b, hp: (hp, 0, b)),
        ],
        out_specs=pl.BlockSpec((LK, 2 * MLA_V), lambda b, hp: (b, hp)),
        out_shape=jax.ShapeDtypeStruct((T, H * MLA_V), BF16),
        compiler_params=_cparams(("arbitrary", "arbitrary")),
        name="mla_attention",
    )(q, kT, v, stats)


def _route(hnew, gffn_ref, wrh_ref, wrl_ref, br_ref, tri_ref, hn_ref, info_ref, cnt_ref, carry_ref):
    tt = hnew.shape[0]
    hn = hnew * lax.rsqrt(jnp.mean(hnew * hnew, axis=-1, keepdims=True) + EPS) * gffn_ref[...]
    hn_hi = hn.astype(BF16)
    hn_lo = (hn - hn_hi.astype(F32)).astype(BF16)
    wrh = wrh_ref[...]
    logits = (jnp.dot(hn_hi, wrh, preferred_element_type=F32)
              + jnp.dot(hn_lo, wrh, preferred_element_type=F32)
              + jnp.dot(hn_hi, wrl_ref[...], preferred_element_type=F32)) + br_ref[...]

    lane = lax.broadcasted_iota(jnp.int32, (tt, LANES), 1)
    lane_f = lane.astype(F32)
    big = float(LANES)
    gmask = (lane >= N_EXPERTS) & (lane < N_EXPERTS + N_GROUPS)
    gl = jnp.where(gmask, logits, NEG)
    gmax = jnp.max(gl, axis=-1, keepdims=True)
    gsel = jnp.min(jnp.where(gl == gmax, lane_f, big), axis=-1, keepdims=True) - float(N_EXPERTS)
    pg = 1.0 / jnp.sum(jnp.exp(gl - gmax), axis=-1, keepdims=True)
    egrp = (lane // EXPERTS_PER_GROUP).astype(F32)
    emask = (lane < N_EXPERTS) & (egrp == gsel)
    el = jnp.where(emask, logits, NEG)
    m1 = jnp.max(el, axis=-1, keepdims=True)
    i1 = jnp.min(jnp.where(el == m1, lane_f, big), axis=-1, keepdims=True)
    el2 = jnp.where(lane_f == i1, NEG, el)
    m2 = jnp.max(el2, axis=-1, keepdims=True)
    i2 = jnp.min(jnp.where(el2 == m2, lane_f, big), axis=-1, keepdims=True)
    t21 = jnp.exp(m2 - m1)
    w1 = pg / (1.0 + t21)
    w2 = w1 * t21

    lo = jnp.minimum(i1, i2)
    hi = jnp.maximum(i1, i2)
    first_is_lo = i1 < i2
    wa = jnp.where(first_is_lo, w1, w2)
    wb = jnp.where(first_is_lo, w2, w1)
    gbase = gsel * float(EXPERTS_PER_GROUP)
    cls = gsel * float(CLASS_STRIDE) + (lo - gbase) * float(EXPERTS_PER_GROUP) + (hi - gbase)
    lane2 = lax.broadcasted_iota(jnp.int32, (tt, N_CLASS_SLOTS), 1).astype(F32)
    sel = lane2 == cls
    oh = jnp.where(sel, 1.0, 0.0)
    before = jnp.dot(tri_ref[...], oh.astype(BF16), preferred_element_type=F32) + carry_ref[...]
    rank = jnp.sum(jnp.where(sel, before, 0.0), axis=-1, keepdims=True)
    carry_ref[...] = carry_ref[...] + jnp.sum(oh, axis=0, keepdims=True)
    info = jnp.where(lane == 0, cls, jnp.where(lane == 1, rank, jnp.where(lane == 2, wa, jnp.where(
        lane == 3, wb, 0.0))))
    hn_ref[:, :D_MODEL] = hn
    hn_ref[:, D_MODEL:] = info
    info_ref[...] = info
    cnt_ref[...] = jnp.broadcast_to(carry_ref[...], cnt_ref.shape)


def _mla_out_kernel(h_ref, a_ref, wo_ref, gffn_ref, wrh_ref, wrl_ref, br_ref, tri_ref,
                    hout_ref, hn_ref, info_ref, cnt_ref, carry_ref):
    @pl.when(pl.program_id(0) == 0)
    def _():
        carry_ref[...] = jnp.zeros_like(carry_ref)

    hnew = h_ref[...] + jnp.dot(a_ref[...], wo_ref[...], preferred_element_type=F32)
    hout_ref[...] = hnew
    _route(hnew, gffn_ref, wrh_ref, wrl_ref, br_ref, tri_ref, hn_ref, info_ref, cnt_ref, carry_ref)


def _gla_out_kernel(h_ref, a_ref, r_ref, gout_ref, wo_ref, gffn_ref, wrh_ref, wrl_ref, br_ref, tri_ref,
                    hout_ref, hn_ref, info_ref, cnt_ref, carry_ref):
    @pl.when(pl.program_id(0) == 0)
    def _():
        carry_ref[...] = jnp.zeros_like(carry_ref)

    o = a_ref[...].astype(F32)
    r = r_ref[...].astype(F32)
    gout = gout_ref[...]
    parts = []
    for hh in range(GLA_HEADS):
        oh = o[:, GLA_DV * hh:GLA_DV * (hh + 1)]
        parts.append(oh * lax.rsqrt(jnp.mean(oh * oh, axis=-1, keepdims=True) + EPS) * gout)
    a = (jnp.concatenate(parts, axis=1) * (r * jax.nn.sigmoid(r))).astype(BF16)
    hnew = h_ref[...] + jnp.dot(a, wo_ref[...], preferred_element_type=F32)
    hout_ref[...] = hnew
    _route(hnew, gffn_ref, wrh_ref, wrl_ref, br_ref, tri_ref, hn_ref, info_ref, cnt_ref, carry_ref)


def _mixer_out(h, a, wo, gffn, wrh, wrl, br, tri, gla_extra=None):
    T = h.shape[0]
    tt = TOK_TILE
    tile = lambda w: pl.BlockSpec((tt, w), lambda i: (i, 0))
    const = lambda shape: pl.BlockSpec(shape, lambda i: (0,) * len(shape))
    if gla_extra is None:
        kern = _mla_out_kernel
        ins = [h, a]
        in_specs = [tile(D_MODEL), tile(a.shape[1])]
    else:
        r, gout = gla_extra
        kern = _gla_out_kernel
        ins = [h, a, r, gout]
        in_specs = [tile(D_MODEL), tile(a.shape[1]), tile(r.shape[1]), const(gout.shape)]
    ins += [wo, gffn, wrh, wrl, br, tri]
    in_specs += [const(wo.shape), const(gffn.shape), const(wrh.shape), const(wrl.shape), const(br.shape),
                 const(tri.shape)]
    return pl.pallas_call(
        kern,
        grid=(T // tt,),
        in_specs=in_specs,
        out_specs=[tile(D_MODEL), tile(ROW_EXT), tile(LANES), pl.BlockSpec((8, N_CLASS_SLOTS), lambda i: (0, 0))],
        out_shape=[
            jax.ShapeDtypeStruct((T, D_MODEL), F32),
            jax.ShapeDtypeStruct((T, ROW_EXT), F32),
            jax.ShapeDtypeStruct((T, LANES), F32),
            jax.ShapeDtypeStruct((8, N_CLASS_SLOTS), F32),
        ],
        scratch_shapes=[pltpu.VMEM((1, N_CLASS_SLOTS), F32)],
        compiler_params=_cparams(("arbitrary",)),
        name="mixer_out_router",
    )(*ins)


def _group(d):
    return lax.shift_right_logical(d, SUBLANES.bit_length() - 1)


def _in_group(d):
    return d & (SUBLANES - 1)


def _dispatch_kernel(dest_ref, hn_ref, xs_in_ref, xs_ref, sem):
    del xs_in_ref
    ng = hn_ref.shape[0]

    def row_copy(g, j, d):
        return pltpu.make_async_copy(hn_ref.at[g, pl.ds(j, 1)],
                                     xs_ref.at[_group(d), pl.ds(_in_group(d), 1)], sem)

    def issue(g, c):
        for j in range(SUBLANES):
            row_copy(g, j, dest_ref[0, 0, g * SUBLANES + j]).start()
        return c

    lax.fori_loop(0, ng, issue, 0)

    def drain(r, c):
        row_copy(0, 0, 0).wait()
        return c

    lax.fori_loop(0, ng * SUBLANES, drain, 0, unroll=8)


def _dispatch(dest3, hn, xs0):
    T = hn.shape[0]
    tt = TOK_TILE
    P = xs0.shape[0]
    out = pl.pallas_call(
        _dispatch_kernel,
        grid=(T // tt,),
        in_specs=[
            pl.BlockSpec((1, 1, tt), lambda i: (i, 0, 0), memory_space=pltpu.SMEM),
            pl.BlockSpec((tt // SUBLANES, SUBLANES, ROW_EXT), lambda i: (i, 0, 0)),
            pl.BlockSpec(memory_space=pl.ANY),
        ],
        out_specs=pl.BlockSpec(memory_space=pl.ANY),
        out_shape=jax.ShapeDtypeStruct((P // SUBLANES, SUBLANES, ROW_EXT), xs0.dtype),
        scratch_shapes=[pltpu.SemaphoreType.DMA],
        input_output_aliases={2: 0},
        compiler_params=_cparams(("arbitrary",)),
        name="moe_dispatch",
    )(dest3, hn.reshape(T // SUBLANES, SUBLANES, ROW_EXT), xs0.reshape(P // SUBLANES, SUBLANES, ROW_EXT))
    return out.reshape(P, ROW_EXT)


def _expert_kernel(ea_ref, eb_ref, nv_ref, xs_ref, *refs):
    del ea_ref, eb_ref
    w_refs, y_ref = refs[:-1], refs[-1]
    tm = EXP_TILE
    valid = 2 * pl.program_id(0) < nv_ref[0]

    def hidden(gu, w):
        g = gu[:, :D_EXPERT]
        return (g * jax.nn.sigmoid(g) * gu[:, D_EXPERT:] * w).astype(BF16)

    @pl.when(valid)
    def _():
        for u in range(2):
            wgua_ref, wda_ref, wgub_ref, wdb_ref = w_refs[4 * u:4 * u + 4]
            rows = slice(u * tm, (u + 1) * tm)
            x = xs_ref[rows, :D_MODEL].astype(BF16)
            gua = jnp.dot(x, wgua_ref[0], preferred_element_type=F32)
            gub = jnp.dot(x, wgub_ref[0], preferred_element_type=F32)
            ha = hidden(gua, xs_ref[rows, D_MODEL + 2:D_MODEL + 3])
            hb = hidden(gub, xs_ref[rows, D_MODEL + 3:D_MODEL + 4])
            y_ref[rows, :] = (jnp.dot(ha, wda_ref[0], preferred_element_type=F32)
                              + jnp.dot(hb, wdb_ref[0], preferred_element_type=F32))

    @pl.when(jnp.logical_not(valid))
    def _():
        y_ref[...] = jnp.zeros_like(y_ref)


def _experts(tile_ea, tile_eb, nvalid, xs, wgu, wd):
    P = xs.shape[0]
    tm = EXP_TILE
    assert (P // tm) % 2 == 0
    w_specs = []
    for u in range(2):
        for table in (0, 1):
            idx = lambda i, ea, eb, nv, u=u, table=table: ((ea, eb)[table][2 * i + u], 0, 0)
            w_specs += [pl.BlockSpec((1, D_MODEL, 2 * D_EXPERT), idx), pl.BlockSpec((1, D_EXPERT, D_MODEL), idx)]
    return pl.pallas_call(
        _expert_kernel,
        grid_spec=pltpu.PrefetchScalarGridSpec(
            num_scalar_prefetch=3,
            grid=(P // (2 * tm),),
            in_specs=[pl.BlockSpec((2 * tm, ROW_EXT),
                                   lambda i, ea, eb, nv: (jnp.minimum(i, (nv[0] - 1) // 2), 0))] + w_specs,
            out_specs=pl.BlockSpec((2 * tm, D_MODEL), lambda i, ea, eb, nv: (i, 0)),
        ),
        out_shape=jax.ShapeDtypeStruct((P, D_MODEL), F32),
        compiler_params=_cparams(("arbitrary",)),
        name="moe_experts",
    )(tile_ea, tile_eb, nvalid, xs, *([wgu, wd] * 4))


def _combine_kernel(dest_ref, h_ref, y_ref, out_ref, ybuf, sem):
    ng = ybuf.shape[0]

    def row_copy(g, j, d):
        return pltpu.make_async_copy(y_ref.at[_group(d), pl.ds(_in_group(d), 1)],
                                     ybuf.at[g, pl.ds(j, 1)], sem)

    def issue(g, c):
        for j in range(SUBLANES):
            row_copy(g, j, dest_ref[0, 0, g * SUBLANES + j]).start()
        return c

    lax.fori_loop(0, ng, issue, 0)

    def drain(r, c):
        row_copy(0, 0, 0).wait()
        return c

    lax.fori_loop(0, ng * SUBLANES, drain, 0, unroll=8)
    out_ref[...] = h_ref[...] + ybuf[...].reshape(out_ref.shape)


def _combine(dest3, h, y):
    T = h.shape[0]
    tt = TOK_TILE
    P = y.shape[0]
    return pl.pallas_call(
        _combine_kernel,
        grid=(T // tt,),
        in_specs=[
            pl.BlockSpec((1, 1, tt), lambda i: (i, 0, 0), memory_space=pltpu.SMEM),
            pl.BlockSpec((tt, D_MODEL), lambda i: (i, 0)),
            pl.BlockSpec(memory_space=pl.ANY),
        ],
        out_specs=pl.BlockSpec((tt, D_MODEL), lambda i: (i, 0)),
        out_shape=jax.ShapeDtypeStruct((T, D_MODEL), F32),
        scratch_shapes=[pltpu.VMEM((tt // SUBLANES, SUBLANES, D_MODEL), F32), pltpu.SemaphoreType.DMA],
        compiler_params=_cparams(("arbitrary",)),
        name="moe_combine",
    )(dest3, h, y.reshape(P // SUBLANES, SUBLANES, D_MODEL))


def _combine_frames(dest, h, y, B, LK):
    tt = TOK_TILE
    P = y.shape[0]
    S = LK - META_BLOCK
    nj = S // tt
    dest3 = dest.reshape(B, LK)[:, META_BLOCK:].reshape(B * nj, 1, tt)
    return pl.pallas_call(
        _combine_kernel,
        grid=(B, nj),
        in_specs=[
            pl.BlockSpec((1, 1, tt), lambda b, j: (b * nj + j, 0, 0), memory_space=pltpu.SMEM),
            pl.BlockSpec((pl.Element(tt), pl.Element(D_MODEL)),
                         lambda b, j: (pl.multiple_of(b * LK + META_BLOCK + j * tt, LANES), 0)),
            pl.BlockSpec(memory_space=pl.ANY),
        ],
        out_specs=pl.BlockSpec((tt, D_MODEL), lambda b, j: (b * nj + j, 0)),
        out_shape=jax.ShapeDtypeStruct((B * S, D_MODEL), F32),
        scratch_shapes=[pltpu.VMEM((tt // SUBLANES, SUBLANES, D_MODEL), F32), pltpu.SemaphoreType.DMA],
        compiler_params=_cparams(("arbitrary", "arbitrary")),
        name="moe_combine_out",
    )(dest3, h, y.reshape(P // SUBLANES, SUBLANES, D_MODEL))


def _moe(h, hn_ext, info, cnt, wgu, wd, xs_init, frames_of=None):
    T = h.shape[0]
    tm = EXP_TILE
    ntiles = xs_init.shape[0] // tm
    counts = cnt[0].astype(jnp.int32)
    tiles_c = (counts + tm - 1) // tm
    cum = jnp.cumsum(tiles_c)
    start_row = ((cum - tiles_c) * tm).astype(jnp.int32)
    tile_class = jnp.minimum(
        jnp.sum(jnp.arange(ntiles, dtype=jnp.int32)[:, None] >= cum[None, :], axis=1), N_CLASS_SLOTS - 1
    ).astype(jnp.int32)
    group_base = (tile_class // CLASS_STRIDE) * EXPERTS_PER_GROUP
    tile_ea = group_base + (tile_class % CLASS_STRIDE) // EXPERTS_PER_GROUP
    tile_eb = group_base + tile_class % EXPERTS_PER_GROUP
    nvalid = cum[-1:].astype(jnp.int32)
    cr = info[:, 0:2].astype(jnp.int32)
    onehot = (cr[:, 0:1] == jnp.arange(N_CLASS_SLOTS, dtype=jnp.int32)[None, :]).astype(BF16)
    digits = jnp.stack([start_row // 256, start_row % 256], axis=1).astype(BF16)
    hl = jnp.dot(onehot, digits, preferred_element_type=F32).astype(jnp.int32)
    dest = hl[:, 0] * 256 + hl[:, 1] + cr[:, 1]
    dest3 = dest.reshape(T // TOK_TILE, 1, TOK_TILE)
    xs = _dispatch(dest3, hn_ext, xs_init)
    y = _experts(tile_ea, tile_eb, nvalid, xs, wgu, wd)
    if frames_of is not None:
        return _combine_frames(dest, h, y, *frames_of), xs
    return _combine(dest3, h, y), xs


def _gla_in_kernel(h_ref, w_ref, wgu_ref, bg_ref, q_ref, k_ref, g_ref, v_ref, r_ref):
    h = h_ref[...]
    hn = (h * lax.rsqrt(jnp.mean(h * h, axis=-1, keepdims=True) + EPS)).astype(BF16)
    z = jnp.dot(hn, w_ref[...], preferred_element_type=F32)
    nk = GLA_HEADS * GLA_DK
    nv = GLA_HEADS * GLA_DV
    o_v = 2 * nk
    o_g = o_v + nv
    o_r = o_g + LANES
    xg = jnp.dot(z[:, o_g:o_r].astype(BF16), wgu_ref[...], preferred_element_type=F32) + bg_ref[...]
    log_a = (jnp.minimum(xg, 0.0) - jnp.log(1.0 + jnp.exp(-jnp.abs(xg)))) * (1.0 / GLA_TAU)
    for hh in range(GLA_HEADS):
        q_ref[hh] = z[:, GLA_DK * hh:GLA_DK * (hh + 1)].astype(BF16)
        k_ref[hh] = z[:, nk + GLA_DK * hh:nk + GLA_DK * (hh + 1)].astype(BF16)
        g_ref[hh] = log_a[:, GLA_DK * hh:GLA_DK * (hh + 1)]
        v_ref[hh] = z[:, o_v + GLA_DV * hh:o_v + GLA_DV * (hh + 1)].astype(BF16)
    r_ref[...] = z[:, o_r:].astype(BF16)


def _gla_in(h, w, wgu, bg):
    T = h.shape[0]
    tt = TOK_TILE
    GH = GLA_HEADS
    const = lambda shape: pl.BlockSpec(shape, lambda i: (0,) * len(shape))
    hspec = lambda w_: pl.BlockSpec((GH, tt, w_), lambda i: (0, i, 0))
    return pl.pallas_call(
        _gla_in_kernel,
        grid=(T // tt,),
        in_specs=[pl.BlockSpec((tt, D_MODEL), lambda i: (i, 0)), const(w.shape), const(wgu.shape), const(bg.shape)],
        out_specs=[hspec(GLA_DK), hspec(GLA_DK), hspec(GLA_DK), hspec(GLA_DV),
                   pl.BlockSpec((tt, GH * GLA_DV), lambda i: (i, 0))],
        out_shape=[
            jax.ShapeDtypeStruct((GH, T, GLA_DK), BF16),
            jax.ShapeDtypeStruct((GH, T, GLA_DK), BF16),
            jax.ShapeDtypeStruct((GH, T, GLA_DK), F32),
            jax.ShapeDtypeStruct((GH, T, GLA_DV), BF16),
            jax.ShapeDtypeStruct((T, GH * GLA_DV), BF16),
        ],
        compiler_params=_cparams(("arbitrary",)),
        name="gla_in",
    )(h, w, wgu, bg)


def _bdot(a, b, contract_b):
    return lax.dot_general(a, b, (((2,), (contract_b,)), ((0,), (0,))), preferred_element_type=F32)


def _gla_scan_kernel(q_ref, k_ref, g_ref, v_ref, stack_ref, o_ref, st_ref, oin_ref, qe_ref, kd_ref, dec_ref, *,
                     tile):
    C = SCAN_CHUNK
    nc = tile // C
    t = pl.program_id(1)

    @pl.when(t == 0)
    def _():
        st_ref[...] = jnp.zeros_like(st_ref)

    row = lax.broadcasted_iota(jnp.int32, (C, C), 0)
    col = lax.broadcasted_iota(jnp.int32, (C, C), 1)
    diag = (row == col)[None]
    masks = [(((row // (2 * c)) == (col // (2 * c))) & ((row % (2 * c)) >= c) & ((col % (2 * c)) < c))[None]
             for c in GLA_LEVELS]
    real = (t * tile + lax.broadcasted_iota(jnp.int32, (tile, 1), 0) >= PAD_ROWS).reshape(nc, C, 1)
    stack = stack_ref[...]

    for hh in range(GLA_HEADS):
        q = q_ref[hh].astype(F32).reshape(nc, C, GLA_DK)
        k = jnp.where(real, k_ref[hh].astype(F32).reshape(nc, C, GLA_DK), 0.0)
        g = g_ref[hh].reshape(nc, C, GLA_DK)
        v = v_ref[hh].reshape(nc, C, GLA_DV)
        g_hi = g.astype(BF16)
        g_lo = (g - g_hi.astype(F32)).astype(BF16)
        sums2 = _bdot(stack, jnp.concatenate([g_hi, g_lo], axis=2), 1)
        sums = sums2[:, :, :GLA_DK] + sums2[:, :, GLA_DK:]
        b = sums[:, 0:C]
        b_last = b[:, C - 1:C]
        att = jnp.where(diag, _bdot(q.astype(BF16), k.astype(BF16), 2), 0.0)
        for li in range(len(GLA_LEVELS)):
            ref = sums[:, C * (li + 1):C * (li + 2)]
            qf = (q * jnp.exp(jnp.minimum(b - ref, 0.0))).astype(BF16)
            kb = (k * jnp.exp(jnp.minimum(ref - b, 0.0))).astype(BF16)
            att = att + jnp.where(masks[li], _bdot(qf, kb, 2), 0.0)
        oin_ref[hh] = _bdot(att.astype(BF16), v, 1).reshape(tile, GLA_DV)
        qe_ref[hh] = (q * jnp.exp(b)).astype(BF16).reshape(tile, GLA_DK)
        kd_ref[hh] = (k * jnp.exp(b_last - b)).astype(BF16).reshape(tile, GLA_DK)
        dec_ref[hh] = jnp.broadcast_to(jnp.exp(b_last), (nc, 8, GLA_DK))

    def chunk(ci, carry):
        r0 = pl.multiple_of(ci * C, C)
        for hh in range(GLA_HEADS):
            st = st_ref[hh]
            o = oin_ref[hh, pl.ds(r0, C), :] + _nt_dot(qe_ref[hh, pl.ds(r0, C), :], st.astype(BF16))
            o_ref[pl.ds(r0, C), GLA_DV * hh:GLA_DV * (hh + 1)] = o.astype(BF16)
            st_ref[hh] = st * dec_ref[hh, ci][0:1] + _tn_dot(v_ref[hh, pl.ds(r0, C), :], kd_ref[hh, pl.ds(r0, C), :])
        return carry

    lax.fori_loop(0, nc, chunk, 0)


def _gla_scan(q, k, g, v, stack, B, LK):
    GH, T, _ = q.shape
    tile = 640 if LK % 640 == 0 else LANES
    nt = LK // tile
    nc = tile // SCAN_CHUNK
    stack3 = jnp.broadcast_to(stack[None], (nc,) + stack.shape)
    hspec = lambda w_: pl.BlockSpec((GH, tile, w_), lambda b, t: (0, b * nt + t, 0))
    return pl.pallas_call(
        functools.partial(_gla_scan_kernel, tile=tile),
        grid=(B, nt),
        in_specs=[hspec(GLA_DK), hspec(GLA_DK), hspec(GLA_DK), hspec(GLA_DV),
                  pl.BlockSpec(stack3.shape, lambda b, t: (0, 0, 0))],
        out_specs=pl.BlockSpec((tile, GH * GLA_DV), lambda b, t: (b * nt + t, 0)),
        out_shape=jax.ShapeDtypeStruct((T, GH * GLA_DV), BF16),
        scratch_shapes=[
            pltpu.VMEM((GH, GLA_DV, GLA_DK), F32),
            pltpu.VMEM((GH, tile, GLA_DV), F32),
            pltpu.VMEM((GH, tile, GLA_DK), BF16),
            pltpu.VMEM((GH, tile, GLA_DK), BF16),
            pltpu.VMEM((GH, nc, 8, GLA_DK), F32),
        ],
        compiler_params=_cparams(("arbitrary", "arbitrary")),
        name="gla_scan",
    )(q, k, g, v, stack3)


def _gla_stack():
    C = SCAN_CHUNK
    t = jnp.arange(C)[:, None]
    u = jnp.arange(C)[None, :]
    mats = [u <= t]
    for c in GLA_LEVELS:
        boundary = (t // (2 * c)) * (2 * c) + c - 1
        mats.append(u <= boundary)
    return jnp.concatenate(mats, axis=0).astype(BF16)


def _router_weights(w_rg, b_rg, w_re, b_re):
    w = jnp.zeros((D_MODEL, LANES), F32).at[:, :N_EXPERTS].set(w_re).at[:, N_EXPERTS:N_EXPERTS + N_GROUPS].set(w_rg)
    b = jnp.zeros((1, LANES), F32).at[0, :N_EXPERTS].set(b_re).at[0, N_EXPERTS:N_EXPERTS + N_GROUPS].set(b_rg)
    w_hi = w.astype(BF16)
    w_lo = (w - w_hi.astype(F32)).astype(BF16)
    return w_hi, w_lo, b


def kernel(x, meta_tokens, norm_mix, norm_ffn, mla_w_in, mla_g_q, mla_w_uq, mla_g_kv, mla_w_ukv, mla_g_qn, mla_g_kn, mla_w_o, gla_w_in, gla_w_gate_up, gla_b_gate, gla_g_out, gla_w_o, moe_w_rg, moe_b_rg, moe_w_re, moe_b_re, moe_w_gate, moe_w_up, moe_w_down):
    B, S, D = x.shape
    assert D == D_MODEL and S % (2 * ATT_TILE) == 0
    LK = META_BLOCK + S
    T = B * LK
    assert T % TOK_TILE == 0 and TOK_TILE % LANES == 0
    tt = TOK_TILE
    H = MLA_HEADS

    meta = jnp.broadcast_to(meta_tokens.astype(F32)[None], (B, N_META, D))
    h = jnp.pad(x.astype(F32), ((0, 0), (META_BLOCK, 0), (0, 0))).at[:, PAD_ROWS:META_BLOCK].set(meta)
    h = h.reshape(T, D)

    rows = jnp.arange(LK)
    pos = jnp.where(rows < META_BLOCK, jnp.maximum(rows - PAD_ROWS, 0), rows - META_BLOCK + N_META)
    half = MLA_ROPE // 2
    inv = 1.0 / (ROPE_THETA ** (jnp.arange(half, dtype=F32) / half))
    ang = inv[:, None] * pos.astype(F32)[None, :]
    cos = jnp.tile(jnp.cos(ang), (1, B))
    sin = jnp.tile(jnp.sin(ang), (1, B))

    tri = (jnp.arange(tt)[None, :] < jnp.arange(tt)[:, None]).astype(BF16)

    winT = (mla_w_in[0] * norm_mix[0][:, None]).T.astype(BF16)
    wuqT = (mla_w_uq[0] * mla_g_q[0][:, None]).T.astype(BF16)
    wukv = (mla_w_ukv[0] * mla_g_kv[0][:, None]).reshape(MLA_KV_RANK, H, MLA_NOPE + MLA_V)
    wkT = wukv[:, :, :MLA_NOPE].reshape(MLA_KV_RANK, H * MLA_NOPE).T.astype(BF16)
    wv = jnp.zeros((MLA_KV_RANK, H, LANES), F32).at[:, :, :MLA_V].set(wukv[:, :, MLA_NOPE:])
    wv = wv.reshape(MLA_KV_RANK, H * LANES).astype(BF16)
    vone = jnp.zeros((1, H, LANES), F32).at[:, :, MLA_V].set(1.0).reshape(1, H * LANES)
    gq = jnp.broadcast_to((mla_g_qn[0] * (MLA_QK ** -0.5 * LOG2E))[:, None], (MLA_QK, tt)).astype(F32)
    gk = jnp.broadcast_to(mla_g_kn[0][:, None], (MLA_QK, tt)).astype(F32)
    q, kT, v, stats = _mla_proj(h, winT, wuqT, wkT, wv, vone, gq, gk, cos, sin)
    att = _attention(q, kT, v, stats, B, LK)

    wrh, wrl, br = _router_weights(moe_w_rg[0], moe_b_rg[0], moe_w_re[0], moe_b_re[0])
    h, hn, info, cnt = _mixer_out(h, att, mla_w_o[0].astype(BF16), norm_ffn[0][None, :], wrh, wrl, br, tri)
    wgu = jnp.concatenate([moe_w_gate[0], moe_w_up[0]], axis=-1).astype(BF16)
    xs0 = jnp.zeros(((T // EXP_TILE + N_CLASSES) * EXP_TILE, ROW_EXT), F32)
    h, xs1 = _moe(h, hn, info, cnt, wgu, moe_w_down[0].astype(BF16), xs0)

    nk = GLA_HEADS * GLA_DK
    nv = GLA_HEADS * GLA_DV
    w1 = gla_w_in[0] * norm_mix[1][:, None]
    wg_pad = jnp.zeros((D, LANES), F32).at[:, :GLA_GATE_RANK].set(w1[:, 2 * nk + nv:2 * nk + nv + GLA_GATE_RANK])
    w_all = jnp.concatenate([w1[:, :nk] * (GLA_DK ** -0.5), w1[:, nk:2 * nk + nv], wg_pad,
                             w1[:, 2 * nk + nv + GLA_GATE_RANK:]], axis=1).astype(BF16)
    wgate = jnp.zeros((LANES, nk), F32).at[:GLA_GATE_RANK].set(gla_w_gate_up[0]).astype(BF16)
    gq_, gk_, gg_, gv_, gr_ = _gla_in(h, w_all, wgate, gla_b_gate[0][None, :])
    go = _gla_scan(gq_, gk_, gg_, gv_, _gla_stack(), B, LK)

    wrh, wrl, br = _router_weights(moe_w_rg[1], moe_b_rg[1], moe_w_re[1], moe_b_re[1])
    h, hn, info, cnt = _mixer_out(h, go, gla_w_o[0].astype(BF16), norm_ffn[1][None, :], wrh, wrl, br, tri,
                                  gla_extra=(gr_, gla_g_out[0][None, :]))
    wgu = jnp.concatenate([moe_w_gate[1], moe_w_up[1]], axis=-1).astype(BF16)
    out, _ = _moe(h, hn, info, cnt, wgu, moe_w_down[1].astype(BF16), xs1, frames_of=(B, LK))
    return out.reshape(B, S, D).astype(x.dtype)
```

```python
import functools

import jax
import jax.numpy as jnp
from jax import lax
from jax.experimental import pallas as pl
from jax.experimental.pallas import tpu as pltpu

F32 = jnp.float32
BF16 = jnp.bfloat16

D_MODEL = 1024
CHUNK = 64
N_META = 16
MLA_HEADS = 16
MLA_Q_RANK = 384
MLA_KV_RANK = 256
MLA_NOPE = 64
MLA_ROPE = 32
MLA_V = 64
MLA_QK = MLA_NOPE + MLA_ROPE
ROPE_THETA = 10000.0
GLA_HEADS = 4
GLA_DK = 128
GLA_DV = 256
GLA_GATE_RANK = 16
GLA_TAU = 16.0
N_GROUPS = 4
EXPERTS_PER_GROUP = 8
N_EXPERTS = N_GROUPS * EXPERTS_PER_GROUP
D_EXPERT = 256
EPS = 1e-6

LANES = 128
SUBLANES = 8
META_BLOCK = 128
PAD_ROWS = META_BLOCK - N_META
NEG = -1e30
LOG2E = 1.4426950408889634

TOK_TILE = 512
ATT_TILE = 512
ATT_SAFE_BOUND = 50.0
SCAN_CHUNK = 64
EXP_TILE = 128
EXP_GROUP = 4
CLASS_STRIDE = EXPERTS_PER_GROUP * EXPERTS_PER_GROUP
N_CLASS_SLOTS = N_GROUPS * CLASS_STRIDE
N_CLASSES = N_GROUPS * (EXPERTS_PER_GROUP * (EXPERTS_PER_GROUP - 1) // 2)
ROW_EXT = D_MODEL + LANES
VMEM_LIMIT = 56 * 1024 * 1024

GLA_LEVELS = (32, 16, 8, 4, 2, 1)


def _cparams(sem):
    return pltpu.CompilerParams(dimension_semantics=sem, vmem_limit_bytes=VMEM_LIMIT)


def _nt_dot(a, b):
    return lax.dot_general(a, b, (((1,), (1,)), ((), ())), preferred_element_type=F32)


def _tn_dot(a, b):
    return lax.dot_general(a, b, (((0,), (0,)), ((), ())), preferred_element_type=F32)


def _mla_proj_kernel(h_ref, winT_ref, wuqT_ref, wkT_ref, wv_ref, vone_ref, gq_ref, gk_ref,
                     cos_ref, sin_ref, q_ref, kT_ref, v_ref, stats_ref):
    tt = h_ref.shape[0]
    h = h_ref[...]
    hn = (h * lax.rsqrt(jnp.mean(h * h, axis=-1, keepdims=True) + EPS)).astype(BF16)
    zT = _nt_dot(winT_ref[...], hn)
    cq = zT[0:MLA_Q_RANK]
    ckv = zT[MLA_Q_RANK:MLA_Q_RANK + MLA_KV_RANK]
    kr = zT[MLA_Q_RANK + MLA_KV_RANK:]
    cqn = (cq * lax.rsqrt(jnp.mean(cq * cq, axis=0, keepdims=True) + EPS)).astype(BF16)
    ckvn = ckv * lax.rsqrt(jnp.mean(ckv * ckv, axis=0, keepdims=True) + EPS)
    qT = jnp.dot(wuqT_ref[...], cqn, preferred_element_type=F32)
    kT = jnp.dot(wkT_ref[...], ckvn.astype(BF16), preferred_element_type=F32)
    v = jnp.dot(ckvn.T.astype(BF16), wv_ref[...], preferred_element_type=F32) + vone_ref[...]
    cos = cos_ref[...]
    sin = sin_ref[...]
    gq = gq_ref[...]
    gk = gk_ref[...]
    half = MLA_ROPE // 2
    zpad = jnp.zeros((LANES - MLA_QK - 1, tt), F32)
    kpad = jnp.concatenate([jnp.ones((1, tt), F32), zpad], axis=0)
    spad = jnp.zeros((6, tt), F32)
    kr_ss = jnp.sum(kr * kr, axis=0, keepdims=True)
    for hh in range(MLA_HEADS):
        qh = qT[MLA_QK * hh:MLA_QK * (hh + 1)]
        qn = qh * lax.rsqrt(jnp.sum(qh * qh, axis=0, keepdims=True) * (1.0 / MLA_QK) + EPS) * gq
        x1 = qn[MLA_NOPE:MLA_NOPE + half]
        x2 = qn[MLA_NOPE + half:]
        qmain = jnp.concatenate([qn[:MLA_NOPE], x1 * cos - x2 * sin, x2 * cos + x1 * sin], axis=0)
        qsq = jnp.sum(qmain * qmain, axis=0, keepdims=True)
        q_ref[hh] = jnp.concatenate([qmain, -jnp.sqrt(qsq), zpad], axis=0).T.astype(BF16)
        kn = kT[MLA_NOPE * hh:MLA_NOPE * (hh + 1)]
        rk = lax.rsqrt((jnp.sum(kn * kn, axis=0, keepdims=True) + kr_ss) * (1.0 / MLA_QK) + EPS)
        knn = kn * rk * gk[:MLA_NOPE]
        krn = kr * rk * gk[MLA_NOPE:]
        y1 = krn[:half]
        y2 = krn[half:]
        kmain = jnp.concatenate([knn, y1 * cos - y2 * sin, y2 * cos + y1 * sin], axis=0)
        stats_ref[hh] = jnp.concatenate([jnp.sum(kmain * kmain, axis=0, keepdims=True), qsq, spad], axis=0)
        kfull = jnp.concatenate([kmain, kpad], axis=0).astype(BF16)
        for u in range(tt // LANES):
            kT_ref[hh, u] = kfull[:, LANES * u:LANES * (u + 1)]
        v_ref[hh] = v[:, LANES * hh:LANES * (hh + 1)].astype(BF16)


def _mla_proj(h, winT, wuqT, wkT, wv, vone, gq, gk, cos, sin):
    T = h.shape[0]
    tt = TOK_TILE
    H = MLA_HEADS
    const = lambda shape: pl.BlockSpec(shape, lambda i: (0,) * len(shape))
    return pl.pallas_call(
        _mla_proj_kernel,
        grid=(T // tt,),
        in_specs=[
            pl.BlockSpec((tt, D_MODEL), lambda i: (i, 0)),
            const(winT.shape), const(wuqT.shape), const(wkT.shape), const(wv.shape), const(vone.shape),
            const(gq.shape), const(gk.shape),
            pl.BlockSpec((MLA_ROPE // 2, tt), lambda i: (0, i)),
            pl.BlockSpec((MLA_ROPE // 2, tt), lambda i: (0, i)),
        ],
        out_specs=[
            pl.BlockSpec((H, tt, LANES), lambda i: (0, i, 0)),
            pl.BlockSpec((H, tt // LANES, LANES, LANES), lambda i: (0, i, 0, 0)),
            pl.BlockSpec((H, tt, LANES), lambda i: (0, i, 0)),
            pl.BlockSpec((H, 8, tt), lambda i: (0, 0, i)),
        ],
        out_shape=[
            jax.ShapeDtypeStruct((H, T, LANES), BF16),
            jax.ShapeDtypeStruct((H, T // LANES, LANES, LANES), BF16),
            jax.ShapeDtypeStruct((H, T, LANES), BF16),
            jax.ShapeDtypeStruct((H, 8, T), F32),
        ],
        compiler_params=_cparams(("arbitrary",)),
        name="mla_proj",
    )(h, winT, wuqT, wkT, wv, vone, gq, gk, cos, sin)


def _attn_kernel(q_ref, kT_ref, v_ref, stats_ref, o_ref, *, nf):
    tq = ATT_TILE
    nsub = tq // LANES
    heads = range(2)
    row = lax.broadcasted_iota(jnp.int32, (tq, tq), 0)
    col = lax.broadcasted_iota(jnp.int32, (tq, tq), 1)
    diag_bias = jnp.where((col // CHUNK) <= (row // CHUNK), 0.0, NEG).astype(F32)
    col0 = lax.broadcasted_iota(jnp.int32, (1, LANES), 1)
    bias0 = jnp.where(col0 >= PAD_ROWS, 0.0, NEG).astype(F32)

    def ktile(hh, blk):
        return jnp.concatenate([kT_ref[hh, blk + u] for u in range(nsub)], axis=1)

    def vtile(hh, t):
        return v_ref[hh, pl.ds(pl.multiple_of(META_BLOCK + t * tq, LANES), tq), :]

    def finish(acc):
        return (acc[:, :MLA_V] / acc[:, MLA_V:MLA_V + 1]).astype(BF16)

    def put(r0, n, hh, acc):
        o_ref[pl.ds(r0, n), MLA_V * hh:MLA_V * (hh + 1)] = finish(acc)

    sq_max = [jnp.max(stats_ref[hh], axis=1, keepdims=True) for hh in heads]
    worst = jnp.sqrt(jnp.maximum(sq_max[0][0:1] * sq_max[0][1:2], sq_max[1][0:1] * sq_max[1][1:2]))

    def shifted():
        lane = lax.broadcasted_iota(jnp.int32, (1, LANES), 1)
        col_scale = [jnp.where(lane == MLA_QK, jnp.sqrt(sq_max[hh][0:1]), 1.0).astype(BF16) for hh in heads]

        def with_shift(q, hh):
            return q * col_scale[hh]

        def part(q, kt, vv, bias):
            s = jnp.dot(q, kt, preferred_element_type=F32)
            if bias is not None:
                s = s + bias
            return jnp.dot(jnp.exp2(s).astype(BF16), vv, preferred_element_type=F32)

        for hh in heads:
            q0 = with_shift(q_ref[hh, 0:META_BLOCK, :], hh)
            put(0, META_BLOCK, hh, part(q0, kT_ref[hh, 0], v_ref[hh, 0:META_BLOCK, :], bias0))

        chains = [(hh, r) for hh in heads for r in range(2)]

        def qpair(ii, carry):
            r0 = pl.multiple_of(META_BLOCK + 2 * ii * tq, LANES)
            qs = [with_shift(q_ref[hh, pl.ds(r0 + r * tq, tq), :], hh) for hh, r in chains]

            def kv(j, acc):
                c0 = pl.multiple_of(META_BLOCK + 2 * j * tq, LANES)
                kts = [jnp.concatenate([kT_ref[hh, 1 + 2 * j * nsub + u] for u in range(2 * nsub)], axis=1)
                       for hh in heads]
                vts = [v_ref[hh, pl.ds(c0, 2 * tq), :] for hh in heads]
                return tuple(acc[c] + part(qs[c], kts[hh], vts[hh], None) for c, (hh, r) in enumerate(chains))

            accs = lax.fori_loop(0, ii, kv, tuple(jnp.zeros((tq, LANES), F32) for _ in chains))
            for c, (hh, r) in enumerate(chains):
                acc = accs[c] + part(qs[c], kT_ref[hh, 0], v_ref[hh, 0:META_BLOCK, :], bias0)
                if r == 1:
                    acc = acc + part(qs[c], ktile(hh, 1 + 2 * ii * nsub), vtile(hh, 2 * ii), None)
                acc = acc + part(qs[c], ktile(hh, 1 + (2 * ii + r) * nsub), vtile(hh, 2 * ii + r), diag_bias)
                put(r0 + r * tq, tq, hh, acc)
            return carry

        lax.fori_loop(0, nf // 2, qpair, 0)

    def online():
        def step(q, kt, vv, m, acc, bias):
            s = jnp.dot(q, kt, preferred_element_type=F32)
            if bias is not None:
                s = s + bias
            m_new = jnp.maximum(m, jnp.max(s, axis=-1, keepdims=True))
            p = jnp.exp2(s - m_new)
            acc = jnp.exp2(m - m_new) * acc + jnp.dot(p.astype(BF16), vv, preferred_element_type=F32)
            return m_new, acc

        def first(q, hh, n):
            return step(q, kT_ref[hh, 0], v_ref[hh, 0:META_BLOCK, :], jnp.full((n, 1), NEG, F32),
                        jnp.zeros((n, LANES), F32), bias0)

        for hh in heads:
            put(0, META_BLOCK, hh, first(q_ref[hh, 0:META_BLOCK, :], hh, META_BLOCK)[1])

        def qtile(i, carry):
            r0 = pl.multiple_of(META_BLOCK + i * tq, LANES)
            qs = [q_ref[hh, pl.ds(r0, tq), :] for hh in heads]
            state = []
            for hh in heads:
                state += list(first(qs[hh], hh, tq))

            def kv(j, c):
                out = []
                for hh in heads:
                    out += list(step(qs[hh], ktile(hh, 1 + j * nsub), vtile(hh, j), c[2 * hh], c[2 * hh + 1], None))
                return tuple(out)

            state = lax.fori_loop(0, i, kv, tuple(state))
            for hh in heads:
                m, acc = step(qs[hh], ktile(hh, 1 + i * nsub), vtile(hh, i), state[2 * hh], state[2 * hh + 1],
                              diag_bias)
                put(r0, tq, hh, acc)
            return carry

        lax.fori_loop(0, nf, qtile, 0)

    lax.cond(worst[0, 0] <= ATT_SAFE_BOUND, shifted, online)


def _attention(q, kT, v, stats, B, LK):
    H, T, _ = q.shape
    nf = (LK - META_BLOCK) // ATT_TILE
    nblk = LK // LANES
    return pl.pallas_call(
        functools.partial(_attn_kernel, nf=nf),
        grid=(B, H // 2),
        in_specs=[
            pl.BlockSpec((2, LK, LANES), lambda b, hp: (hp, b, 0)),
            pl.BlockSpec((2, nblk, LANES, LANES), lambda b, hp: (hp, b, 0, 0)),
            pl.BlockSpec((2, LK, LANES), lambda b, hp: (hp, b, 0)),
            pl.BlockSpec((2, 8, LK), lambda b, hp: (hp, 0, b)),
        ],
        out_specs=pl.BlockSpec((LK, 2 * MLA_V), lambda b, hp: (b, hp)),
        out_shape=jax.ShapeDtypeStruct((T, H * MLA_V), BF16),
        compiler_params=_cparams(("arbitrary", "arbitrary")),
        name="mla_attention",
    )(q, kT, v, stats)


def _route(hnew, gffn_ref, wrh_ref, wrl_ref, br_ref, tri_ref, hn_ref, info_ref, cnt_ref, carry_ref):
    tt = hnew.shape[0]
    hn = hnew * lax.rsqrt(jnp.mean(hnew * hnew, axis=-1, keepdims=True) + EPS) * gffn_ref[...]
    hn_hi = hn.astype(BF16)
    hn_lo = (hn - hn_hi.astype(F32)).astype(BF16)
    wrh = wrh_ref[...]
    both = jnp.dot(hn_hi, jnp.concatenate([wrh, wrl_ref[...]], axis=1), preferred_element_type=F32)
    logits = (both[:, :LANES] + both[:, LANES:]
              + jnp.dot(hn_lo, wrh, preferred_element_type=F32)) + br_ref[...]

    lane = lax.broadcasted_iota(jnp.int32, (tt, LANES), 1)
    lane_f = lane.astype(F32)
    big = float(LANES)
    gmask = (lane >= N_EXPERTS) & (lane < N_EXPERTS + N_GROUPS)
    gl = jnp.where(gmask, logits, NEG)
    gmax = jnp.max(gl, axis=-1, keepdims=True)
    gsel = jnp.min(jnp.where(gl == gmax, lane_f, big), axis=-1, keepdims=True) - float(N_EXPERTS)
    pg = 1.0 / jnp.sum(jnp.exp(gl - gmax), axis=-1, keepdims=True)
    egrp = (lane // EXPERTS_PER_GROUP).astype(F32)
    emask = (lane < N_EXPERTS) & (egrp == gsel)
    el = jnp.where(emask, logits, NEG)
    m1 = jnp.max(el, axis=-1, keepdims=True)
    i1 = jnp.min(jnp.where(el == m1, lane_f, big), axis=-1, keepdims=True)
    el2 = jnp.where(lane_f == i1, NEG, el)
    m2 = jnp.max(el2, axis=-1, keepdims=True)
    i2 = jnp.min(jnp.where(el2 == m2, lane_f, big), axis=-1, keepdims=True)
    t21 = jnp.exp(m2 - m1)
    w1 = pg / (1.0 + t21)
    w2 = w1 * t21

    lo = jnp.minimum(i1, i2)
    hi = jnp.maximum(i1, i2)
    first_is_lo = i1 < i2
    wa = jnp.where(first_is_lo, w1, w2)
    wb = jnp.where(first_is_lo, w2, w1)
    gbase = gsel * float(EXPERTS_PER_GROUP)
    cls = gsel * float(CLASS_STRIDE) + (lo - gbase) * float(EXPERTS_PER_GROUP) + (hi - gbase)
    lane2 = lax.broadcasted_iota(jnp.int32, (tt, N_CLASS_SLOTS), 1).astype(F32)
    sel = lane2 == cls
    oh = jnp.where(sel, 1.0, 0.0)
    before = jnp.dot(tri_ref[...], oh.astype(BF16), preferred_element_type=F32) + carry_ref[...]
    rank = jnp.sum(jnp.where(sel, before, 0.0), axis=-1, keepdims=True)
    carry_ref[...] = carry_ref[...] + jnp.sum(oh, axis=0, keepdims=True)
    info = jnp.where(lane == 0, cls, jnp.where(lane == 1, rank, jnp.where(lane == 2, wa, jnp.where(
        lane == 3, wb, 0.0))))
    hn_ref[:, :D_MODEL] = hn
    hn_ref[:, D_MODEL:] = info
    info_ref[...] = info
    cnt_ref[...] = jnp.broadcast_to(carry_ref[...], cnt_ref.shape)


def _mla_out_kernel(h_ref, a_ref, wo_ref, gffn_ref, wrh_ref, wrl_ref, br_ref, tri_ref,
                    hout_ref, hn_ref, info_ref, cnt_ref, carry_ref):
    @pl.when(pl.program_id(0) == 0)
    def _():
        carry_ref[...] = jnp.zeros_like(carry_ref)

    hnew = h_ref[...] + jnp.dot(a_ref[...], wo_ref[...], preferred_element_type=F32)
    hout_ref[...] = hnew
    _route(hnew, gffn_ref, wrh_ref, wrl_ref, br_ref, tri_ref, hn_ref, info_ref, cnt_ref, carry_ref)


def _gla_out_kernel(h_ref, a_ref, r_ref, gout_ref, wo_ref, gffn_ref, wrh_ref, wrl_ref, br_ref, tri_ref,
                    hout_ref, hn_ref, info_ref, cnt_ref, carry_ref):
    @pl.when(pl.program_id(0) == 0)
    def _():
        carry_ref[...] = jnp.zeros_like(carry_ref)

    o = a_ref[...].astype(F32)
    r = r_ref[...].astype(F32)
    gout = gout_ref[...]
    parts = []
    for hh in range(GLA_HEADS):
        oh = o[:, GLA_DV * hh:GLA_DV * (hh + 1)]
        parts.append(oh * lax.rsqrt(jnp.mean(oh * oh, axis=-1, keepdims=True) + EPS) * gout)
    a = (jnp.concatenate(parts, axis=1) * (r * jax.nn.sigmoid(r))).astype(BF16)
    hnew = h_ref[...] + jnp.dot(a, wo_ref[...], preferred_element_type=F32)
    hout_ref[...] = hnew
    _route(hnew, gffn_ref, wrh_ref, wrl_ref, br_ref, tri_ref, hn_ref, info_ref, cnt_ref, carry_ref)


def _mixer_out(h, a, wo, gffn, wrh, wrl, br, tri, gla_extra=None):
    T = h.shape[0]
    tt = TOK_TILE
    tile = lambda w: pl.BlockSpec((tt, w), lambda i: (i, 0))
    const = lambda shape: pl.BlockSpec(shape, lambda i: (0,) * len(shape))
    if gla_extra is None:
        kern = _mla_out_kernel
        ins = [h, a]
        in_specs = [tile(D_MODEL), tile(a.shape[1])]
    else:
        r, gout = gla_extra
        kern = _gla_out_kernel
        ins = [h, a, r, gout]
        in_specs = [tile(D_MODEL), tile(a.shape[1]), tile(r.shape[1]), const(gout.shape)]
    ins += [wo, gffn, wrh, wrl, br, tri]
    in_specs += [const(wo.shape), const(gffn.shape), const(wrh.shape), const(wrl.shape), const(br.shape),
                 const(tri.shape)]
    return pl.pallas_call(
        kern,
        grid=(T // tt,),
        in_specs=in_specs,
        out_specs=[tile(D_MODEL), tile(ROW_EXT), tile(LANES), pl.BlockSpec((8, N_CLASS_SLOTS), lambda i: (0, 0))],
        out_shape=[
            jax.ShapeDtypeStruct((T, D_MODEL), F32),
            jax.ShapeDtypeStruct((T, ROW_EXT), F32),
            jax.ShapeDtypeStruct((T, LANES), F32),
            jax.ShapeDtypeStruct((8, N_CLASS_SLOTS), F32),
        ],
        scratch_shapes=[pltpu.VMEM((1, N_CLASS_SLOTS), F32)],
        compiler_params=_cparams(("arbitrary",)),
        name="mixer_out_router",
    )(*ins)


def _group(d):
    return lax.shift_right_logical(d, SUBLANES.bit_length() - 1)


def _in_group(d):
    return d & (SUBLANES - 1)


def _dispatch_kernel(dest_ref, hn_ref, xs_in_ref, xs_ref, sem):
    del xs_in_ref
    ng = hn_ref.shape[0]

    def row_copy(g, j, d):
        return pltpu.make_async_copy(hn_ref.at[g, pl.ds(j, 1)],
                                     xs_ref.at[_group(d), pl.ds(_in_group(d), 1)], sem)

    def issue(g, c):
        for j in range(SUBLANES):
            row_copy(g, j, dest_ref[0, 0, g * SUBLANES + j]).start()
        return c

    lax.fori_loop(0, ng, issue, 0)

    def drain(r, c):
        row_copy(0, 0, 0).wait()
        return c

    lax.fori_loop(0, ng * SUBLANES, drain, 0, unroll=8)


def _dispatch(dest3, hn, xs0):
    T = hn.shape[0]
    tt = TOK_TILE
    P = xs0.shape[0]
    out = pl.pallas_call(
        _dispatch_kernel,
        grid=(T // tt,),
        in_specs=[
            pl.BlockSpec((1, 1, tt), lambda i: (i, 0, 0), memory_space=pltpu.SMEM),
            pl.BlockSpec((tt // SUBLANES, SUBLANES, ROW_EXT), lambda i: (i, 0, 0)),
            pl.BlockSpec(memory_space=pl.ANY),
        ],
        out_specs=pl.BlockSpec(memory_space=pl.ANY),
        out_shape=jax.ShapeDtypeStruct((P // SUBLANES, SUBLANES, ROW_EXT), xs0.dtype),
        scratch_shapes=[pltpu.SemaphoreType.DMA],
        input_output_aliases={2: 0},
        compiler_params=_cparams(("arbitrary",)),
        name="moe_dispatch",
    )(dest3, hn.reshape(T // SUBLANES, SUBLANES, ROW_EXT), xs0.reshape(P // SUBLANES, SUBLANES, ROW_EXT))
    return out.reshape(P, ROW_EXT)


def _expert_kernel(ea_ref, eb_ref, nv_ref, xs_ref, *refs):
    del ea_ref, eb_ref
    w_refs, y_ref = refs[:-1], refs[-1]
    tm = EXP_TILE
    valid = EXP_GROUP * pl.program_id(0) < nv_ref[0]

    def hidden(gu, w):
        g = gu[:, :D_EXPERT]
        return (g * jax.nn.sigmoid(g) * gu[:, D_EXPERT:] * w).astype(BF16)

    @pl.when(valid)
    def _():
        for u in range(EXP_GROUP):
            wgua_ref, wda_ref, wgub_ref, wdb_ref = w_refs[4 * u:4 * u + 4]
            rows = slice(u * tm, (u + 1) * tm)
            x = xs_ref[rows, :D_MODEL].astype(BF16)
            gua = jnp.dot(x, wgua_ref[0], preferred_element_type=F32)
            gub = jnp.dot(x, wgub_ref[0], preferred_element_type=F32)
            ha = hidden(gua, xs_ref[rows, D_MODEL + 2:D_MODEL + 3])
            hb = hidden(gub, xs_ref[rows, D_MODEL + 3:D_MODEL + 4])
            y_ref[rows, :] = (jnp.dot(ha, wda_ref[0], preferred_element_type=F32)
                              + jnp.dot(hb, wdb_ref[0], preferred_element_type=F32))

    @pl.when(jnp.logical_not(valid))
    def _():
        y_ref[...] = jnp.zeros_like(y_ref)


def _experts(tile_ea, tile_eb, nvalid, xs, wgu, wd):
    P = xs.shape[0]
    tm = EXP_TILE
    assert (P // tm) % EXP_GROUP == 0
    w_specs = []
    for u in range(EXP_GROUP):
        for table in (0, 1):
            idx = lambda i, ea, eb, nv, u=u, table=table: ((ea, eb)[table][EXP_GROUP * i + u], 0, 0)
            w_specs += [pl.BlockSpec((1, D_MODEL, 2 * D_EXPERT), idx), pl.BlockSpec((1, D_EXPERT, D_MODEL), idx)]
    return pl.pallas_call(
        _expert_kernel,
        grid_spec=pltpu.PrefetchScalarGridSpec(
            num_scalar_prefetch=3,
            grid=(P // (EXP_GROUP * tm),),
            in_specs=[pl.BlockSpec((EXP_GROUP * tm, ROW_EXT),
                                   lambda i, ea, eb, nv: (jnp.minimum(i, (nv[0] - 1) // EXP_GROUP), 0))] + w_specs,
            out_specs=pl.BlockSpec((EXP_GROUP * tm, D_MODEL), lambda i, ea, eb, nv: (i, 0)),
        ),
        out_shape=jax.ShapeDtypeStruct((P, D_MODEL), F32),
        compiler_params=_cparams(("arbitrary",)),
        name="moe_experts",
    )(tile_ea, tile_eb, nvalid, xs, *([wgu, wd] * (2 * EXP_GROUP)))


def _combine_kernel(dest_ref, h_ref, y_ref, out_ref, ybuf, sem):
    ng = ybuf.shape[0]

    def row_copy(g, j, d):
        return pltpu.make_async_copy(y_ref.at[_group(d), pl.ds(_in_group(d), 1)],
                                     ybuf.at[g, pl.ds(j, 1)], sem)

    def issue(g, c):
        for j in range(SUBLANES):
            row_copy(g, j, dest_ref[0, 0, g * SUBLANES + j]).start()
        return c

    lax.fori_loop(0, ng, issue, 0)

    def drain(r, c):
        row_copy(0, 0, 0).wait()
        return c

    lax.fori_loop(0, ng * SUBLANES, drain, 0, unroll=8)
    out_ref[...] = h_ref[...] + ybuf[...].reshape(out_ref.shape)


def _combine(dest3, h, y):
    T = h.shape[0]
    tt = TOK_TILE
    P = y.shape[0]
    return pl.pallas_call(
        _combine_kernel,
        grid=(T // tt,),
        in_specs=[
            pl.BlockSpec((1, 1, tt), lambda i: (i, 0, 0), memory_space=pltpu.SMEM),
            pl.BlockSpec((tt, D_MODEL), lambda i: (i, 0)),
            pl.BlockSpec(memory_space=pl.ANY),
        ],
        out_specs=pl.BlockSpec((tt, D_MODEL), lambda i: (i, 0)),
        out_shape=jax.ShapeDtypeStruct((T, D_MODEL), F32),
        scratch_shapes=[pltpu.VMEM((tt // SUBLANES, SUBLANES, D_MODEL), F32), pltpu.SemaphoreType.DMA],
        compiler_params=_cparams(("arbitrary",)),
        name="moe_combine",
    )(dest3, h, y.reshape(P // SUBLANES, SUBLANES, D_MODEL))


def _combine_frames(dest, h, y, B, LK):
    tt = TOK_TILE
    P = y.shape[0]
    S = LK - META_BLOCK
    nj = S // tt
    dest3 = dest.reshape(B, LK)[:, META_BLOCK:].reshape(B * nj, 1, tt)
    return pl.pallas_call(
        _combine_kernel,
        grid=(B, nj),
        in_specs=[
            pl.BlockSpec((1, 1, tt), lambda b, j: (b * nj + j, 0, 0), memory_space=pltpu.SMEM),
            pl.BlockSpec((pl.Element(tt), pl.Element(D_MODEL)),
                         lambda b, j: (pl.multiple_of(b * LK + META_BLOCK + j * tt, LANES), 0)),
            pl.BlockSpec(memory_space=pl.ANY),
        ],
        out_specs=pl.BlockSpec((tt, D_MODEL), lambda b, j: (b * nj + j, 0)),
        out_shape=jax.ShapeDtypeStruct((B * S, D_MODEL), F32),
        scratch_shapes=[pltpu.VMEM((tt // SUBLANES, SUBLANES, D_MODEL), F32), pltpu.SemaphoreType.DMA],
        compiler_params=_cparams(("arbitrary", "arbitrary")),
        name="moe_combine_out",
    )(dest3, h, y.reshape(P // SUBLANES, SUBLANES, D_MODEL))


def _moe(h, hn_ext, info, cnt, wgu, wd, xs_init, frames_of=None):
    T = h.shape[0]
    tm = EXP_TILE
    ntiles = xs_init.shape[0] // tm
    counts = cnt[0].astype(jnp.int32)
    tiles_c = (counts + tm - 1) // tm
    cum = jnp.cumsum(tiles_c)
    start_row = ((cum - tiles_c) * tm).astype(jnp.int32)
    tile_class = jnp.minimum(
        jnp.sum(jnp.arange(ntiles, dtype=jnp.int32)[:, None] >= cum[None, :], axis=1), N_CLASS_SLOTS - 1
    ).astype(jnp.int32)
    group_base = (tile_class // CLASS_STRIDE) * EXPERTS_PER_GROUP
    tile_ea = group_base + (tile_class % CLASS_STRIDE) // EXPERTS_PER_GROUP
    tile_eb = group_base + tile_class % EXPERTS_PER_GROUP
    nvalid = cum[-1:].astype(jnp.int32)
    cr = info[:, 0:2].astype(jnp.int32)
    onehot = (cr[:, 0:1] == jnp.arange(N_CLASS_SLOTS, dtype=jnp.int32)[None, :]).astype(BF16)
    digits = jnp.stack([start_row // 256, start_row % 256], axis=1).astype(BF16)
    hl = jnp.dot(onehot, digits, preferred_element_type=F32).astype(jnp.int32)
    dest = hl[:, 0] * 256 + hl[:, 1] + cr[:, 1]
    dest3 = dest.reshape(T // TOK_TILE, 1, TOK_TILE)
    xs = _dispatch(dest3, hn_ext, xs_init)
    y = _experts(tile_ea, tile_eb, nvalid, xs, wgu, wd)
    if frames_of is not None:
        return _combine_frames(dest, h, y, *frames_of), xs
    return _combine(dest3, h, y), xs


def _gla_in_kernel(h_ref, w_ref, wgu_ref, bg_ref, q_ref, k_ref, g_ref, v_ref, r_ref):
    h = h_ref[...]
    hn = (h * lax.rsqrt(jnp.mean(h * h, axis=-1, keepdims=True) + EPS)).astype(BF16)
    z = jnp.dot(hn, w_ref[...], preferred_element_type=F32)
    nk = GLA_HEADS * GLA_DK
    nv = GLA_HEADS * GLA_DV
    o_v = 2 * nk
    o_g = o_v + nv
    o_r = o_g + LANES
    xg = jnp.dot(z[:, o_g:o_r].astype(BF16), wgu_ref[...], preferred_element_type=F32) + bg_ref[...]
    log_a = (jnp.minimum(xg, 0.0) - jnp.log(1.0 + jnp.exp(-jnp.abs(xg)))) * (1.0 / GLA_TAU)
    for hh in range(GLA_HEADS):
        q_ref[hh] = z[:, GLA_DK * hh:GLA_DK * (hh + 1)].astype(BF16)
        k_ref[hh] = z[:, nk + GLA_DK * hh:nk + GLA_DK * (hh + 1)].astype(BF16)
        g_ref[hh] = log_a[:, GLA_DK * hh:GLA_DK * (hh + 1)]
        v_ref[hh] = z[:, o_v + GLA_DV * hh:o_v + GLA_DV * (hh + 1)].astype(BF16)
    r_ref[...] = z[:, o_r:].astype(BF16)


def _gla_in(h, w, wgu, bg):
    T = h.shape[0]
    tt = TOK_TILE
    GH = GLA_HEADS
    const = lambda shape: pl.BlockSpec(shape, lambda i: (0,) * len(shape))
    hspec = lambda w_: pl.BlockSpec((GH, tt, w_), lambda i: (0, i, 0))
    return pl.pallas_call(
        _gla_in_kernel,
        grid=(T // tt,),
        in_specs=[pl.BlockSpec((tt, D_MODEL), lambda i: (i, 0)), const(w.shape), const(wgu.shape), const(bg.shape)],
        out_specs=[hspec(GLA_DK), hspec(GLA_DK), hspec(GLA_DK), hspec(GLA_DV),
                   pl.BlockSpec((tt, GH * GLA_DV), lambda i: (i, 0))],
        out_shape=[
            jax.ShapeDtypeStruct((GH, T, GLA_DK), BF16),
            jax.ShapeDtypeStruct((GH, T, GLA_DK), BF16),
            jax.ShapeDtypeStruct((GH, T, GLA_DK), F32),
            jax.ShapeDtypeStruct((GH, T, GLA_DV), BF16),
            jax.ShapeDtypeStruct((T, GH * GLA_DV), BF16),
        ],
        compiler_params=_cparams(("arbitrary",)),
        name="gla_in",
    )(h, w, wgu, bg)


def _bdot(a, b, contract_b):
    return lax.dot_general(a, b, (((2,), (contract_b,)), ((0,), (0,))), preferred_element_type=F32)


def _gla_scan_kernel(q_ref, k_ref, g_ref, v_ref, stack_ref, o_ref, st_ref, oin_ref, qe_ref, kd_ref, dec_ref, *,
                     tile):
    C = SCAN_CHUNK
    nc = tile // C
    t = pl.program_id(1)

    @pl.when(t == 0)
    def _():
        st_ref[...] = jnp.zeros_like(st_ref)

    row = lax.broadcasted_iota(jnp.int32, (C, C), 0)
    col = lax.broadcasted_iota(jnp.int32, (C, C), 1)
    diag = (row == col)[None]
    masks = [(((row // (2 * c)) == (col // (2 * c))) & ((row % (2 * c)) >= c) & ((col % (2 * c)) < c))[None]
             for c in GLA_LEVELS]
    real = (t * tile + lax.broadcasted_iota(jnp.int32, (tile, 1), 0) >= PAD_ROWS).reshape(nc, C, 1)
    stack = stack_ref[...]

    for hh in range(GLA_HEADS):
        q = q_ref[hh].astype(F32).reshape(nc, C, GLA_DK)
        k = jnp.where(real, k_ref[hh].astype(F32).reshape(nc, C, GLA_DK), 0.0)
        g = g_ref[hh].reshape(nc, C, GLA_DK)
        v = v_ref[hh].reshape(nc, C, GLA_DV)
        g_hi = g.astype(BF16)
        g_lo = (g - g_hi.astype(F32)).astype(BF16)
        sums2 = _bdot(stack, jnp.concatenate([g_hi, g_lo], axis=2), 1)
        sums = sums2[:, :, :GLA_DK] + sums2[:, :, GLA_DK:]
        b = sums[:, 0:C]
        b_last = b[:, C - 1:C]
        att = jnp.where(diag, _bdot(q.astype(BF16), k.astype(BF16), 2), 0.0)
        for li in range(len(GLA_LEVELS)):
            ref = sums[:, C * (li + 1):C * (li + 2)]
            qf = (q * jnp.exp(jnp.minimum(b - ref, 0.0))).astype(BF16)
            kb = (k * jnp.exp(jnp.minimum(ref - b, 0.0))).astype(BF16)
            att = att + jnp.where(masks[li], _bdot(qf, kb, 2), 0.0)
        oin_ref[hh] = _bdot(att.astype(BF16), v, 1).reshape(tile, GLA_DV)
        qe_ref[hh] = (q * jnp.exp(b)).astype(BF16).reshape(tile, GLA_DK)
        kd_ref[hh] = (k * jnp.exp(b_last - b)).astype(BF16).reshape(tile, GLA_DK)
        dec_ref[hh] = jnp.broadcast_to(jnp.exp(b_last), (nc, 8, GLA_DK))

    def chunk(ci, carry):
        r0 = pl.multiple_of(ci * C, C)
        for hh in range(GLA_HEADS):
            st = st_ref[hh]
            o = oin_ref[hh, pl.ds(r0, C), :] + _nt_dot(qe_ref[hh, pl.ds(r0, C), :], st.astype(BF16))
            o_ref[pl.ds(r0, C), GLA_DV * hh:GLA_DV * (hh + 1)] = o.astype(BF16)
            st_ref[hh] = st * dec_ref[hh, ci][0:1] + _tn_dot(v_ref[hh, pl.ds(r0, C), :], kd_ref[hh, pl.ds(r0, C), :])
        return carry

    lax.fori_loop(0, nc, chunk, 0)


def _gla_scan(q, k, g, v, stack, B, LK):
    GH, T, _ = q.shape
    tile = 640 if LK % 640 == 0 else LANES
    nt = LK // tile
    nc = tile // SCAN_CHUNK
    stack3 = jnp.broadcast_to(stack[None], (nc,) + stack.shape)
    hspec = lambda w_: pl.BlockSpec((GH, tile, w_), lambda b, t: (0, b * nt + t, 0))
    return pl.pallas_call(
        functools.partial(_gla_scan_kernel, tile=tile),
        grid=(B, nt),
        in_specs=[hspec(GLA_DK), hspec(GLA_DK), hspec(GLA_DK), hspec(GLA_DV),
                  pl.BlockSpec(stack3.shape, lambda b, t: (0, 0, 0))],
        out_specs=pl.BlockSpec((tile, GH * GLA_DV), lambda b, t: (b * nt + t, 0)),
        out_shape=jax.ShapeDtypeStruct((T, GH * GLA_DV), BF16),
        scratch_shapes=[
            pltpu.VMEM((GH, GLA_DV, GLA_DK), F32),
            pltpu.VMEM((GH, tile, GLA_DV), F32),
            pltpu.VMEM((GH, tile, GLA_DK), BF16),
            pltpu.VMEM((GH, tile, GLA_DK), BF16),
            pltpu.VMEM((GH, nc, 8, GLA_DK), F32),
        ],
        compiler_params=_cparams(("arbitrary", "arbitrary")),
        name="gla_scan",
    )(q, k, g, v, stack3)


def _gla_stack():
    C = SCAN_CHUNK
    t = jnp.arange(C)[:, None]
    u = jnp.arange(C)[None, :]
    mats = [u <= t]
    for c in GLA_LEVELS:
        boundary = (t // (2 * c)) * (2 * c) + c - 1
        mats.append(u <= boundary)
    return jnp.concatenate(mats, axis=0).astype(BF16)


def _router_weights(w_rg, b_rg, w_re, b_re):
    w = jnp.zeros((D_MODEL, LANES), F32).at[:, :N_EXPERTS].set(w_re).at[:, N_EXPERTS:N_EXPERTS + N_GROUPS].set(w_rg)
    b = jnp.zeros((1, LANES), F32).at[0, :N_EXPERTS].set(b_re).at[0, N_EXPERTS:N_EXPERTS + N_GROUPS].set(b_rg)
    w_hi = w.astype(BF16)
    w_lo = (w - w_hi.astype(F32)).astype(BF16)
    return w_hi, w_lo, b


def kernel(x, meta_tokens, norm_mix, norm_ffn, mla_w_in, mla_g_q, mla_w_uq, mla_g_kv, mla_w_ukv, mla_g_qn, mla_g_kn, mla_w_o, gla_w_in, gla_w_gate_up, gla_b_gate, gla_g_out, gla_w_o, moe_w_rg, moe_b_rg, moe_w_re, moe_b_re, moe_w_gate, moe_w_up, moe_w_down):
    B, S, D = x.shape
    assert D == D_MODEL and S % (2 * ATT_TILE) == 0
    LK = META_BLOCK + S
    T = B * LK
    assert T % TOK_TILE == 0 and TOK_TILE % LANES == 0
    tt = TOK_TILE
    H = MLA_HEADS

    meta = jnp.broadcast_to(meta_tokens.astype(F32)[None], (B, N_META, D))
    h = jnp.pad(x.astype(F32), ((0, 0), (META_BLOCK, 0), (0, 0))).at[:, PAD_ROWS:META_BLOCK].set(meta)
    h = h.reshape(T, D)

    rows = jnp.arange(LK)
    pos = jnp.where(rows < META_BLOCK, jnp.maximum(rows - PAD_ROWS, 0), rows - META_BLOCK + N_META)
    half = MLA_ROPE // 2
    inv = 1.0 / (ROPE_THETA ** (jnp.arange(half, dtype=F32) / half))
    ang = inv[:, None] * pos.astype(F32)[None, :]
    cos = jnp.tile(jnp.cos(ang), (1, B))
    sin = jnp.tile(jnp.sin(ang), (1, B))

    tri = (jnp.arange(tt)[None, :] < jnp.arange(tt)[:, None]).astype(BF16)

    winT = (mla_w_in[0] * norm_mix[0][:, None]).T.astype(BF16)
    wuqT = (mla_w_uq[0] * mla_g_q[0][:, None]).T.astype(BF16)
    wukv = (mla_w_ukv[0] * mla_g_kv[0][:, None]).reshape(MLA_KV_RANK, H, MLA_NOPE + MLA_V)
    wkT = wukv[:, :, :MLA_NOPE].reshape(MLA_KV_RANK, H * MLA_NOPE).T.astype(BF16)
    wv = jnp.zeros((MLA_KV_RANK, H, LANES), F32).at[:, :, :MLA_V].set(wukv[:, :, MLA_NOPE:])
    wv = wv.reshape(MLA_KV_RANK, H * LANES).astype(BF16)
    vone = jnp.zeros((1, H, LANES), F32).at[:, :, MLA_V].set(1.0).reshape(1, H * LANES)
    gq = jnp.broadcast_to((mla_g_qn[0] * (MLA_QK ** -0.5 * LOG2E))[:, None], (MLA_QK, tt)).astype(F32)
    gk = jnp.broadcast_to(mla_g_kn[0][:, None], (MLA_QK, tt)).astype(F32)
    q, kT, v, stats = _mla_proj(h, winT, wuqT, wkT, wv, vone, gq, gk, cos, sin)
    att = _attention(q, kT, v, stats, B, LK)

    wrh, wrl, br = _router_weights(moe_w_rg[0], moe_b_rg[0], moe_w_re[0], moe_b_re[0])
    h, hn, info, cnt = _mixer_out(h, att, mla_w_o[0].astype(BF16), norm_ffn[0][None, :], wrh, wrl, br, tri)
    wgu = jnp.concatenate([moe_w_gate[0], moe_w_up[0]], axis=-1).astype(BF16)
    xs0 = jnp.zeros(((T // EXP_TILE + N_CLASSES) * EXP_TILE, ROW_EXT), F32)
    h, xs1 = _moe(h, hn, info, cnt, wgu, moe_w_down[0].astype(BF16), xs0)

    nk = GLA_HEADS * GLA_DK
    nv = GLA_HEADS * GLA_DV
    w1 = gla_w_in[0] * norm_mix[1][:, None]
    wg_pad = jnp.zeros((D, LANES), F32).at[:, :GLA_GATE_RANK].set(w1[:, 2 * nk + nv:2 * nk + nv + GLA_GATE_RANK])
    w_all = jnp.concatenate([w1[:, :nk] * (GLA_DK ** -0.5), w1[:, nk:2 * nk + nv], wg_pad,
                             w1[:, 2 * nk + nv + GLA_GATE_RANK:]], axis=1).astype(BF16)
    wgate = jnp.zeros((LANES, nk), F32).at[:GLA_GATE_RANK].set(gla_w_gate_up[0]).astype(BF16)
    gq_, gk_, gg_, gv_, gr_ = _gla_in(h, w_all, wgate, gla_b_gate[0][None, :])
    go = _gla_scan(gq_, gk_, gg_, gv_, _gla_stack(), B, LK)

    wrh, wrl, br = _router_weights(moe_w_rg[1], moe_b_rg[1], moe_w_re[1], moe_b_re[1])
    h, hn, info, cnt = _mixer_out(h, go, gla_w_o[0].astype(BF16), norm_ffn[1][None, :], wrh, wrl, br, tri,
                                  gla_extra=(gr_, gla_g_out[0][None, :]))
    wgu = jnp.concatenate([moe_w_gate[1], moe_w_up[1]], axis=-1).astype(BF16)
    out, _ = _moe(h, hn, info, cnt, wgu, moe_w_down[1].astype(BF16), xs1, frames_of=(B, LK))
    return out.reshape(B, S, D).astype(x.dtype)
```

```python
import functools

import jax
import jax.numpy as jnp
from jax import lax
from jax.experimental import pallas as pl
from jax.experimental.pallas import tpu as pltpu

F32 = jnp.float32
BF16 = jnp.bfloat16

D_MODEL = 1024
CHUNK = 64
N_META = 16
MLA_HEADS = 16
MLA_Q_RANK = 384
MLA_KV_RANK = 256
MLA_NOPE = 64
MLA_ROPE = 32
MLA_V = 64
MLA_QK = MLA_NOPE + MLA_ROPE
ROPE_THETA = 10000.0
GLA_HEADS = 4
GLA_DK = 128
GLA_DV = 256
GLA_GATE_RANK = 16
GLA_TAU = 16.0
N_GROUPS = 4
EXPERTS_PER_GROUP = 8
N_EXPERTS = N_GROUPS * EXPERTS_PER_GROUP
D_EXPERT = 256
EPS = 1e-6

LANES = 128
SUBLANES = 8
META_BLOCK = 128
PAD_ROWS = META_BLOCK - N_META
NEG = -1e30
LOG2E = 1.4426950408889634

TOK_TILE = 512
ATT_TILE = 512
ATT_SAFE_BOUND = 50.0
SCAN_CHUNK = 64
EXP_TILE = 128
EXP_GROUP = 4
CLASS_STRIDE = EXPERTS_PER_GROUP * EXPERTS_PER_GROUP
N_CLASS_SLOTS = N_GROUPS * CLASS_STRIDE
N_CLASSES = N_GROUPS * (EXPERTS_PER_GROUP * (EXPERTS_PER_GROUP - 1) // 2)
ROW_EXT = D_MODEL + LANES
VMEM_LIMIT = 56 * 1024 * 1024

GLA_LEVELS = (32, 16, 8, 4, 2, 1)


def _cparams(sem):
    return pltpu.CompilerParams(dimension_semantics=sem, vmem_limit_bytes=VMEM_LIMIT)


def _nt_dot(a, b):
    return lax.dot_general(a, b, (((1,), (1,)), ((), ())), preferred_element_type=F32)


def _tn_dot(a, b):
    return lax.dot_general(a, b, (((0,), (0,)), ((), ())), preferred_element_type=F32)


def _mla_proj_kernel(h_ref, winT_ref, wuqT_ref, wkT_ref, wv_ref, vone_ref, gq_ref, gk_ref,
                     cos_ref, sin_ref, q_ref, kT_ref, v_ref, stats_ref):
    tt = h_ref.shape[0]
    h = h_ref[...]
    hn = (h * lax.rsqrt(jnp.mean(h * h, axis=-1, keepdims=True) + EPS)).astype(BF16)
    zT = _nt_dot(winT_ref[...], hn)
    cq = zT[0:MLA_Q_RANK]
    ckv = zT[MLA_Q_RANK:MLA_Q_RANK + MLA_KV_RANK]
    kr = zT[MLA_Q_RANK + MLA_KV_RANK:]
    cqn = (cq * lax.rsqrt(jnp.mean(cq * cq, axis=0, keepdims=True) + EPS)).astype(BF16)
    ckvn = ckv * lax.rsqrt(jnp.mean(ckv * ckv, axis=0, keepdims=True) + EPS)
    qT = jnp.dot(wuqT_ref[...], cqn, preferred_element_type=F32)
    kT = jnp.dot(wkT_ref[...], ckvn.astype(BF16), preferred_element_type=F32)
    v = jnp.dot(ckvn.T.astype(BF16), wv_ref[...], preferred_element_type=F32) + vone_ref[...]
    cos = cos_ref[...]
    sin = sin_ref[...]
    gq = gq_ref[...]
    gk = gk_ref[...]
    half = MLA_ROPE // 2
    zpad = jnp.zeros((LANES - MLA_QK - 1, tt), F32)
    kpad = jnp.concatenate([jnp.ones((1, tt), F32), zpad], axis=0)
    spad = jnp.zeros((6, tt), F32)
    kr_ss = jnp.sum(kr * kr, axis=0, keepdims=True)
    for hh in range(MLA_HEADS):
        qh = qT[MLA_QK * hh:MLA_QK * (hh + 1)]
        qn = qh * lax.rsqrt(jnp.sum(qh * qh, axis=0, keepdims=True) * (1.0 / MLA_QK) + EPS) * gq
        x1 = qn[MLA_NOPE:MLA_NOPE + half]
        x2 = qn[MLA_NOPE + half:]
        qmain = jnp.concatenate([qn[:MLA_NOPE], x1 * cos - x2 * sin, x2 * cos + x1 * sin], axis=0)
        qsq = jnp.sum(qmain * qmain, axis=0, keepdims=True)
        q_ref[hh] = jnp.concatenate([qmain, -jnp.sqrt(qsq), zpad], axis=0).T.astype(BF16)
        kn = kT[MLA_NOPE * hh:MLA_NOPE * (hh + 1)]
        rk = lax.rsqrt((jnp.sum(kn * kn, axis=0, keepdims=True) + kr_ss) * (1.0 / MLA_QK) + EPS)
        knn = kn * rk * gk[:MLA_NOPE]
        krn = kr * rk * gk[MLA_NOPE:]
        y1 = krn[:half]
        y2 = krn[half:]
        kmain = jnp.concatenate([knn, y1 * cos - y2 * sin, y2 * cos + y1 * sin], axis=0)
        stats_ref[hh] = jnp.concatenate([jnp.sum(kmain * kmain, axis=0, keepdims=True), qsq, spad], axis=0)
        kfull = jnp.concatenate([kmain, kpad], axis=0).astype(BF16)
        for u in range(tt // LANES):
            kT_ref[hh, u] = kfull[:, LANES * u:LANES * (u + 1)]
        v_ref[hh] = v[:, LANES * hh:LANES * (hh + 1)].astype(BF16)


def _mla_proj(h, winT, wuqT, wkT, wv, vone, gq, gk, cos, sin):
    T = h.shape[0]
    tt = TOK_TILE
    H = MLA_HEADS
    const = lambda shape: pl.BlockSpec(shape, lambda i: (0,) * len(shape))
    return pl.pallas_call(
        _mla_proj_kernel,
        grid=(T // tt,),
        in_specs=[
            pl.BlockSpec((tt, D_MODEL), lambda i: (i, 0)),
            const(winT.shape), const(wuqT.shape), const(wkT.shape), const(wv.shape), const(vone.shape),
            const(gq.shape), const(gk.shape),
            pl.BlockSpec((MLA_ROPE // 2, tt), lambda i: (0, i)),
            pl.BlockSpec((MLA_ROPE // 2, tt), lambda i: (0, i)),
        ],
        out_specs=[
            pl.BlockSpec((H, tt, LANES), lambda i: (0, i, 0)),
            pl.BlockSpec((H, tt // LANES, LANES, LANES), lambda i: (0, i, 0, 0)),
            pl.BlockSpec((H, tt, LANES), lambda i: (0, i, 0)),
            pl.BlockSpec((H, 8, tt), lambda i: (0, 0, i)),
        ],
        out_shape=[
            jax.ShapeDtypeStruct((H, T, LANES), BF16),
            jax.ShapeDtypeStruct((H, T // LANES, LANES, LANES), BF16),
            jax.ShapeDtypeStruct((H, T, LANES), BF16),
            jax.ShapeDtypeStruct((H, 8, T), F32),
        ],
        compiler_params=_cparams(("arbitrary",)),
        name="mla_proj",
    )(h, winT, wuqT, wkT, wv, vone, gq, gk, cos, sin)


def _attn_kernel(q_ref, kT_ref, v_ref, stats_ref, o_ref, *, nf):
    tq = ATT_TILE
    nsub = tq // LANES
    heads = range(2)
    row = lax.broadcasted_iota(jnp.int32, (tq, tq), 0)
    col = lax.broadcasted_iota(jnp.int32, (tq, tq), 1)
    diag_bias = jnp.where((col // CHUNK) <= (row // CHUNK), 0.0, NEG).astype(F32)
    col0 = lax.broadcasted_iota(jnp.int32, (1, LANES), 1)
    bias0 = jnp.where(col0 >= PAD_ROWS, 0.0, NEG).astype(F32)

    def ktile(hh, blk):
        return jnp.concatenate([kT_ref[hh, blk + u] for u in range(nsub)], axis=1)

    def vtile(hh, t):
        return v_ref[hh, pl.ds(pl.multiple_of(META_BLOCK + t * tq, LANES), tq), :]

    def finish(acc):
        return (acc[:, :MLA_V] / acc[:, MLA_V:MLA_V + 1]).astype(BF16)

    def put(r0, n, hh, acc):
        o_ref[pl.ds(r0, n), MLA_V * hh:MLA_V * (hh + 1)] = finish(acc)

    sq_max = [jnp.max(stats_ref[hh], axis=1, keepdims=True) for hh in heads]
    worst = jnp.sqrt(jnp.maximum(sq_max[0][0:1] * sq_max[0][1:2], sq_max[1][0:1] * sq_max[1][1:2]))

    def shifted():
        lane = lax.broadcasted_iota(jnp.int32, (1, LANES), 1)
        col_scale = [jnp.where(lane == MLA_QK, jnp.sqrt(sq_max[hh][0:1]), 1.0).astype(BF16) for hh in heads]

        def with_shift(q, hh):
            return q * col_scale[hh]

        def part(q, kt, vv, bias):
            s = jnp.dot(q, kt, preferred_element_type=F32)
            if bias is not None:
                s = s + bias
            return jnp.dot(jnp.exp2(s).astype(BF16), vv, preferred_element_type=F32)

        for hh in heads:
            q0 = with_shift(q_ref[hh, 0:META_BLOCK, :], hh)
            put(0, META_BLOCK, hh, part(q0, kT_ref[hh, 0], v_ref[hh, 0:META_BLOCK, :], bias0))

        chains = [(hh, r) for hh in heads for r in range(2)]

        def qpair(ii, carry):
            r0 = pl.multiple_of(META_BLOCK + 2 * ii * tq, LANES)
            qs = [with_shift(q_ref[hh, pl.ds(r0 + r * tq, tq), :], hh) for hh, r in chains]

            def kv(j, acc):
                c0 = pl.multiple_of(META_BLOCK + 2 * j * tq, LANES)
                kts = [jnp.concatenate([kT_ref[hh, 1 + 2 * j * nsub + u] for u in range(2 * nsub)], axis=1)
                       for hh in heads]
                vts = [v_ref[hh, pl.ds(c0, 2 * tq), :] for hh in heads]
                return tuple(acc[c] + part(qs[c], kts[hh], vts[hh], None) for c, (hh, r) in enumerate(chains))

            accs = lax.fori_loop(0, ii, kv, tuple(jnp.zeros((tq, LANES), F32) for _ in chains))
            for c, (hh, r) in enumerate(chains):
                acc = accs[c] + part(qs[c], kT_ref[hh, 0], v_ref[hh, 0:META_BLOCK, :], bias0)
                if r == 1:
                    acc = acc + part(qs[c], ktile(hh, 1 + 2 * ii * nsub), vtile(hh, 2 * ii), None)
                acc = acc + part(qs[c], ktile(hh, 1 + (2 * ii + r) * nsub), vtile(hh, 2 * ii + r), diag_bias)
                put(r0 + r * tq, tq, hh, acc)
            return carry

        lax.fori_loop(0, nf // 2, qpair, 0)

    def online():
        def step(q, kt, vv, m, acc, bias):
            s = jnp.dot(q, kt, preferred_element_type=F32)
            if bias is not None:
                s = s + bias
            m_new = jnp.maximum(m, jnp.max(s, axis=-1, keepdims=True))
            p = jnp.exp2(s - m_new)
            acc = jnp.exp2(m - m_new) * acc + jnp.dot(p.astype(BF16), vv, preferred_element_type=F32)
            return m_new, acc

        def first(q, hh, n):
            return step(q, kT_ref[hh, 0], v_ref[hh, 0:META_BLOCK, :], jnp.full((n, 1), NEG, F32),
                        jnp.zeros((n, LANES), F32), bias0)

        for hh in heads:
            put(0, META_BLOCK, hh, first(q_ref[hh, 0:META_BLOCK, :], hh, META_BLOCK)[1])

        def qtile(i, carry):
            r0 = pl.multiple_of(META_BLOCK + i * tq, LANES)
            qs = [q_ref[hh, pl.ds(r0, tq), :] for hh in heads]
            state = []
            for hh in heads:
                state += list(first(qs[hh], hh, tq))

            def kv(j, c):
                out = []
                for hh in heads:
                    out += list(step(qs[hh], ktile(hh, 1 + j * nsub), vtile(hh, j), c[2 * hh], c[2 * hh + 1], None))
                return tuple(out)

            state = lax.fori_loop(0, i, kv, tuple(state))
            for hh in heads:
                m, acc = step(qs[hh], ktile(hh, 1 + i * nsub), vtile(hh, i), state[2 * hh], state[2 * hh + 1],
                              diag_bias)
                put(r0, tq, hh, acc)
            return carry

        lax.fori_loop(0, nf, qtile, 0)

    lax.cond(worst[0, 0] <= ATT_SAFE_BOUND, shifted, online)


def _attention(q, kT, v, stats, B, LK):
    H, T, _ = q.shape
    nf = (LK - META_BLOCK) // ATT_TILE
    nblk = LK // LANES
    return pl.pallas_call(
        functools.partial(_attn_kernel, nf=nf),
        grid=(B, H // 2),
        in_specs=[
            pl.BlockSpec((2, LK, LANES), lambda b, hp: (hp, b, 0)),
            pl.BlockSpec((2, nblk, LANES, LANES), lambda b, hp: (hp, b, 0, 0)),
            pl.BlockSpec((2, LK, LANES), lambda b, hp: (hp, b, 0)),
            pl.BlockSpec((2, 8, LK), lambda b, hp: (hp, 0, b)),
        ],
        out_specs=pl.BlockSpec((LK, 2 * MLA_V), lambda b, hp: (b, hp)),
        out_shape=jax.ShapeDtypeStruct((T, H * MLA_V), BF16),
        compiler_params=_cparams(("arbitrary", "arbitrary")),
        name="mla_attention",
    )(q, kT, v, stats)


def _route(hnew, gffn_ref, wrh_ref, wrl_ref, br_ref, tri_ref, hn_ref, info_ref, cnt_ref, carry_ref):
    tt = hnew.shape[0]
    hn = hnew * lax.rsqrt(jnp.mean(hnew * hnew, axis=-1, keepdims=True) + EPS) * gffn_ref[...]
    hn_hi = hn.astype(BF16)
    hn_lo = (hn - hn_hi.astype(F32)).astype(BF16)
    wrh = wrh_ref[...]
    both = jnp.dot(hn_hi, jnp.concatenate([wrh, wrl_ref[...]], axis=1), preferred_element_type=F32)
    logits = (both[:, :LANES] + both[:, LANES:]
              + jnp.dot(hn_lo, wrh, preferred_element_type=F32)) + br_ref[...]

    lane = lax.broadcasted_iota(jnp.int32, (tt, LANES), 1)
    lane_f = lane.astype(F32)
    big = float(LANES)
    gmask = (lane >= N_EXPERTS) & (lane < N_EXPERTS + N_GROUPS)
    gl = jnp.where(gmask, logits, NEG)
    gmax = jnp.max(gl, axis=-1, keepdims=True)
    gsel = jnp.min(jnp.where(gl == gmax, lane_f, big), axis=-1, keepdims=True) - float(N_EXPERTS)
    pg = 1.0 / jnp.sum(jnp.exp(gl - gmax), axis=-1, keepdims=True)
    egrp = (lane // EXPERTS_PER_GROUP).astype(F32)
    emask = (lane < N_EXPERTS) & (egrp == gsel)
    el = jnp.where(emask, logits, NEG)
    m1 = jnp.max(el, axis=-1, keepdims=True)
    i1 = jnp.min(jnp.where(el == m1, lane_f, big), axis=-1, keepdims=True)
    el2 = jnp.where(lane_f == i1, NEG, el)
    m2 = jnp.max(el2, axis=-1, keepdims=True)
    i2 = jnp.min(jnp.where(el2 == m2, lane_f, big), axis=-1, keepdims=True)
    t21 = jnp.exp(m2 - m1)
    w1 = pg / (1.0 + t21)
    w2 = w1 * t21

    lo = jnp.minimum(i1, i2)
    hi = jnp.maximum(i1, i2)
    first_is_lo = i1 < i2
    wa = jnp.where(first_is_lo, w1, w2)
    wb = jnp.where(first_is_lo, w2, w1)
    gbase = gsel * float(EXPERTS_PER_GROUP)
    cls = gsel * float(CLASS_STRIDE) + (lo - gbase) * float(EXPERTS_PER_GROUP) + (hi - gbase)
    lane2 = lax.broadcasted_iota(jnp.int32, (tt, N_CLASS_SLOTS), 1).astype(F32)
    sel = lane2 == cls
    oh = jnp.where(sel, 1.0, 0.0)
    before = jnp.dot(tri_ref[...], oh.astype(BF16), preferred_element_type=F32) + carry_ref[...]
    rank = jnp.sum(jnp.where(sel, before, 0.0), axis=-1, keepdims=True)
    carry_ref[...] = carry_ref[...] + jnp.sum(oh, axis=0, keepdims=True)
    info = jnp.where(lane == 0, cls, jnp.where(lane == 1, rank, jnp.where(lane == 2, wa, jnp.where(
        lane == 3, wb, 0.0))))
    hn_ref[:, :D_MODEL] = hn
    hn_ref[:, D_MODEL:] = info
    info_ref[...] = info
    cnt_ref[...] = jnp.broadcast_to(carry_ref[...], cnt_ref.shape)


def _mla_out_kernel(h_ref, a_ref, wo_ref, gffn_ref, wrh_ref, wrl_ref, br_ref, tri_ref,
                    hout_ref, hn_ref, info_ref, cnt_ref, carry_ref):
    @pl.when(pl.program_id(0) == 0)
    def _():
        carry_ref[...] = jnp.zeros_like(carry_ref)

    hnew = h_ref[...] + jnp.dot(a_ref[...], wo_ref[...], preferred_element_type=F32)
    hout_ref[...] = hnew
    _route(hnew, gffn_ref, wrh_ref, wrl_ref, br_ref, tri_ref, hn_ref, info_ref, cnt_ref, carry_ref)


def _gla_out_kernel(h_ref, a_ref, r_ref, gout_ref, wo_ref, gffn_ref, wrh_ref, wrl_ref, br_ref, tri_ref,
                    hout_ref, hn_ref, info_ref, cnt_ref, carry_ref):
    @pl.when(pl.program_id(0) == 0)
    def _():
        carry_ref[...] = jnp.zeros_like(carry_ref)

    o = a_ref[...].astype(F32)
    r = r_ref[...].astype(F32)
    gout = gout_ref[...]
    parts = []
    for hh in range(GLA_HEADS):
        oh = o[:, GLA_DV * hh:GLA_DV * (hh + 1)]
        parts.append(oh * lax.rsqrt(jnp.mean(oh * oh, axis=-1, keepdims=True) + EPS) * gout)
    a = (jnp.concatenate(parts, axis=1) * (r * jax.nn.sigmoid(r))).astype(BF16)
    hnew = h_ref[...] + jnp.dot(a, wo_ref[...], preferred_element_type=F32)
    hout_ref[...] = hnew
    _route(hnew, gffn_ref, wrh_ref, wrl_ref, br_ref, tri_ref, hn_ref, info_ref, cnt_ref, carry_ref)


def _mixer_out(h, a, wo, gffn, wrh, wrl, br, tri, gla_extra=None):
    T = h.shape[0]
    tt = TOK_TILE
    tile = lambda w: pl.BlockSpec((tt, w), lambda i: (i, 0))
    const = lambda shape: pl.BlockSpec(shape, lambda i: (0,) * len(shape))
    if gla_extra is None:
        kern = _mla_out_kernel
        ins = [h, a]
        in_specs = [tile(D_MODEL), tile(a.shape[1])]
    else:
        r, gout = gla_extra
        kern = _gla_out_kernel
        ins = [h, a, r, gout]
        in_specs = [tile(D_MODEL), tile(a.shape[1]), tile(r.shape[1]), const(gout.shape)]
    ins += [wo, gffn, wrh, wrl, br, tri]
    in_specs += [const(wo.shape), const(gffn.shape), const(wrh.shape), const(wrl.shape), const(br.shape),
                 const(tri.shape)]
    return pl.pallas_call(
        kern,
        grid=(T // tt,),
        in_specs=in_specs,
        out_specs=[tile(D_MODEL), tile(ROW_EXT), tile(LANES), pl.BlockSpec((8, N_CLASS_SLOTS), lambda i: (0, 0))],
        out_shape=[
            jax.ShapeDtypeStruct((T, D_MODEL), F32),
            jax.ShapeDtypeStruct((T, ROW_EXT), F32),
            jax.ShapeDtypeStruct((T, LANES), F32),
            jax.ShapeDtypeStruct((8, N_CLASS_SLOTS), F32),
        ],
        scratch_shapes=[pltpu.VMEM((1, N_CLASS_SLOTS), F32)],
        compiler_params=_cparams(("arbitrary",)),
        name="mixer_out_router",
    )(*ins)


def _group(d):
    return lax.shift_right_logical(d, SUBLANES.bit_length() - 1)


def _in_group(d):
    return d & (SUBLANES - 1)


def _dispatch_kernel(dest_ref, hn_ref, xs_in_ref, xs_ref, sem):
    del xs_in_ref
    ng = hn_ref.shape[0]

    def row_copy(g, j, d):
        return pltpu.make_async_copy(hn_ref.at[g, pl.ds(j, 1)],
                                     xs_ref.at[_group(d), pl.ds(_in_group(d), 1)], sem)

    def issue(g, c):
        for j in range(SUBLANES):
            row_copy(g, j, dest_ref[0, 0, g * SUBLANES + j]).start()
        return c

    lax.fori_loop(0, ng, issue, 0)

    def drain(r, c):
        row_copy(0, 0, 0).wait()
        return c

    lax.fori_loop(0, ng * SUBLANES, drain, 0, unroll=8)


def _dispatch(dest3, hn, xs0):
    T = hn.shape[0]
    tt = TOK_TILE
    P = xs0.shape[0]
    out = pl.pallas_call(
        _dispatch_kernel,
        grid=(T // tt,),
        in_specs=[
            pl.BlockSpec((1, 1, tt), lambda i: (i, 0, 0), memory_space=pltpu.SMEM),
            pl.BlockSpec((tt // SUBLANES, SUBLANES, ROW_EXT), lambda i: (i, 0, 0)),
            pl.BlockSpec(memory_space=pl.ANY),
        ],
        out_specs=pl.BlockSpec(memory_space=pl.ANY),
        out_shape=jax.ShapeDtypeStruct((P // SUBLANES, SUBLANES, ROW_EXT), xs0.dtype),
        scratch_shapes=[pltpu.SemaphoreType.DMA],
        input_output_aliases={2: 0},
        compiler_params=_cparams(("arbitrary",)),
        name="moe_dispatch",
    )(dest3, hn.reshape(T // SUBLANES, SUBLANES, ROW_EXT), xs0.reshape(P // SUBLANES, SUBLANES, ROW_EXT))
    return out.reshape(P, ROW_EXT)


def _expert_kernel(ea_ref, eb_ref, xs_ref, *refs):
    del ea_ref, eb_ref
    w_refs, y_ref = refs[:-1], refs[-1]

    def hidden(gu, w):
        g = gu[:, :D_EXPERT]
        return (g * jax.nn.sigmoid(g) * gu[:, D_EXPERT:] * w).astype(BF16)

    for u in range(EXP_GROUP):
        wgua_ref, wda_ref, wgub_ref, wdb_ref = w_refs[4 * u:4 * u + 4]
        x = xs_ref[u, :, :D_MODEL].astype(BF16)
        gua = jnp.dot(x, wgua_ref[0], preferred_element_type=F32)
        gub = jnp.dot(x, wgub_ref[0], preferred_element_type=F32)
        ha = hidden(gua, xs_ref[u, :, D_MODEL + 2:D_MODEL + 3])
        hb = hidden(gub, xs_ref[u, :, D_MODEL + 3:D_MODEL + 4])
        y_ref[u] = (jnp.dot(ha, wda_ref[0], preferred_element_type=F32)
                    + jnp.dot(hb, wdb_ref[0], preferred_element_type=F32))


def _experts(tile_ea, tile_eb, xs, wgu, wd):
    P = xs.shape[0]
    tm = EXP_TILE
    steps = P // (EXP_GROUP * tm)
    assert steps * EXP_GROUP * tm == P
    w_specs = []
    for u in range(EXP_GROUP):
        for table in (0, 1):
            idx = lambda i, ea, eb, u=u, table=table: ((ea, eb)[table][u * steps + i], 0, 0)
            w_specs += [pl.BlockSpec((1, D_MODEL, 2 * D_EXPERT), idx), pl.BlockSpec((1, D_EXPERT, D_MODEL), idx)]
    y = pl.pallas_call(
        _expert_kernel,
        grid_spec=pltpu.PrefetchScalarGridSpec(
            num_scalar_prefetch=2,
            grid=(steps,),
            in_specs=[pl.BlockSpec((EXP_GROUP, tm, ROW_EXT), lambda i, ea, eb: (0, i, 0))] + w_specs,
            out_specs=pl.BlockSpec((EXP_GROUP, tm, D_MODEL), lambda i, ea, eb: (0, i, 0)),
        ),
        out_shape=jax.ShapeDtypeStruct((EXP_GROUP, steps * tm, D_MODEL), F32),
        compiler_params=_cparams(("arbitrary",)),
        name="moe_experts",
    )(tile_ea, tile_eb, xs.reshape(EXP_GROUP, steps * tm, ROW_EXT), *([wgu, wd] * (2 * EXP_GROUP)))
    return y.reshape(P, D_MODEL)


def _combine_kernel(dest_ref, h_ref, y_ref, out_ref, ybuf, sem):
    ng = ybuf.shape[0]

    def row_copy(g, j, d):
        return pltpu.make_async_copy(y_ref.at[_group(d), pl.ds(_in_group(d), 1)],
                                     ybuf.at[g, pl.ds(j, 1)], sem)

    def issue(g, c):
        for j in range(SUBLANES):
            row_copy(g, j, dest_ref[0, 0, g * SUBLANES + j]).start()
        return c

    lax.fori_loop(0, ng, issue, 0)

    def drain(r, c):
        row_copy(0, 0, 0).wait()
        return c

    lax.fori_loop(0, ng * SUBLANES, drain, 0, unroll=8)
    out_ref[...] = h_ref[...] + ybuf[...].reshape(out_ref.shape)


def _combine(dest3, h, y):
    T = h.shape[0]
    tt = TOK_TILE
    P = y.shape[0]
    return pl.pallas_call(
        _combine_kernel,
        grid=(T // tt,),
        in_specs=[
            pl.BlockSpec((1, 1, tt), lambda i: (i, 0, 0), memory_space=pltpu.SMEM),
            pl.BlockSpec((tt, D_MODEL), lambda i: (i, 0)),
            pl.BlockSpec(memory_space=pl.ANY),
        ],
        out_specs=pl.BlockSpec((tt, D_MODEL), lambda i: (i, 0)),
        out_shape=jax.ShapeDtypeStruct((T, D_MODEL), F32),
        scratch_shapes=[pltpu.VMEM((tt // SUBLANES, SUBLANES, D_MODEL), F32), pltpu.SemaphoreType.DMA],
        compiler_params=_cparams(("arbitrary",)),
        name="moe_combine",
    )(dest3, h, y.reshape(P // SUBLANES, SUBLANES, D_MODEL))


def _combine_frames(dest, h, y, B, LK):
    tt = TOK_TILE
    P = y.shape[0]
    S = LK - META_BLOCK
    nj = S // tt
    dest3 = dest.reshape(B, LK)[:, META_BLOCK:].reshape(B * nj, 1, tt)
    return pl.pallas_call(
        _combine_kernel,
        grid=(B, nj),
        in_specs=[
            pl.BlockSpec((1, 1, tt), lambda b, j: (b * nj + j, 0, 0), memory_space=pltpu.SMEM),
            pl.BlockSpec((pl.Element(tt), pl.Element(D_MODEL)),
                         lambda b, j: (pl.multiple_of(b * LK + META_BLOCK + j * tt, LANES), 0)),
            pl.BlockSpec(memory_space=pl.ANY),
        ],
        out_specs=pl.BlockSpec((tt, D_MODEL), lambda b, j: (b * nj + j, 0)),
        out_shape=jax.ShapeDtypeStruct((B * S, D_MODEL), F32),
        scratch_shapes=[pltpu.VMEM((tt // SUBLANES, SUBLANES, D_MODEL), F32), pltpu.SemaphoreType.DMA],
        compiler_params=_cparams(("arbitrary", "arbitrary")),
        name="moe_combine_out",
    )(dest3, h, y.reshape(P // SUBLANES, SUBLANES, D_MODEL))


def _moe(h, hn_ext, info, cnt, wgu, wd, xs_init, frames_of=None):
    T = h.shape[0]
    tm = EXP_TILE
    ntiles = xs_init.shape[0] // tm
    counts = cnt[0].astype(jnp.int32)
    tiles_c = (counts + tm - 1) // tm
    cum = jnp.cumsum(tiles_c)
    start_row = ((cum - tiles_c) * tm).astype(jnp.int32)
    tile_class = jnp.minimum(
        jnp.sum(jnp.arange(ntiles, dtype=jnp.int32)[:, None] >= cum[None, :], axis=1), N_CLASS_SLOTS - 1
    ).astype(jnp.int32)
    group_base = (tile_class // CLASS_STRIDE) * EXPERTS_PER_GROUP
    tile_ea = group_base + (tile_class % CLASS_STRIDE) // EXPERTS_PER_GROUP
    tile_eb = group_base + tile_class % EXPERTS_PER_GROUP
    cr = info[:, 0:2].astype(jnp.int32)
    onehot = (cr[:, 0:1] == jnp.arange(N_CLASS_SLOTS, dtype=jnp.int32)[None, :]).astype(BF16)
    digits = jnp.stack([start_row // 256, start_row % 256], axis=1).astype(BF16)
    hl = jnp.dot(onehot, digits, preferred_element_type=F32).astype(jnp.int32)
    dest = hl[:, 0] * 256 + hl[:, 1] + cr[:, 1]
    dest3 = dest.reshape(T // TOK_TILE, 1, TOK_TILE)
    xs = _dispatch(dest3, hn_ext, xs_init)
    y = _experts(tile_ea, tile_eb, xs, wgu, wd)
    if frames_of is not None:
        return _combine_frames(dest, h, y, *frames_of), xs
    return _combine(dest3, h, y), xs


def _gla_in_kernel(h_ref, w_ref, wgu_ref, bg_ref, q_ref, k_ref, g_ref, v_ref, r_ref):
    h = h_ref[...]
    hn = (h * lax.rsqrt(jnp.mean(h * h, axis=-1, keepdims=True) + EPS)).astype(BF16)
    z = jnp.dot(hn, w_ref[...], preferred_element_type=F32)
    nk = GLA_HEADS * GLA_DK
    nv = GLA_HEADS * GLA_DV
    o_v = 2 * nk
    o_g = o_v + nv
    o_r = o_g + LANES
    xg = jnp.dot(z[:, o_g:o_r].astype(BF16), wgu_ref[...], preferred_element_type=F32) + bg_ref[...]
    log_a = (jnp.minimum(xg, 0.0) - jnp.log(1.0 + jnp.exp(-jnp.abs(xg)))) * (1.0 / GLA_TAU)
    for hh in range(GLA_HEADS):
        q_ref[hh] = z[:, GLA_DK * hh:GLA_DK * (hh + 1)].astype(BF16)
        k_ref[hh] = z[:, nk + GLA_DK * hh:nk + GLA_DK * (hh + 1)].astype(BF16)
        g_ref[hh] = log_a[:, GLA_DK * hh:GLA_DK * (hh + 1)]
        v_ref[hh] = z[:, o_v + GLA_DV * hh:o_v + GLA_DV * (hh + 1)].astype(BF16)
    r_ref[...] = z[:, o_r:].astype(BF16)


def _gla_in(h, w, wgu, bg):
    T = h.shape[0]
    tt = TOK_TILE
    GH = GLA_HEADS
    const = lambda shape: pl.BlockSpec(shape, lambda i: (0,) * len(shape))
    hspec = lambda w_: pl.BlockSpec((GH, tt, w_), lambda i: (0, i, 0))
    return pl.pallas_call(
        _gla_in_kernel,
        grid=(T // tt,),
        in_specs=[pl.BlockSpec((tt, D_MODEL), lambda i: (i, 0)), const(w.shape), const(wgu.shape), const(bg.shape)],
        out_specs=[hspec(GLA_DK), hspec(GLA_DK), hspec(GLA_DK), hspec(GLA_DV),
                   pl.BlockSpec((tt, GH * GLA_DV), lambda i: (i, 0))],
        out_shape=[
            jax.ShapeDtypeStruct((GH, T, GLA_DK), BF16),
            jax.ShapeDtypeStruct((GH, T, GLA_DK), BF16),
            jax.ShapeDtypeStruct((GH, T, GLA_DK), F32),
            jax.ShapeDtypeStruct((GH, T, GLA_DV), BF16),
            jax.ShapeDtypeStruct((T, GH * GLA_DV), BF16),
        ],
        compiler_params=_cparams(("arbitrary",)),
        name="gla_in",
    )(h, w, wgu, bg)


def _bdot(a, b, contract_b):
    return lax.dot_general(a, b, (((2,), (contract_b,)), ((0,), (0,))), preferred_element_type=F32)


def _gla_scan_kernel(q_ref, k_ref, g_ref, v_ref, stack_ref, o_ref, st_ref, oin_ref, qe_ref, kd_ref, dec_ref, *,
                     tile):
    C = SCAN_CHUNK
    nc = tile // C
    t = pl.program_id(1)

    @pl.when(t == 0)
    def _():
        st_ref[...] = jnp.zeros_like(st_ref)

    row = lax.broadcasted_iota(jnp.int32, (C, C), 0)
    col = lax.broadcasted_iota(jnp.int32, (C, C), 1)
    diag = (row == col)[None]
    masks = [(((row // (2 * c)) == (col // (2 * c))) & ((row % (2 * c)) >= c) & ((col % (2 * c)) < c))[None]
             for c in GLA_LEVELS]
    real = (t * tile + lax.broadcasted_iota(jnp.int32, (tile, 1), 0) >= PAD_ROWS).reshape(nc, C, 1)
    stack = stack_ref[...]

    for hh in range(GLA_HEADS):
        q = q_ref[hh].astype(F32).reshape(nc, C, GLA_DK)
        k = jnp.where(real, k_ref[hh].astype(F32).reshape(nc, C, GLA_DK), 0.0)
        g = g_ref[hh].reshape(nc, C, GLA_DK)
        v = v_ref[hh].reshape(nc, C, GLA_DV)
        g_hi = g.astype(BF16)
        g_lo = (g - g_hi.astype(F32)).astype(BF16)
        sums2 = _bdot(stack, jnp.concatenate([g_hi, g_lo], axis=2), 1)
        sums = sums2[:, :, :GLA_DK] + sums2[:, :, GLA_DK:]
        b = sums[:, 0:C]
        b_last = b[:, C - 1:C]
        att = jnp.where(diag, _bdot(q.astype(BF16), k.astype(BF16), 2), 0.0)
        for li in range(len(GLA_LEVELS)):
            ref = sums[:, C * (li + 1):C * (li + 2)]
            qf = (q * jnp.exp(jnp.minimum(b - ref, 0.0))).astype(BF16)
            kb = (k * jnp.exp(jnp.minimum(ref - b, 0.0))).astype(BF16)
            att = att + jnp.where(masks[li], _bdot(qf, kb, 2), 0.0)
        oin_ref[hh] = _bdot(att.astype(BF16), v, 1).reshape(tile, GLA_DV)
        qe_ref[hh] = (q * jnp.exp(b)).astype(BF16).reshape(tile, GLA_DK)
        kd_ref[hh] = (k * jnp.exp(b_last - b)).astype(BF16).reshape(tile, GLA_DK)
        dec_ref[hh] = jnp.broadcast_to(jnp.exp(b_last), (nc, 8, GLA_DK))

    def chunk(ci, carry):
        r0 = pl.multiple_of(ci * C, C)
        for hh in range(GLA_HEADS):
            st = st_ref[hh]
            o = oin_ref[hh, pl.ds(r0, C), :] + _nt_dot(qe_ref[hh, pl.ds(r0, C), :], st.astype(BF16))
            o_ref[pl.ds(r0, C), GLA_DV * hh:GLA_DV * (hh + 1)] = o.astype(BF16)
            st_ref[hh] = st * dec_ref[hh, ci][0:1] + _tn_dot(v_ref[hh, pl.ds(r0, C), :], kd_ref[hh, pl.ds(r0, C), :])
        return carry

    lax.fori_loop(0, nc, chunk, 0)


def _gla_scan(q, k, g, v, stack, B, LK):
    GH, T, _ = q.shape
    tile = 640 if LK % 640 == 0 else LANES
    nt = LK // tile
    nc = tile // SCAN_CHUNK
    stack3 = jnp.broadcast_to(stack[None], (nc,) + stack.shape)
    hspec = lambda w_: pl.BlockSpec((GH, tile, w_), lambda b, t: (0, b * nt + t, 0))
    return pl.pallas_call(
        functools.partial(_gla_scan_kernel, tile=tile),
        grid=(B, nt),
        in_specs=[hspec(GLA_DK), hspec(GLA_DK), hspec(GLA_DK), hspec(GLA_DV),
                  pl.BlockSpec(stack3.shape, lambda b, t: (0, 0, 0))],
        out_specs=pl.BlockSpec((tile, GH * GLA_DV), lambda b, t: (b * nt + t, 0)),
        out_shape=jax.ShapeDtypeStruct((T, GH * GLA_DV), BF16),
        scratch_shapes=[
            pltpu.VMEM((GH, GLA_DV, GLA_DK), F32),
            pltpu.VMEM((GH, tile, GLA_DV), F32),
            pltpu.VMEM((GH, tile, GLA_DK), BF16),
            pltpu.VMEM((GH, tile, GLA_DK), BF16),
            pltpu.VMEM((GH, nc, 8, GLA_DK), F32),
        ],
        compiler_params=_cparams(("arbitrary", "arbitrary")),
        name="gla_scan",
    )(q, k, g, v, stack3)


def _gla_stack():
    C = SCAN_CHUNK
    t = jnp.arange(C)[:, None]
    u = jnp.arange(C)[None, :]
    mats = [u <= t]
    for c in GLA_LEVELS:
        boundary = (t // (2 * c)) * (2 * c) + c - 1
        mats.append(u <= boundary)
    return jnp.concatenate(mats, axis=0).astype(BF16)


def _router_weights(w_rg, b_rg, w_re, b_re):
    w = jnp.zeros((D_MODEL, LANES), F32).at[:, :N_EXPERTS].set(w_re).at[:, N_EXPERTS:N_EXPERTS + N_GROUPS].set(w_rg)
    b = jnp.zeros((1, LANES), F32).at[0, :N_EXPERTS].set(b_re).at[0, N_EXPERTS:N_EXPERTS + N_GROUPS].set(b_rg)
    w_hi = w.astype(BF16)
    w_lo = (w - w_hi.astype(F32)).astype(BF16)
    return w_hi, w_lo, b


def kernel(x, meta_tokens, norm_mix, norm_ffn, mla_w_in, mla_g_q, mla_w_uq, mla_g_kv, mla_w_ukv, mla_g_qn, mla_g_kn, mla_w_o, gla_w_in, gla_w_gate_up, gla_b_gate, gla_g_out, gla_w_o, moe_w_rg, moe_b_rg, moe_w_re, moe_b_re, moe_w_gate, moe_w_up, moe_w_down):
    B, S, D = x.shape
    assert D == D_MODEL and S % (2 * ATT_TILE) == 0
    LK = META_BLOCK + S
    T = B * LK
    assert T % TOK_TILE == 0 and TOK_TILE % LANES == 0
    tt = TOK_TILE
    H = MLA_HEADS

    meta = jnp.broadcast_to(meta_tokens.astype(F32)[None], (B, N_META, D))
    h = jnp.pad(x.astype(F32), ((0, 0), (META_BLOCK, 0), (0, 0))).at[:, PAD_ROWS:META_BLOCK].set(meta)
    h = h.reshape(T, D)

    rows = jnp.arange(LK)
    pos = jnp.where(rows < META_BLOCK, jnp.maximum(rows - PAD_ROWS, 0), rows - META_BLOCK + N_META)
    half = MLA_ROPE // 2
    inv = 1.0 / (ROPE_THETA ** (jnp.arange(half, dtype=F32) / half))
    ang = inv[:, None] * pos.astype(F32)[None, :]
    cos = jnp.tile(jnp.cos(ang), (1, B))
    sin = jnp.tile(jnp.sin(ang), (1, B))

    tri = (jnp.arange(tt)[None, :] < jnp.arange(tt)[:, None]).astype(BF16)

    winT = (mla_w_in[0] * norm_mix[0][:, None]).T.astype(BF16)
    wuqT = (mla_w_uq[0] * mla_g_q[0][:, None]).T.astype(BF16)
    wukv = (mla_w_ukv[0] * mla_g_kv[0][:, None]).reshape(MLA_KV_RANK, H, MLA_NOPE + MLA_V)
    wkT = wukv[:, :, :MLA_NOPE].reshape(MLA_KV_RANK, H * MLA_NOPE).T.astype(BF16)
    wv = jnp.zeros((MLA_KV_RANK, H, LANES), F32).at[:, :, :MLA_V].set(wukv[:, :, MLA_NOPE:])
    wv = wv.reshape(MLA_KV_RANK, H * LANES).astype(BF16)
    vone = jnp.zeros((1, H, LANES), F32).at[:, :, MLA_V].set(1.0).reshape(1, H * LANES)
    gq = jnp.broadcast_to((mla_g_qn[0] * (MLA_QK ** -0.5 * LOG2E))[:, None], (MLA_QK, tt)).astype(F32)
    gk = jnp.broadcast_to(mla_g_kn[0][:, None], (MLA_QK, tt)).astype(F32)
    q, kT, v, stats = _mla_proj(h, winT, wuqT, wkT, wv, vone, gq, gk, cos, sin)
    att = _attention(q, kT, v, stats, B, LK)

    wrh, wrl, br = _router_weights(moe_w_rg[0], moe_b_rg[0], moe_w_re[0], moe_b_re[0])
    h, hn, info, cnt = _mixer_out(h, att, mla_w_o[0].astype(BF16), norm_ffn[0][None, :], wrh, wrl, br, tri)
    wgu = jnp.concatenate([moe_w_gate[0], moe_w_up[0]], axis=-1).astype(BF16)
    xs0 = jnp.zeros(((T // EXP_TILE + N_CLASSES) * EXP_TILE, ROW_EXT), F32)
    h, xs1 = _moe(h, hn, info, cnt, wgu, moe_w_down[0].astype(BF16), xs0)

    nk = GLA_HEADS * GLA_DK
    nv = GLA_HEADS * GLA_DV
    w1 = gla_w_in[0] * norm_mix[1][:, None]
    wg_pad = jnp.zeros((D, LANES), F32).at[:, :GLA_GATE_RANK].set(w1[:, 2 * nk + nv:2 * nk + nv + GLA_GATE_RANK])
    w_all = jnp.concatenate([w1[:, :nk] * (GLA_DK ** -0.5), w1[:, nk:2 * nk + nv], wg_pad,
                             w1[:, 2 * nk + nv + GLA_GATE_RANK:]], axis=1).astype(BF16)
    wgate = jnp.zeros((LANES, nk), F32).at[:GLA_GATE_RANK].set(gla_w_gate_up[0]).astype(BF16)
    gq_, gk_, gg_, gv_, gr_ = _gla_in(h, w_all, wgate, gla_b_gate[0][None, :])
    go = _gla_scan(gq_, gk_, gg_, gv_, _gla_stack(), B, LK)

    wrh, wrl, br = _router_weights(moe_w_rg[1], moe_b_rg[1], moe_w_re[1], moe_b_re[1])
    h, hn, info, cnt = _mixer_out(h, go, gla_w_o[0].astype(BF16), norm_ffn[1][None, :], wrh, wrl, br, tri,
                                  gla_extra=(gr_, gla_g_out[0][None, :]))
    wgu = jnp.concatenate([moe_w_gate[1], moe_w_up[1]], axis=-1).astype(BF16)
    out, _ = _moe(h, hn, info, cnt, wgu, moe_w_down[1].astype(BF16), xs1, frames_of=(B, LK))
    return out.reshape(B, S, D).astype(x.dtype)
```

```python
import functools

import jax
import jax.numpy as jnp
from jax import lax
from jax.experimental import pallas as pl
from jax.experimental.pallas import tpu as pltpu

F32 = jnp.float32
BF16 = jnp.bfloat16

D_MODEL = 1024
CHUNK = 64
N_META = 16
MLA_HEADS = 16
MLA_Q_RANK = 384
MLA_KV_RANK = 256
MLA_NOPE = 64
MLA_ROPE = 32
MLA_V = 64
MLA_QK = MLA_NOPE + MLA_ROPE
ROPE_THETA = 10000.0
GLA_HEADS = 4
GLA_DK = 128
GLA_DV = 256
GLA_GATE_RANK = 16
GLA_TAU = 16.0
N_GROUPS = 4
EXPERTS_PER_GROUP = 8
N_EXPERTS = N_GROUPS * EXPERTS_PER_GROUP
D_EXPERT = 256
EPS = 1e-6

LANES = 128
SUBLANES = 8
META_BLOCK = 128
PAD_ROWS = META_BLOCK - N_META
NEG = -1e30
LOG2E = 1.4426950408889634

TOK_TILE = 512
ATT_TILE = 512
ATT_SAFE_BOUND = 50.0
SCAN_CHUNK = 64
EXP_TILE = 128
EXP_GROUP = 4
CLASS_STRIDE = EXPERTS_PER_GROUP * EXPERTS_PER_GROUP
N_CLASS_SLOTS = N_GROUPS * CLASS_STRIDE
N_CLASSES = N_GROUPS * (EXPERTS_PER_GROUP * (EXPERTS_PER_GROUP - 1) // 2)
ROW_EXT = D_MODEL + LANES
VMEM_LIMIT = 56 * 1024 * 1024

GLA_LEVELS = (32, 16, 8, 4, 2, 1)


def _cparams(sem):
    return pltpu.CompilerParams(dimension_semantics=sem, vmem_limit_bytes=VMEM_LIMIT)


def _nt_dot(a, b):
    return lax.dot_general(a, b, (((1,), (1,)), ((), ())), preferred_element_type=F32)


def _tn_dot(a, b):
    return lax.dot_general(a, b, (((0,), (0,)), ((), ())), preferred_element_type=F32)


def _mla_proj_kernel(h_ref, winT_ref, wuqT_ref, wkT_ref, wv_ref, vone_ref, gq_ref, gk_ref,
                     cos_ref, sin_ref, q_ref, kT_ref, v_ref, stats_ref):
    tt = h_ref.shape[0]
    h = h_ref[...]
    hn = (h * lax.rsqrt(jnp.mean(h * h, axis=-1, keepdims=True) + EPS)).astype(BF16)
    zT = _nt_dot(winT_ref[...], hn)
    cq = zT[0:MLA_Q_RANK]
    ckv = zT[MLA_Q_RANK:MLA_Q_RANK + MLA_KV_RANK]
    kr = zT[MLA_Q_RANK + MLA_KV_RANK:]
    cqn = (cq * lax.rsqrt(jnp.mean(cq * cq, axis=0, keepdims=True) + EPS)).astype(BF16)
    ckvn = ckv * lax.rsqrt(jnp.mean(ckv * ckv, axis=0, keepdims=True) + EPS)
    qT = jnp.dot(wuqT_ref[...], cqn, preferred_element_type=F32)
    kT = jnp.dot(wkT_ref[...], ckvn.astype(BF16), preferred_element_type=F32)
    v = jnp.dot(ckvn.T.astype(BF16), wv_ref[...], preferred_element_type=F32) + vone_ref[...]
    cos = cos_ref[...]
    sin = sin_ref[...]
    gq = gq_ref[...]
    gk = gk_ref[...]
    half = MLA_ROPE // 2
    zpad = jnp.zeros((LANES - MLA_QK - 1, tt), F32)
    kpad = jnp.concatenate([jnp.ones((1, tt), F32), zpad], axis=0)
    spad = jnp.zeros((6, tt), F32)
    kr_ss = jnp.sum(kr * kr, axis=0, keepdims=True)
    for hh in range(MLA_HEADS):
        qh = qT[MLA_QK * hh:MLA_QK * (hh + 1)]
        qn = qh * lax.rsqrt(jnp.sum(qh * qh, axis=0, keepdims=True) * (1.0 / MLA_QK) + EPS) * gq
        x1 = qn[MLA_NOPE:MLA_NOPE + half]
        x2 = qn[MLA_NOPE + half:]
        qmain = jnp.concatenate([qn[:MLA_NOPE], x1 * cos - x2 * sin, x2 * cos + x1 * sin], axis=0)
        qsq = jnp.sum(qmain * qmain, axis=0, keepdims=True)
        q_ref[hh] = jnp.concatenate([qmain, -jnp.sqrt(qsq), zpad], axis=0).T.astype(BF16)
        kn = kT[MLA_NOPE * hh:MLA_NOPE * (hh + 1)]
        rk = lax.rsqrt((jnp.sum(kn * kn, axis=0, keepdims=True) + kr_ss) * (1.0 / MLA_QK) + EPS)
        knn = kn * rk * gk[:MLA_NOPE]
        krn = kr * rk * gk[MLA_NOPE:]
        y1 = krn[:half]
        y2 = krn[half:]
        kmain = jnp.concatenate([knn, y1 * cos - y2 * sin, y2 * cos + y1 * sin], axis=0)
        stats_ref[hh] = jnp.concatenate([jnp.sum(kmain * kmain, axis=0, keepdims=True), qsq, spad], axis=0)
        kfull = jnp.concatenate([kmain, kpad], axis=0).astype(BF16)
        for u in range(tt // LANES):
            kT_ref[hh, u] = kfull[:, LANES * u:LANES * (u + 1)]
        v_ref[hh] = v[:, LANES * hh:LANES * (hh + 1)].astype(BF16)


def _mla_proj(h, winT, wuqT, wkT, wv, vone, gq, gk, cos, sin):
    T = h.shape[0]
    tt = TOK_TILE
    H = MLA_HEADS
    const = lambda shape: pl.BlockSpec(shape, lambda i: (0,) * len(shape))
    return pl.pallas_call(
        _mla_proj_kernel,
        grid=(T // tt,),
        in_specs=[
            pl.BlockSpec((tt, D_MODEL), lambda i: (i, 0)),
            const(winT.shape), const(wuqT.shape), const(wkT.shape), const(wv.shape), const(vone.shape),
            const(gq.shape), const(gk.shape),
            pl.BlockSpec((MLA_ROPE // 2, tt), lambda i: (0, i)),
            pl.BlockSpec((MLA_ROPE // 2, tt), lambda i: (0, i)),
        ],
        out_specs=[
            pl.BlockSpec((H, tt, LANES), lambda i: (0, i, 0)),
            pl.BlockSpec((H, tt // LANES, LANES, LANES), lambda i: (0, i, 0, 0)),
            pl.BlockSpec((H, tt, LANES), lambda i: (0, i, 0)),
            pl.BlockSpec((H, 8, tt), lambda i: (0, 0, i)),
        ],
        out_shape=[
            jax.ShapeDtypeStruct((H, T, LANES), BF16),
            jax.ShapeDtypeStruct((H, T // LANES, LANES, LANES), BF16),
            jax.ShapeDtypeStruct((H, T, LANES), BF16),
            jax.ShapeDtypeStruct((H, 8, T), F32),
        ],
        compiler_params=_cparams(("arbitrary",)),
        name="mla_proj",
    )(h, winT, wuqT, wkT, wv, vone, gq, gk, cos, sin)


def _attn_kernel(q_ref, kT_ref, v_ref, stats_ref, o_ref, *, nf):
    tq = ATT_TILE
    nsub = tq // LANES
    heads = range(2)
    row = lax.broadcasted_iota(jnp.int32, (tq, tq), 0)
    col = lax.broadcasted_iota(jnp.int32, (tq, tq), 1)
    diag_bias = jnp.where((col // CHUNK) <= (row // CHUNK), 0.0, NEG).astype(F32)
    col0 = lax.broadcasted_iota(jnp.int32, (1, LANES), 1)
    bias0 = jnp.where(col0 >= PAD_ROWS, 0.0, NEG).astype(F32)

    def ktile(hh, blk):
        return jnp.concatenate([kT_ref[hh, blk + u] for u in range(nsub)], axis=1)

    def vtile(hh, t):
        return v_ref[hh, pl.ds(pl.multiple_of(META_BLOCK + t * tq, LANES), tq), :]

    def finish(acc):
        return (acc[:, :MLA_V] / acc[:, MLA_V:MLA_V + 1]).astype(BF16)

    def put(r0, n, hh, acc):
        o_ref[pl.ds(r0, n), MLA_V * hh:MLA_V * (hh + 1)] = finish(acc)

    sq_max = [jnp.max(stats_ref[hh], axis=1, keepdims=True) for hh in heads]
    worst = jnp.sqrt(jnp.maximum(sq_max[0][0:1] * sq_max[0][1:2], sq_max[1][0:1] * sq_max[1][1:2]))

    def shifted():
        lane = lax.broadcasted_iota(jnp.int32, (1, LANES), 1)
        col_scale = [jnp.where(lane == MLA_QK, jnp.sqrt(sq_max[hh][0:1]), 1.0).astype(BF16) for hh in heads]

        def with_shift(q, hh):
            return q * col_scale[hh]

        def part(q, kt, vv, bias):
            s = jnp.dot(q, kt, preferred_element_type=F32)
            if bias is not None:
                s = s + bias
            return jnp.dot(jnp.exp2(s).astype(BF16), vv, preferred_element_type=F32)

        for hh in heads:
            q0 = with_shift(q_ref[hh, 0:META_BLOCK, :], hh)
            put(0, META_BLOCK, hh, part(q0, kT_ref[hh, 0], v_ref[hh, 0:META_BLOCK, :], bias0))

        chains = [(hh, r) for hh in heads for r in range(2)]
        zk = jnp.zeros((LANES, META_BLOCK), BF16)
        k0_pair = jnp.concatenate([jnp.concatenate([kT_ref[0, 0], zk], axis=1),
                                   jnp.concatenate([zk, kT_ref[1, 0]], axis=1)], axis=0)
        v0_pair = jnp.concatenate([jnp.concatenate([v_ref[0, 0:META_BLOCK, :], zk], axis=1),
                                   jnp.concatenate([zk, v_ref[1, 0:META_BLOCK, :]], axis=1)], axis=0)
        bias0_pair = jnp.concatenate([bias0, bias0], axis=1)

        def qpair(ii, carry):
            r0 = pl.multiple_of(META_BLOCK + 2 * ii * tq, LANES)
            qs = [with_shift(q_ref[hh, pl.ds(r0 + r * tq, tq), :], hh) for hh, r in chains]

            def kv(j, acc):
                c0 = pl.multiple_of(META_BLOCK + 2 * j * tq, LANES)
                kts = [jnp.concatenate([kT_ref[hh, 1 + 2 * j * nsub + u] for u in range(2 * nsub)], axis=1)
                       for hh in heads]
                vts = [v_ref[hh, pl.ds(c0, 2 * tq), :] for hh in heads]
                return tuple(acc[c] + part(qs[c], kts[hh], vts[hh], None) for c, (hh, r) in enumerate(chains))

            accs = lax.fori_loop(0, ii, kv, tuple(jnp.zeros((tq, LANES), F32) for _ in chains))
            meta = [part(jnp.concatenate([qs[r], qs[2 + r]], axis=1), k0_pair, v0_pair, bias0_pair)
                    for r in range(2)]
            for c, (hh, r) in enumerate(chains):
                acc = accs[c] + meta[r][:, LANES * hh:LANES * (hh + 1)]
                if r == 1:
                    acc = acc + part(qs[c], ktile(hh, 1 + 2 * ii * nsub), vtile(hh, 2 * ii), None)
                acc = acc + part(qs[c], ktile(hh, 1 + (2 * ii + r) * nsub), vtile(hh, 2 * ii + r), diag_bias)
                put(r0 + r * tq, tq, hh, acc)
            return carry

        lax.fori_loop(0, nf // 2, qpair, 0)

    def online():
        def step(q, kt, vv, m, acc, bias):
            s = jnp.dot(q, kt, preferred_element_type=F32)
            if bias is not None:
                s = s + bias
            m_new = jnp.maximum(m, jnp.max(s, axis=-1, keepdims=True))
            p = jnp.exp2(s - m_new)
            acc = jnp.exp2(m - m_new) * acc + jnp.dot(p.astype(BF16), vv, preferred_element_type=F32)
            return m_new, acc

        def first(q, hh, n):
            return step(q, kT_ref[hh, 0], v_ref[hh, 0:META_BLOCK, :], jnp.full((n, 1), NEG, F32),
                        jnp.zeros((n, LANES), F32), bias0)

        for hh in heads:
            put(0, META_BLOCK, hh, first(q_ref[hh, 0:META_BLOCK, :], hh, META_BLOCK)[1])

        def qtile(i, carry):
            r0 = pl.multiple_of(META_BLOCK + i * tq, LANES)
            qs = [q_ref[hh, pl.ds(r0, tq), :] for hh in heads]
            state = []
            for hh in heads:
                state += list(first(qs[hh], hh, tq))

            def kv(j, c):
                out = []
                for hh in heads:
                    out += list(step(qs[hh], ktile(hh, 1 + j * nsub), vtile(hh, j), c[2 * hh], c[2 * hh + 1], None))
                return tuple(out)

            state = lax.fori_loop(0, i, kv, tuple(state))
            for hh in heads:
                m, acc = step(qs[hh], ktile(hh, 1 + i * nsub), vtile(hh, i), state[2 * hh], state[2 * hh + 1],
                              diag_bias)
                put(r0, tq, hh, acc)
            return carry

        lax.fori_loop(0, nf, qtile, 0)

    lax.cond(worst[0, 0] <= ATT_SAFE_BOUND, shifted, online)


def _attention(q, kT, v, stats, B, LK):
    H, T, _ = q.shape
    nf = (LK - META_BLOCK) // ATT_TILE
    nblk = LK // LANES
    return pl.pallas_call(
        functools.partial(_attn_kernel, nf=nf),
        grid=(B, H // 2),
        in_specs=[
            pl.BlockSpec((2, LK, LANES), lambda b, hp: (hp, b, 0)),
            pl.BlockSpec((2, nblk, LANES, LANES), lambda b, hp: (hp, b, 0, 0)),
            pl.BlockSpec((2, LK, LANES), lambda b, hp: (hp, b, 0)),
            pl.BlockSpec((2, 8, LK), lambda b, hp: (hp, 0, b)),
        ],
        out_specs=pl.BlockSpec((LK, 2 * MLA_V), lambda b, hp: (b, hp)),
        out_shape=jax.ShapeDtypeStruct((T, H * MLA_V), BF16),
        compiler_params=_cparams(("arbitrary", "arbitrary")),
        name="mla_attention",
    )(q, kT, v, stats)


def _route(hnew, gffn_ref, wrh_ref, wrl_ref, br_ref, tri_ref, hn_ref, info_ref, cnt_ref, carry_ref):
    tt = hnew.shape[0]
    hn = hnew * lax.rsqrt(jnp.mean(hnew * hnew, axis=-1, keepdims=True) + EPS) * gffn_ref[...]
    hn_hi = hn.astype(BF16)
    hn_lo = (hn - hn_hi.astype(F32)).astype(BF16)
    wrh = wrh_ref[...]
    both = jnp.dot(hn_hi, jnp.concatenate([wrh, wrl_ref[...]], axis=1), preferred_element_type=F32)
    logits = (both[:, :LANES] + both[:, LANES:]
              + jnp.dot(hn_lo, wrh, preferred_element_type=F32)) + br_ref[...]

    lane = lax.broadcasted_iota(jnp.int32, (tt, LANES), 1)
    lane_f = lane.astype(F32)
    big = float(LANES)
    gmask = (lane >= N_EXPERTS) & (lane < N_EXPERTS + N_GROUPS)
    gl = jnp.where(gmask, logits, NEG)
    gmax = jnp.max(gl, axis=-1, keepdims=True)
    gsel = jnp.min(jnp.where(gl == gmax, lane_f, big), axis=-1, keepdims=True) - float(N_EXPERTS)
    pg = 1.0 / jnp.sum(jnp.exp(gl - gmax), axis=-1, keepdims=True)
    egrp = (lane // EXPERTS_PER_GROUP).astype(F32)
    emask = (lane < N_EXPERTS) & (egrp == gsel)
    el = jnp.where(emask, logits, NEG)
    m1 = jnp.max(el, axis=-1, keepdims=True)
    i1 = jnp.min(jnp.where(el == m1, lane_f, big), axis=-1, keepdims=True)
    el2 = jnp.where(lane_f == i1, NEG, el)
    m2 = jnp.max(el2, axis=-1, keepdims=True)
    i2 = jnp.min(jnp.where(el2 == m2, lane_f, big), axis=-1, keepdims=True)
    t21 = jnp.exp(m2 - m1)
    w1 = pg / (1.0 + t21)
    w2 = w1 * t21

    lo = jnp.minimum(i1, i2)
    hi = jnp.maximum(i1, i2)
    first_is_lo = i1 < i2
    wa = jnp.where(first_is_lo, w1, w2)
    wb = jnp.where(first_is_lo, w2, w1)
    gbase = gsel * float(EXPERTS_PER_GROUP)
    cls = gsel * float(CLASS_STRIDE) + (lo - gbase) * float(EXPERTS_PER_GROUP) + (hi - gbase)
    lane2 = lax.broadcasted_iota(jnp.int32, (tt, N_CLASS_SLOTS), 1).astype(F32)
    sel = lane2 == cls
    oh = jnp.where(sel, 1.0, 0.0)
    before = jnp.dot(tri_ref[...], oh.astype(BF16), preferred_element_type=F32) + carry_ref[...]
    rank = jnp.sum(jnp.where(sel, before, 0.0), axis=-1, keepdims=True)
    carry_ref[...] = carry_ref[...] + jnp.sum(oh, axis=0, keepdims=True)
    info = jnp.where(lane == 0, cls, jnp.where(lane == 1, rank, jnp.where(lane == 2, wa, jnp.where(
        lane == 3, wb, 0.0))))
    hn_ref[:, :D_MODEL] = hn
    hn_ref[:, D_MODEL:] = info
    info_ref[...] = info
    cnt_ref[...] = jnp.broadcast_to(carry_ref[...], cnt_ref.shape)


def _mla_out_kernel(h_ref, a_ref, wo_ref, gffn_ref, wrh_ref, wrl_ref, br_ref, tri_ref,
                    hout_ref, hn_ref, info_ref, cnt_ref, carry_ref):
    @pl.when(pl.program_id(0) == 0)
    def _():
        carry_ref[...] = jnp.zeros_like(carry_ref)

    hnew = h_ref[...] + jnp.dot(a_ref[...], wo_ref[...], preferred_element_type=F32)
    hout_ref[...] = hnew
    _route(hnew, gffn_ref, wrh_ref, wrl_ref, br_ref, tri_ref, hn_ref, info_ref, cnt_ref, carry_ref)


def _gla_out_kernel(h_ref, a_ref, r_ref, gout_ref, wo_ref, gffn_ref, wrh_ref, wrl_ref, br_ref, tri_ref,
                    hout_ref, hn_ref, info_ref, cnt_ref, carry_ref):
    @pl.when(pl.program_id(0) == 0)
    def _():
        carry_ref[...] = jnp.zeros_like(carry_ref)

    o = a_ref[...].astype(F32)
    r = r_ref[...].astype(F32)
    gout = gout_ref[...]
    parts = []
    for hh in range(GLA_HEADS):
        oh = o[:, GLA_DV * hh:GLA_DV * (hh + 1)]
        parts.append(oh * lax.rsqrt(jnp.mean(oh * oh, axis=-1, keepdims=True) + EPS) * gout)
    a = (jnp.concatenate(parts, axis=1) * (r * jax.nn.sigmoid(r))).astype(BF16)
    hnew = h_ref[...] + jnp.dot(a, wo_ref[...], preferred_element_type=F32)
    hout_ref[...] = hnew
    _route(hnew, gffn_ref, wrh_ref, wrl_ref, br_ref, tri_ref, hn_ref, info_ref, cnt_ref, carry_ref)


def _mixer_out(h, a, wo, gffn, wrh, wrl, br, tri, gla_extra=None):
    T = h.shape[0]
    tt = TOK_TILE
    tile = lambda w: pl.BlockSpec((tt, w), lambda i: (i, 0))
    const = lambda shape: pl.BlockSpec(shape, lambda i: (0,) * len(shape))
    if gla_extra is None:
        kern = _mla_out_kernel
        ins = [h, a]
        in_specs = [tile(D_MODEL), tile(a.shape[1])]
    else:
        r, gout = gla_extra
        kern = _gla_out_kernel
        ins = [h, a, r, gout]
        in_specs = [tile(D_MODEL), tile(a.shape[1]), tile(r.shape[1]), const(gout.shape)]
    ins += [wo, gffn, wrh, wrl, br, tri]
    in_specs += [const(wo.shape), const(gffn.shape), const(wrh.shape), const(wrl.shape), const(br.shape),
                 const(tri.shape)]
    return pl.pallas_call(
        kern,
        grid=(T // tt,),
        in_specs=in_specs,
        out_specs=[tile(D_MODEL), tile(ROW_EXT), tile(LANES), pl.BlockSpec((8, N_CLASS_SLOTS), lambda i: (0, 0))],
        out_shape=[
            jax.ShapeDtypeStruct((T, D_MODEL), F32),
            jax.ShapeDtypeStruct((T, ROW_EXT), F32),
            jax.ShapeDtypeStruct((T, LANES), F32),
            jax.ShapeDtypeStruct((8, N_CLASS_SLOTS), F32),
        ],
        scratch_shapes=[pltpu.VMEM((1, N_CLASS_SLOTS), F32)],
        compiler_params=_cparams(("arbitrary",)),
        name="mixer_out_router",
    )(*ins)


def _group(d):
    return lax.shift_right_logical(d, SUBLANES.bit_length() - 1)


def _in_group(d):
    return d & (SUBLANES - 1)


def _dispatch_kernel(dest_ref, hn_ref, xs_in_ref, xs_ref, sem):
    del xs_in_ref
    ng = hn_ref.shape[0]

    def row_copy(g, j, d):
        return pltpu.make_async_copy(hn_ref.at[g, pl.ds(j, 1)],
                                     xs_ref.at[_group(d), pl.ds(_in_group(d), 1)], sem)

    def issue(g, c):
        for j in range(SUBLANES):
            row_copy(g, j, dest_ref[0, 0, g * SUBLANES + j]).start()
        return c

    lax.fori_loop(0, ng, issue, 0)

    def drain(r, c):
        row_copy(0, 0, 0).wait()
        return c

    lax.fori_loop(0, ng * SUBLANES, drain, 0, unroll=8)


def _dispatch(dest3, hn, xs0):
    T = hn.shape[0]
    tt = TOK_TILE
    P = xs0.shape[0]
    out = pl.pallas_call(
        _dispatch_kernel,
        grid=(T // tt,),
        in_specs=[
            pl.BlockSpec((1, 1, tt), lambda i: (i, 0, 0), memory_space=pltpu.SMEM),
            pl.BlockSpec((tt // SUBLANES, SUBLANES, ROW_EXT), lambda i: (i, 0, 0)),
            pl.BlockSpec(memory_space=pl.ANY),
        ],
        out_specs=pl.BlockSpec(memory_space=pl.ANY),
        out_shape=jax.ShapeDtypeStruct((P // SUBLANES, SUBLANES, ROW_EXT), xs0.dtype),
        scratch_shapes=[pltpu.SemaphoreType.DMA],
        input_output_aliases={2: 0},
        compiler_params=_cparams(("arbitrary",)),
        name="moe_dispatch",
    )(dest3, hn.reshape(T // SUBLANES, SUBLANES, ROW_EXT), xs0.reshape(P // SUBLANES, SUBLANES, ROW_EXT))
    return out.reshape(P, ROW_EXT)


def _expert_kernel(ea_ref, eb_ref, xs_ref, *refs):
    del ea_ref, eb_ref
    w_refs, y_ref = refs[:-1], refs[-1]

    def hidden(gu, w):
        g = gu[:, :D_EXPERT]
        return (g * jax.nn.sigmoid(g) * gu[:, D_EXPERT:] * w).astype(BF16)

    for u in range(EXP_GROUP):
        wgua_ref, wda_ref, wgub_ref, wdb_ref = w_refs[4 * u:4 * u + 4]
        x = xs_ref[u, :, :D_MODEL].astype(BF16)
        gua = jnp.dot(x, wgua_ref[0], preferred_element_type=F32)
        gub = jnp.dot(x, wgub_ref[0], preferred_element_type=F32)
        ha = hidden(gua, xs_ref[u, :, D_MODEL + 2:D_MODEL + 3])
        hb = hidden(gub, xs_ref[u, :, D_MODEL + 3:D_MODEL + 4])
        y_ref[u] = (jnp.dot(ha, wda_ref[0], preferred_element_type=F32)
                    + jnp.dot(hb, wdb_ref[0], preferred_element_type=F32))


def _experts(tile_ea, tile_eb, xs, wgu, wd):
    P = xs.shape[0]
    tm = EXP_TILE
    steps = P // (EXP_GROUP * tm)
    assert steps * EXP_GROUP * tm == P
    w_specs = []
    for u in range(EXP_GROUP):
        for table in (0, 1):
            idx = lambda i, ea, eb, u=u, table=table: ((ea, eb)[table][u * steps + i], 0, 0)
            w_specs += [pl.BlockSpec((1, D_MODEL, 2 * D_EXPERT), idx), pl.BlockSpec((1, D_EXPERT, D_MODEL), idx)]
    y = pl.pallas_call(
        _expert_kernel,
        grid_spec=pltpu.PrefetchScalarGridSpec(
            num_scalar_prefetch=2,
            grid=(steps,),
            in_specs=[pl.BlockSpec((EXP_GROUP, tm, ROW_EXT), lambda i, ea, eb: (0, i, 0))] + w_specs,
            out_specs=pl.BlockSpec((EXP_GROUP, tm, D_MODEL), lambda i, ea, eb: (0, i, 0)),
        ),
        out_shape=jax.ShapeDtypeStruct((EXP_GROUP, steps * tm, D_MODEL), F32),
        compiler_params=_cparams(("arbitrary",)),
        name="moe_experts",
    )(tile_ea, tile_eb, xs.reshape(EXP_GROUP, steps * tm, ROW_EXT), *([wgu, wd] * (2 * EXP_GROUP)))
    return y.reshape(P, D_MODEL)


def _combine_kernel(dest_ref, h_ref, y_ref, out_ref, ybuf, sem):
    ng = ybuf.shape[0]

    def row_copy(g, j, d):
        return pltpu.make_async_copy(y_ref.at[_group(d), pl.ds(_in_group(d), 1)],
                                     ybuf.at[g, pl.ds(j, 1)], sem)

    def issue(g, c):
        for j in range(SUBLANES):
            row_copy(g, j, dest_ref[0, 0, g * SUBLANES + j]).start()
        return c

    lax.fori_loop(0, ng, issue, 0)

    def drain(r, c):
        row_copy(0, 0, 0).wait()
        return c

    lax.fori_loop(0, ng * SUBLANES, drain, 0, unroll=8)
    out_ref[...] = h_ref[...] + ybuf[...].reshape(out_ref.shape)


def _combine(dest3, h, y):
    T = h.shape[0]
    tt = TOK_TILE
    P = y.shape[0]
    return pl.pallas_call(
        _combine_kernel,
        grid=(T // tt,),
        in_specs=[
            pl.BlockSpec((1, 1, tt), lambda i: (i, 0, 0), memory_space=pltpu.SMEM),
            pl.BlockSpec((tt, D_MODEL), lambda i: (i, 0)),
            pl.BlockSpec(memory_space=pl.ANY),
        ],
        out_specs=pl.BlockSpec((tt, D_MODEL), lambda i: (i, 0)),
        out_shape=jax.ShapeDtypeStruct((T, D_MODEL), F32),
        scratch_shapes=[pltpu.VMEM((tt // SUBLANES, SUBLANES, D_MODEL), F32), pltpu.SemaphoreType.DMA],
        compiler_params=_cparams(("arbitrary",)),
        name="moe_combine",
    )(dest3, h, y.reshape(P // SUBLANES, SUBLANES, D_MODEL))


def _combine_frames(dest, h, y, B, LK):
    tt = TOK_TILE
    P = y.shape[0]
    S = LK - META_BLOCK
    nj = S // tt
    dest3 = dest.reshape(B, LK)[:, META_BLOCK:].reshape(B * nj, 1, tt)
    return pl.pallas_call(
        _combine_kernel,
        grid=(B, nj),
        in_specs=[
            pl.BlockSpec((1, 1, tt), lambda b, j: (b * nj + j, 0, 0), memory_space=pltpu.SMEM),
            pl.BlockSpec((pl.Element(tt), pl.Element(D_MODEL)),
                         lambda b, j: (pl.multiple_of(b * LK + META_BLOCK + j * tt, LANES), 0)),
            pl.BlockSpec(memory_space=pl.ANY),
        ],
        out_specs=pl.BlockSpec((tt, D_MODEL), lambda b, j: (b * nj + j, 0)),
        out_shape=jax.ShapeDtypeStruct((B * S, D_MODEL), F32),
        scratch_shapes=[pltpu.VMEM((tt // SUBLANES, SUBLANES, D_MODEL), F32), pltpu.SemaphoreType.DMA],
        compiler_params=_cparams(("arbitrary", "arbitrary")),
        name="moe_combine_out",
    )(dest3, h, y.reshape(P // SUBLANES, SUBLANES, D_MODEL))


def _moe(h, hn_ext, info, cnt, wgu, wd, xs_init, frames_of=None):
    T = h.shape[0]
    tm = EXP_TILE
    ntiles = xs_init.shape[0] // tm
    counts = cnt[0].astype(jnp.int32)
    tiles_c = (counts + tm - 1) // tm
    cum = jnp.cumsum(tiles_c)
    start_row = ((cum - tiles_c) * tm).astype(jnp.int32)
    tile_class = jnp.minimum(
        jnp.sum(jnp.arange(ntiles, dtype=jnp.int32)[:, None] >= cum[None, :], axis=1), N_CLASS_SLOTS - 1
    ).astype(jnp.int32)
    group_base = (tile_class // CLASS_STRIDE) * EXPERTS_PER_GROUP
    tile_ea = group_base + (tile_class % CLASS_STRIDE) // EXPERTS_PER_GROUP
    tile_eb = group_base + tile_class % EXPERTS_PER_GROUP
    cr = info[:, 0:2].astype(jnp.int32)
    onehot = (cr[:, 0:1] == jnp.arange(N_CLASS_SLOTS, dtype=jnp.int32)[None, :]).astype(BF16)
    digits = jnp.stack([start_row // 256, start_row % 256], axis=1).astype(BF16)
    hl = jnp.dot(onehot, digits, preferred_element_type=F32).astype(jnp.int32)
    dest = hl[:, 0] * 256 + hl[:, 1] + cr[:, 1]
    dest3 = dest.reshape(T // TOK_TILE, 1, TOK_TILE)
    xs = _dispatch(dest3, hn_ext, xs_init)
    y = _experts(tile_ea, tile_eb, xs, wgu, wd)
    if frames_of is not None:
        return _combine_frames(dest, h, y, *frames_of), xs
    return _combine(dest3, h, y), xs


def _gla_in_kernel(h_ref, w_ref, wgu_ref, bg_ref, q_ref, k_ref, g_ref, v_ref, r_ref):
    h = h_ref[...]
    hn = (h * lax.rsqrt(jnp.mean(h * h, axis=-1, keepdims=True) + EPS)).astype(BF16)
    z = jnp.dot(hn, w_ref[...], preferred_element_type=F32)
    nk = GLA_HEADS * GLA_DK
    nv = GLA_HEADS * GLA_DV
    o_v = 2 * nk
    o_g = o_v + nv
    o_r = o_g + LANES
    xg = jnp.dot(z[:, o_g:o_r].astype(BF16), wgu_ref[...], preferred_element_type=F32) + bg_ref[...]
    log_a = (jnp.minimum(xg, 0.0) - jnp.log(1.0 + jnp.exp(-jnp.abs(xg)))) * (1.0 / GLA_TAU)
    for hh in range(GLA_HEADS):
        q_ref[hh] = z[:, GLA_DK * hh:GLA_DK * (hh + 1)].astype(BF16)
        k_ref[hh] = z[:, nk + GLA_DK * hh:nk + GLA_DK * (hh + 1)].astype(BF16)
        g_ref[hh] = log_a[:, GLA_DK * hh:GLA_DK * (hh + 1)]
        v_ref[hh] = z[:, o_v + GLA_DV * hh:o_v + GLA_DV * (hh + 1)].astype(BF16)
    r_ref[...] = z[:, o_r:].astype(BF16)


def _gla_in(h, w, wgu, bg):
    T = h.shape[0]
    tt = TOK_TILE
    GH = GLA_HEADS
    const = lambda shape: pl.BlockSpec(shape, lambda i: (0,) * len(shape))
    hspec = lambda w_: pl.BlockSpec((GH, tt, w_), lambda i: (0, i, 0))
    return pl.pallas_call(
        _gla_in_kernel,
        grid=(T // tt,),
        in_specs=[pl.BlockSpec((tt, D_MODEL), lambda i: (i, 0)), const(w.shape), const(wgu.shape), const(bg.shape)],
        out_specs=[hspec(GLA_DK), hspec(GLA_DK), hspec(GLA_DK), hspec(GLA_DV),
                   pl.BlockSpec((tt, GH * GLA_DV), lambda i: (i, 0))],
        out_shape=[
            jax.ShapeDtypeStruct((GH, T, GLA_DK), BF16),
            jax.ShapeDtypeStruct((GH, T, GLA_DK), BF16),
            jax.ShapeDtypeStruct((GH, T, GLA_DK), F32),
            jax.ShapeDtypeStruct((GH, T, GLA_DV), BF16),
            jax.ShapeDtypeStruct((T, GH * GLA_DV), BF16),
        ],
        compiler_params=_cparams(("arbitrary",)),
        name="gla_in",
    )(h, w, wgu, bg)


def _bdot(a, b, contract_b):
    return lax.dot_general(a, b, (((2,), (contract_b,)), ((0,), (0,))), preferred_element_type=F32)


def _gla_scan_kernel(q_ref, k_ref, g_ref, v_ref, stack_ref, o_ref, st_ref, oin_ref, qe_ref, kd_ref, dec_ref, *,
                     tile):
    C = SCAN_CHUNK
    nc = tile // C
    t = pl.program_id(1)

    @pl.when(t == 0)
    def _():
        st_ref[...] = jnp.zeros_like(st_ref)

    row = lax.broadcasted_iota(jnp.int32, (C, C), 0)
    col = lax.broadcasted_iota(jnp.int32, (C, C), 1)
    diag = (row == col)[None]
    masks = [(((row // (2 * c)) == (col // (2 * c))) & ((row % (2 * c)) >= c) & ((col % (2 * c)) < c))[None]
             for c in GLA_LEVELS]
    real = (t * tile + lax.broadcasted_iota(jnp.int32, (tile, 1), 0) >= PAD_ROWS).reshape(nc, C, 1)
    stack = stack_ref[...]

    for hh in range(GLA_HEADS):
        q = q_ref[hh].astype(F32).reshape(nc, C, GLA_DK)
        k = jnp.where(real, k_ref[hh].astype(F32).reshape(nc, C, GLA_DK), 0.0)
        g = g_ref[hh].reshape(nc, C, GLA_DK)
        v = v_ref[hh].reshape(nc, C, GLA_DV)
        g_hi = g.astype(BF16)
        g_lo = (g - g_hi.astype(F32)).astype(BF16)
        sums2 = _bdot(stack, jnp.concatenate([g_hi, g_lo], axis=2), 1)
        sums = sums2[:, :, :GLA_DK] + sums2[:, :, GLA_DK:]
        b = sums[:, 0:C]
        b_last = b[:, C - 1:C]
        att = jnp.where(diag, _bdot(q.astype(BF16), k.astype(BF16), 2), 0.0)
        for li in range(len(GLA_LEVELS)):
            ref = sums[:, C * (li + 1):C * (li + 2)]
            qf = (q * jnp.exp(jnp.minimum(b - ref, 0.0))).astype(BF16)
            kb = (k * jnp.exp(jnp.minimum(ref - b, 0.0))).astype(BF16)
            att = att + jnp.where(masks[li], _bdot(qf, kb, 2), 0.0)
        oin_ref[hh] = _bdot(att.astype(BF16), v, 1).reshape(tile, GLA_DV)
        qe_ref[hh] = (q * jnp.exp(b)).astype(BF16).reshape(tile, GLA_DK)
        kd_ref[hh] = (k * jnp.exp(b_last - b)).astype(BF16).reshape(tile, GLA_DK)
        dec_ref[hh] = jnp.broadcast_to(jnp.exp(b_last), (nc, 8, GLA_DK))

    def chunk(ci, carry):
        r0 = pl.multiple_of(ci * C, C)
        for hh in range(GLA_HEADS):
            st = st_ref[hh]
            o = oin_ref[hh, pl.ds(r0, C), :] + _nt_dot(qe_ref[hh, pl.ds(r0, C), :], st.astype(BF16))
            o_ref[pl.ds(r0, C), GLA_DV * hh:GLA_DV * (hh + 1)] = o.astype(BF16)
            st_ref[hh] = st * dec_ref[hh, ci][0:1] + _tn_dot(v_ref[hh, pl.ds(r0, C), :], kd_ref[hh, pl.ds(r0, C), :])
        return carry

    lax.fori_loop(0, nc, chunk, 0)


def _gla_scan(q, k, g, v, stack, B, LK):
    GH, T, _ = q.shape
    tile = 640 if LK % 640 == 0 else LANES
    nt = LK // tile
    nc = tile // SCAN_CHUNK
    stack3 = jnp.broadcast_to(stack[None], (nc,) + stack.shape)
    hspec = lambda w_: pl.BlockSpec((GH, tile, w_), lambda b, t: (0, b * nt + t, 0))
    return pl.pallas_call(
        functools.partial(_gla_scan_kernel, tile=tile),
        grid=(B, nt),
        in_specs=[hspec(GLA_DK), hspec(GLA_DK), hspec(GLA_DK), hspec(GLA_DV),
                  pl.BlockSpec(stack3.shape, lambda b, t: (0, 0, 0))],
        out_specs=pl.BlockSpec((tile, GH * GLA_DV), lambda b, t: (b * nt + t, 0)),
        out_shape=jax.ShapeDtypeStruct((T, GH * GLA_DV), BF16),
        scratch_shapes=[
            pltpu.VMEM((GH, GLA_DV, GLA_DK), F32),
            pltpu.VMEM((GH, tile, GLA_DV), F32),
            pltpu.VMEM((GH, tile, GLA_DK), BF16),
            pltpu.VMEM((GH, tile, GLA_DK), BF16),
            pltpu.VMEM((GH, nc, 8, GLA_DK), F32),
        ],
        compiler_params=_cparams(("arbitrary", "arbitrary")),
        name="gla_scan",
    )(q, k, g, v, stack3)


def _gla_stack():
    C = SCAN_CHUNK
    t = jnp.arange(C)[:, None]
    u = jnp.arange(C)[None, :]
    mats = [u <= t]
    for c in GLA_LEVELS:
        boundary = (t // (2 * c)) * (2 * c) + c - 1
        mats.append(u <= boundary)
    return jnp.concatenate(mats, axis=0).astype(BF16)


def _router_weights(w_rg, b_rg, w_re, b_re):
    w = jnp.zeros((D_MODEL, LANES), F32).at[:, :N_EXPERTS].set(w_re).at[:, N_EXPERTS:N_EXPERTS + N_GROUPS].set(w_rg)
    b = jnp.zeros((1, LANES), F32).at[0, :N_EXPERTS].set(b_re).at[0, N_EXPERTS:N_EXPERTS + N_GROUPS].set(b_rg)
    w_hi = w.astype(BF16)
    w_lo = (w - w_hi.astype(F32)).astype(BF16)
    return w_hi, w_lo, b


def kernel(x, meta_tokens, norm_mix, norm_ffn, mla_w_in, mla_g_q, mla_w_uq, mla_g_kv, mla_w_ukv, mla_g_qn, mla_g_kn, mla_w_o, gla_w_in, gla_w_gate_up, gla_b_gate, gla_g_out, gla_w_o, moe_w_rg, moe_b_rg, moe_w_re, moe_b_re, moe_w_gate, moe_w_up, moe_w_down):
    B, S, D = x.shape
    assert D == D_MODEL and S % (2 * ATT_TILE) == 0
    LK = META_BLOCK + S
    T = B * LK
    assert T % TOK_TILE == 0 and TOK_TILE % LANES == 0
    tt = TOK_TILE
    H = MLA_HEADS

    meta = jnp.broadcast_to(meta_tokens.astype(F32)[None], (B, N_META, D))
    h = jnp.pad(x.astype(F32), ((0, 0), (META_BLOCK, 0), (0, 0))).at[:, PAD_ROWS:META_BLOCK].set(meta)
    h = h.reshape(T, D)

    rows = jnp.arange(LK)
    pos = jnp.where(rows < META_BLOCK, jnp.maximum(rows - PAD_ROWS, 0), rows - META_BLOCK + N_META)
    half = MLA_ROPE // 2
    inv = 1.0 / (ROPE_THETA ** (jnp.arange(half, dtype=F32) / half))
    ang = inv[:, None] * pos.astype(F32)[None, :]
    cos = jnp.tile(jnp.cos(ang), (1, B))
    sin = jnp.tile(jnp.sin(ang), (1, B))

    tri = (jnp.arange(tt)[None, :] < jnp.arange(tt)[:, None]).astype(BF16)

    winT = (mla_w_in[0] * norm_mix[0][:, None]).T.astype(BF16)
    wuqT = (mla_w_uq[0] * mla_g_q[0][:, None]).T.astype(BF16)
    wukv = (mla_w_ukv[0] * mla_g_kv[0][:, None]).reshape(MLA_KV_RANK, H, MLA_NOPE + MLA_V)
    wkT = wukv[:, :, :MLA_NOPE].reshape(MLA_KV_RANK, H * MLA_NOPE).T.astype(BF16)
    wv = jnp.zeros((MLA_KV_RANK, H, LANES), F32).at[:, :, :MLA_V].set(wukv[:, :, MLA_NOPE:])
    wv = wv.reshape(MLA_KV_RANK, H * LANES).astype(BF16)
    vone = jnp.zeros((1, H, LANES), F32).at[:, :, MLA_V].set(1.0).reshape(1, H * LANES)
    gq = jnp.broadcast_to((mla_g_qn[0] * (MLA_QK ** -0.5 * LOG2E))[:, None], (MLA_QK, tt)).astype(F32)
    gk = jnp.broadcast_to(mla_g_kn[0][:, None], (MLA_QK, tt)).astype(F32)
    q, kT, v, stats = _mla_proj(h, winT, wuqT, wkT, wv, vone, gq, gk, cos, sin)
    att = _attention(q, kT, v, stats, B, LK)

    wrh, wrl, br = _router_weights(moe_w_rg[0], moe_b_rg[0], moe_w_re[0], moe_b_re[0])
    h, hn, info, cnt = _mixer_out(h, att, mla_w_o[0].astype(BF16), norm_ffn[0][None, :], wrh, wrl, br, tri)
    wgu = jnp.concatenate([moe_w_gate[0], moe_w_up[0]], axis=-1).astype(BF16)
    xs0 = jnp.zeros(((T // EXP_TILE + N_CLASSES) * EXP_TILE, ROW_EXT), F32)
    h, xs1 = _moe(h, hn, info, cnt, wgu, moe_w_down[0].astype(BF16), xs0)

    nk = GLA_HEADS * GLA_DK
    nv = GLA_HEADS * GLA_DV
    w1 = gla_w_in[0] * norm_mix[1][:, None]
    wg_pad = jnp.zeros((D, LANES), F32).at[:, :GLA_GATE_RANK].set(w1[:, 2 * nk + nv:2 * nk + nv + GLA_GATE_RANK])
    w_all = jnp.concatenate([w1[:, :nk] * (GLA_DK ** -0.5), w1[:, nk:2 * nk + nv], wg_pad,
                             w1[:, 2 * nk + nv + GLA_GATE_RANK:]], axis=1).astype(BF16)
    wgate = jnp.zeros((LANES, nk), F32).at[:GLA_GATE_RANK].set(gla_w_gate_up[0]).astype(BF16)
    gq_, gk_, gg_, gv_, gr_ = _gla_in(h, w_all, wgate, gla_b_gate[0][None, :])
    go = _gla_scan(gq_, gk_, gg_, gv_, _gla_stack(), B, LK)

    wrh, wrl, br = _router_weights(moe_w_rg[1], moe_b_rg[1], moe_w_re[1], moe_b_re[1])
    h, hn, info, cnt = _mixer_out(h, go, gla_w_o[0].astype(BF16), norm_ffn[1][None, :], wrh, wrl, br, tri,
                                  gla_extra=(gr_, gla_g_out[0][None, :]))
    wgu = jnp.concatenate([moe_w_gate[1], moe_w_up[1]], axis=-1).astype(BF16)
    out, _ = _moe(h, hn, info, cnt, wgu, moe_w_down[1].astype(BF16), xs1, frames_of=(B, LK))
    return out.reshape(B, S, D).astype(x.dtype)
```
